```python
import math, functools
import jax, jax.numpy as jnp
from jax import lax
import numpy as np

D_MODEL = 1024
BATCH = 4
SEQ = 4096
DEPTH = 4
DEC_BATCH = 32
DEC_SEQ = 1
PAST_LEN = 8192
PAGE_SIZE = 128

D_MIX = D_MODEL
D_CONV = D_MIX // 2
N_HEADS = 8
HEAD_DIM = (D_MIX - D_CONV) // N_HEADS
D_ATT = N_HEADS * HEAD_DIM
D_IN = 3 * D_CONV + 3 * D_ATT
CONV_WIDTH = 3
DILATED_PATTERNS = ((128, 1), (512, 4), (2048, 16))
MAX_WINDOW = max(w for w, _ in DILATED_PATTERNS)
Q_BLOCK = 128
F_DENSE = 2816
N_EXPERTS = 8
TOP_K = 2
F_EXPERT = 3584
N_DENSE = (DEPTH + 1) // 2
N_MOE = DEPTH // 2
ALPHA = (2 * DEPTH) ** 0.25
BETA = (8 * DEPTH) ** -0.25
LN_EPS = 1e-5

kernel_name = 'hybrid_conv_dilated_alibi_deepnorm_step'


def alibi_slopes():
    return jnp.exp2(-8.0 * jnp.arange(1, N_HEADS + 1, dtype=jnp.float32) / N_HEADS)


def layer_norm(x):
    xf = x.astype(jnp.float32)
    mu = xf.mean(-1, keepdims=True)
    var = jnp.mean(jnp.square(xf - mu), -1, keepdims=True)
    return ((xf - mu) * lax.rsqrt(var + LN_EPS)).astype(x.dtype)


def layer_norm_affine(x, g, b):
    xf = x.astype(jnp.float32)
    mu = xf.mean(-1, keepdims=True)
    var = jnp.mean(jnp.square(xf - mu), -1, keepdims=True)
    return ((xf - mu) * lax.rsqrt(var + LN_EPS) * g + b).astype(x.dtype)


def rms_norm(x, g):
    xf = x.astype(jnp.float32)
    return (xf * lax.rsqrt(jnp.mean(xf * xf, -1, keepdims=True) + LN_EPS) * g).astype(x.dtype)


def adaln_params(c, w, b):
    m = jnp.einsum('bd,de->be', jax.nn.silu(c), w) + b
    return jnp.split(m[:, None, :], 6, axis=-1)


def modulate(x, shift, scale):
    return layer_norm(x) * (1 + scale) + shift


def post_norm(x, h, gate, g, b):
    return layer_norm_affine(ALPHA * x + (1 + gate) * h, g, b)


def project_in(u, w):
    zz = jnp.einsum('btd,de->bte', u, w)
    b_gate, c_gate, h, q, k, v = jnp.split(
        zz, [D_CONV, 2 * D_CONV, 3 * D_CONV, 3 * D_CONV + D_ATT, 3 * D_CONV + 2 * D_ATT], axis=-1)
    heads = lambda a: a.reshape(a.shape[0], a.shape[1], N_HEADS, HEAD_DIM)
    return b_gate, c_gate * h, heads(q), heads(k), heads(v)


def causal_conv(z_ext, w):
    t = z_ext.shape[1] - (CONV_WIDTH - 1)
    return sum(w[j] * z_ext[:, j:j + t] for j in range(CONV_WIDTH))


def project_out(b_gate, conv_y, att, g_conv, g_att, w):
    y_conv = rms_norm(b_gate * conv_y, g_conv)
    y_att = rms_norm(att.reshape(att.shape[0], att.shape[1], D_ATT), g_att)
    return jnp.einsum('bte,ed->btd', jnp.concatenate([y_conv, y_att], axis=-1), w)


def combine_patterns(nums, dens, maxs):
    m_all = functools.reduce(jnp.maximum, maxs)
    wts = [d * jnp.exp(m - m_all) for d, m in zip(dens, maxs)]
    total = sum(wts)
    return sum((wi / total)[..., None] * (n / d[..., None]) for wi, n, d in zip(wts, nums, dens))


def dilated_attention_prompt(q, k, v, slopes):
    bsz, s_len = q.shape[0], q.shape[1]
    scale = HEAD_DIM ** -0.5
    nums, dens, maxs = [], [], []
    for window, dil in DILATED_PATTERNS:
        steps = window // dil
        n_sub = s_len // dil
        nb = -(-n_sub // Q_BLOCK)
        n_pad = nb * Q_BLOCK

        def to_blocks(a):
            a = a.reshape(bsz, n_sub, dil, N_HEADS, HEAD_DIM)
            a = jnp.pad(a, ((0, 0), (0, n_pad - n_sub), (0, 0), (0, 0), (0, 0)))
            return a.reshape(bsz, nb, Q_BLOCK, dil, N_HEADS, HEAD_DIM)

        def with_prev(a):
            prev = jnp.pad(a, ((0, 0), (1, 0), (0, 0), (0, 0), (0, 0), (0, 0)))[:, :-1]
            return jnp.concatenate([prev, a], axis=2)

        qb = to_blocks(q)
        kc = with_prev(to_blocks(k))
        vc = with_prev(to_blocks(v))
        s = jnp.einsum('bnqrhd,bnkrhd->bnrhqk', qb, kc,
                       preferred_element_type=jnp.float32) * scale
        i = jnp.arange(Q_BLOCK)[:, None]
        j = jnp.arange(2 * Q_BLOCK)[None, :]
        step = Q_BLOCK + i - j
        blk = jnp.arange(nb)[:, None, None]
        valid = (step >= 0) & (step <= steps) & ((blk >= 1) | (j >= Q_BLOCK))
        bias = -slopes[:, None, None] * (dil * step).astype(jnp.float32)[None]
        s = jnp.where(valid[None, :, None, None], s + bias[None, None, None], -jnp.inf)
        m = s.max(-1)
        p = jnp.exp(s - m[..., None])
        den = p.sum(-1)
        num = jnp.einsum('bnrhqk,bnkrhd->bnqrhd', p, vc.astype(jnp.float32))

        def stat_to_pos(a):
            a = jnp.transpose(a, (0, 1, 4, 2, 3)).reshape(bsz, n_pad, dil, N_HEADS)
            return a[:, :n_sub].reshape(bsz, s_len, N_HEADS)

        nums.append(num.reshape(bsz, n_pad, dil, N_HEADS, HEAD_DIM)[:, :n_sub]
                    .reshape(bsz, s_len, N_HEADS, HEAD_DIM))
        dens.append(stat_to_pos(den))
        maxs.append(stat_to_pos(m))
    return combine_patterns(nums, dens, maxs).astype(q.dtype)


def dilated_attention_sample(q, k, v, k_past, v_past, slopes):
    t_new = q.shape[1]
    w_buf = k_past.shape[1]
    k_all = jnp.concatenate([k_past, k], axis=1)
    v_all = jnp.concatenate([v_past, v], axis=1)
    scale = HEAD_DIM ** -0.5
    nums, dens, maxs = [], [], []
    for window, dil in DILATED_PATTERNS:
        steps = window // dil
        jj = jnp.arange(steps + 1)
        idx = (w_buf + jnp.arange(t_new))[:, None] - dil * jj[None, :]
        valid = idx >= 0
        idxc = jnp.maximum(idx, 0)
        kg = k_all[:, idxc]
        vg = v_all[:, idxc]
        s = jnp.einsum('bthd,btjhd->bhtj', q, kg, preferred_element_type=jnp.float32) * scale
        s = s - slopes[:, None, None] * (dil * jj).astype(jnp.float32)[None, None, :]
        s = jnp.where(valid[None, None], s, -jnp.inf)
        m = s.max(-1)
        p = jnp.exp(s - m[..., None])
        nums.append(jnp.einsum('bhtj,btjhd->bthd', p, vg.astype(jnp.float32)))
        dens.append(jnp.transpose(p.sum(-1), (0, 2, 1)))
        maxs.append(jnp.transpose(m, (0, 2, 1)))
    return combine_patterns(nums, dens, maxs).astype(q.dtype)


def swiglu(u, w1, w2):
    g, up = jnp.split(jnp.einsum('btd,df->btf', u, w1), 2, axis=-1)
    return jnp.einsum('btf,fd->btd', jax.nn.silu(g) * up, w2)


def moe_swiglu(u, w_router, w_e1, w_e2):
    logits = jnp.einsum('btd,de->bte', u, w_router, preferred_element_type=jnp.float32)
    top_v, top_i = lax.top_k(logits, TOP_K)
    gates = jax.nn.softmax(top_v, axis=-1)
    combine = jnp.einsum('btk,btke->bte', gates,
                         jax.nn.one_hot(top_i, N_EXPERTS, dtype=jnp.float32))
    out = jnp.zeros_like(u)
    for e in range(N_EXPERTS):
        out = out + combine[..., e:e + 1].astype(u.dtype) * swiglu(u, w_e1[e], w_e2[e])
    return out


def hybrid_layer(x, c, conv_prefix, attend, channel_mixer, w_ada, b_ada, w_in, conv_w,
                 g_conv_out, g_att_out, w_out, ln1_g, ln1_b, ln2_g, ln2_b):
    shift1, scale1, gate1, shift2, scale2, gate2 = adaln_params(c, w_ada, b_ada)
    u = modulate(x, shift1, scale1)
    b_gate, z, q, k, v = project_in(u, w_in)
    z_ext = jnp.concatenate([conv_prefix.astype(z.dtype), z], axis=1)
    conv_y = causal_conv(z_ext, conv_w)
    att = attend(q, k, v)
    h = project_out(b_gate, conv_y, att, g_conv_out, g_att_out, w_out)
    x = post_norm(x, h, gate1, ln1_g, ln1_b)
    u = modulate(x, shift2, scale2)
    x = post_norm(x, channel_mixer(u), gate2, ln2_g, ln2_b)
    return x, z_ext[:, -(CONV_WIDTH - 1):], k, v


def setup_inputs(seed: int = 0) -> dict:
    key = jax.random.key(seed)
    ks = jax.random.split(key, 24)
    f32 = jnp.float32
    nrm = lambda kk, shape, s: jax.random.normal(kk, shape, f32) * s
    w_buf = min(MAX_WINDOW, PAST_LEN)
    return {
        'x_prompt': nrm(ks[0], (BATCH, SEQ, D_MODEL), 1.0),
        'x_sample': nrm(ks[1], (DEC_BATCH, DEC_SEQ, D_MODEL), 1.0),
        'cache_k': nrm(ks[2], (DEPTH, DEC_BATCH, w_buf, N_HEADS, HEAD_DIM), 1.0),
        'cache_v': nrm(ks[3], (DEPTH, DEC_BATCH, w_buf, N_HEADS, HEAD_DIM), 1.0),
        'state_conv': nrm(ks[4], (DEPTH, DEC_BATCH, CONV_WIDTH - 1, D_CONV), 1.0),
        'c_prompt': nrm(ks[5], (BATCH, D_MODEL), 1.0),
        'c_sample': nrm(ks[6], (DEC_BATCH, D_MODEL), 1.0),
        'w_ada': nrm(ks[7], (DEPTH, D_MODEL, 6 * D_MODEL), 0.1 * D_MODEL ** -0.5),
        'b_ada': nrm(ks[8], (DEPTH, 6 * D_MODEL), 0.02),
        'w_in': nrm(ks[9], (DEPTH, D_MODEL, D_IN), D_MODEL ** -0.5),
        'conv_w': nrm(ks[10], (DEPTH, CONV_WIDTH, D_CONV), CONV_WIDTH ** -0.5),
        'g_conv_out': 1.0 + nrm(ks[11], (DEPTH, D_CONV), 0.02),
        'g_att_out': 1.0 + nrm(ks[12], (DEPTH, D_ATT), 0.02),
        'w_out': nrm(ks[13], (DEPTH, D_MIX, D_MODEL), BETA * D_MIX ** -0.5),
        'ln1_g': 1.0 + nrm(ks[14], (DEPTH, D_MODEL), 0.02),
        'ln1_b': nrm(ks[15], (DEPTH, D_MODEL), 0.02),
        'ln2_g': 1.0 + nrm(ks[16], (DEPTH, D_MODEL), 0.02),
        'ln2_b': nrm(ks[17], (DEPTH, D_MODEL), 0.02),
        'w_ff1': nrm(ks[18], (N_DENSE, D_MODEL, 2 * F_DENSE), D_MODEL ** -0.5),
        'w_ff2': nrm(ks[19], (N_DENSE, F_DENSE, D_MODEL), BETA * F_DENSE ** -0.5),
        'w_router': nrm(ks[20], (N_MOE, D_MODEL, N_EXPERTS), D_MODEL ** -0.5),
        'w_e1': nrm(ks[21], (N_MOE, N_EXPERTS, D_MODEL, 2 * F_EXPERT), D_MODEL ** -0.5),
        'w_e2': nrm(ks[22], (N_MOE, N_EXPERTS, F_EXPERT, D_MODEL), BETA * F_EXPERT ** -0.5),
    }


def reference(x_prompt, x_sample, cache_k, cache_v, state_conv, c_prompt, c_sample,
              w_ada, b_ada, w_in, conv_w, g_conv_out, g_att_out, w_out,
              ln1_g, ln1_b, ln2_g, ln2_b, w_ff1, w_ff2, w_router, w_e1, w_e2):
    slopes = alibi_slopes()
    xp, xs = x_prompt, x_sample
    keep = min(MAX_WINDOW, xp.shape[1])
    k_p, v_p, cv_p, k_s, v_s, cv_s = [], [], [], [], [], []
    for l in range(DEPTH):
        if l % 2 == 0:
            ffn = functools.partial(swiglu, w1=w_ff1[l // 2], w2=w_ff2[l // 2])
        else:
            ffn = functools.partial(moe_swiglu, w_router=w_router[l // 2],
                                    w_e1=w_e1[l // 2], w_e2=w_e2[l // 2])
        shared = (w_ada[l], b_ada[l], w_in[l], conv_w[l], g_conv_out[l], g_att_out[l],
                  w_out[l], ln1_g[l], ln1_b[l], ln2_g[l], ln2_b[l])
        prefix_p = jnp.zeros((xp.shape[0], CONV_WIDTH - 1, D_CONV), xp.dtype)
        attend_p = functools.partial(dilated_attention_prompt, slopes=slopes)
        xp, conv_p, kp, vp = hybrid_layer(xp, c_prompt, prefix_p, attend_p, ffn, *shared)
        attend_s = functools.partial(dilated_attention_sample, k_past=cache_k[l],
                                     v_past=cache_v[l], slopes=slopes)
        xs, conv_s, ksn, vsn = hybrid_layer(xs, c_sample, state_conv[l], attend_s, ffn, *shared)
        k_p.append(kp[:, -keep:])
        v_p.append(vp[:, -keep:])
        cv_p.append(conv_p)
        k_s.append(ksn)
        v_s.append(vsn)
        cv_s.append(conv_s)
    return (xp, xs, jnp.stack(k_p), jnp.stack(v_p), jnp.stack(cv_p),
            jnp.stack(k_s), jnp.stack(v_s), jnp.stack(cv_s))
```

```python
import functools

import jax
import jax.numpy as jnp
from jax import lax
from jax.experimental import pallas as pl
from jax.experimental.pallas import tpu as pltpu

F32 = jnp.float32
BF16 = jnp.bfloat16

N_HEADS = 8
HEAD_DIM = 64
CONV_WIDTH = 3
DILATED_PATTERNS = ((128, 1), (512, 4), (2048, 16))
Q_BLOCK = 128
N_EXPERTS = 8
TOP_K = 2
LN_EPS = 1e-5

LANES = 128
SUBLANES = 8
MOE_BLOCK = 256
MOE_TILE = 1024
VMEM_LIMIT = 56 * 1024 * 1024


def _cparams(sem, vmem=VMEM_LIMIT):
    return pltpu.CompilerParams(dimension_semantics=sem, vmem_limit_bytes=vmem)


def _ln(x):
    mu = jnp.mean(x, axis=-1, keepdims=True)
    xc = x - mu
    var = jnp.mean(xc * xc, axis=-1, keepdims=True)
    return xc * lax.rsqrt(var + LN_EPS)


def _silu(x):
    return x * jax.nn.sigmoid(x)


def _mod_rows(mod_ref, k, per_row):
    return mod_ref[k] if per_row else mod_ref[0, k:k + 1, :]


def _mod_spec(per_row, rows, d, tiles_per_seq):
    if per_row:
        return pl.BlockSpec((6, rows, d), lambda i, *_: (0, 0, 0))
    return pl.BlockSpec((1, 6, d), lambda i, *_: (i // tiles_per_seq, 0, 0))


def _ada_kernel(c_ref, w_ref, b_ref, o_ref):
    c = c_ref[...]
    s = _silu(c).astype(BF16)
    o_ref[0] = jnp.dot(s, w_ref[0].astype(BF16), preferred_element_type=F32) + b_ref[0]


def _adaln_all(c_all, w_ada, b_ada):
    depth, d, e6 = w_ada.shape
    rows = c_all.shape[0]
    tn = e6 // 4
    return pl.pallas_call(
        _ada_kernel,
        out_shape=jax.ShapeDtypeStruct((depth, rows, e6), F32),
        grid=(depth, e6 // tn),
        in_specs=[pl.BlockSpec((rows, d), lambda l, j: (0, 0)),
                  pl.BlockSpec((1, d, tn), lambda l, j: (l, 0, j)),
                  pl.BlockSpec((1, 1, tn), lambda l, j: (l, 0, j))],
        out_specs=pl.BlockSpec((1, rows, tn), lambda l, j: (l, 0, j)),
        compiler_params=_cparams(("arbitrary", "arbitrary")),
        name="adaln",
    )(c_all, w_ada, b_ada.reshape(depth, 1, e6))


def _inproj_kernel(x_ref, mod_ref, w_ref, bg_ref, z_ref, q_ref, k_ref, v_ref, *, per_row, dc):
    shift = _mod_rows(mod_ref, 0, per_row)
    scale = _mod_rows(mod_ref, 1, per_row)
    u = (_ln(x_ref[...]) * (1.0 + scale) + shift).astype(BF16)

    def proj(j):
        return jnp.dot(u, w_ref[:, j * dc:(j + 1) * dc], preferred_element_type=F32)

    bg_ref[...] = proj(0)
    z_ref[...] = proj(1) * proj(2)
    q_ref[...] = proj(3)
    k_ref[...] = proj(4)
    v_ref[...] = proj(5)


def _in_proj(x, mod, w_in_b, *, per_row, tm, tiles_per_seq):
    n, d = x.shape
    dc = w_in_b.shape[1] // 6
    row = lambda: pl.BlockSpec((tm, dc), lambda i: (i, 0))
    return pl.pallas_call(
        functools.partial(_inproj_kernel, per_row=per_row, dc=dc),
        out_shape=[jax.ShapeDtypeStruct((n, dc), F32)] * 5,
        grid=(n // tm,),
        in_specs=[pl.BlockSpec((tm, d), lambda i: (i, 0)),
                  _mod_spec(per_row, tm, d, tiles_per_seq),
                  pl.BlockSpec(w_in_b.shape, lambda i: (0, 0), pipeline_mode=pl.Buffered(1))],
        out_specs=[row() for _ in range(5)],
        compiler_params=_cparams(("arbitrary",)),
        name="in_proj",
    )(x, mod, w_in_b)


def _attn_kernel(q_ref, k_ref, v_ref, sl_ref, o_ref, m_scr, l_scr, bias_scr):
    s_len = q_ref.shape[0]
    qb = Q_BLOCK
    half = lax.broadcasted_iota(jnp.int32, (qb, LANES), 1) >= HEAD_DIM
    ri = lax.broadcasted_iota(jnp.int32, (2 * qb, 2 * qb), 0)
    ji = lax.broadcasted_iota(jnp.int32, (2 * qb, 2 * qb), 1)
    step = qb + (ri % qb) - ji
    band = (step >= 0) & (step <= qb)
    slope = jnp.where(ri < qb, sl_ref[0, 0:1, :], sl_ref[0, 1:2, :])
    for pi, (_, d) in enumerate(DILATED_PATTERNS):
        bias_scr[pi] = jnp.where(band, -(slope * (d * step).astype(F32)), -jnp.inf)
    first_cols = lax.broadcasted_iota(jnp.int32, (1, 2 * qb), 1) < qb
    scale = HEAD_DIM ** -0.5

    for pi, (_, d) in enumerate(DILATED_PATTERNS):
        n_blocks = s_len // (qb * d)

        def body(idx, carry, d=d, pi=pi):
            r = idx % d
            n = idx // d
            base = n * (qb * d) + r
            prev = jnp.maximum(n - 1, 0) * (qb * d) + r
            rows = pl.ds(base, qb, stride=d)
            prow = pl.ds(prev, qb, stride=d)
            q = q_ref[rows, :] * scale
            q2 = jnp.concatenate([jnp.where(half, 0.0, q), jnp.where(half, q, 0.0)], axis=0).astype(BF16)
            k2 = jnp.concatenate([k_ref[prow, :], k_ref[rows, :]], axis=0).astype(BF16)
            v2 = jnp.concatenate([v_ref[prow, :], v_ref[rows, :]], axis=0).astype(BF16)
            pen = jnp.where(first_cols, jnp.where(n == 0, -jnp.inf, 0.0), 0.0)
            s = lax.dot_general(q2, k2, (((1,), (1,)), ((), ())), preferred_element_type=F32)
            s = s + bias_scr[pi] + pen
            s0, s1 = s[:, :qb], s[:, qb:]
            mb = jnp.max(jnp.maximum(s0, s1), axis=1, keepdims=True)
            if pi == 0:
                m_new = jnp.broadcast_to(mb, (2 * qb, LANES))
            else:
                m_old = jnp.concatenate([m_scr[0, rows, :], m_scr[1, rows, :]], axis=0)
                m_new = jnp.maximum(m_old, mb)
            p0 = jnp.exp(s0 - m_new)
            p1 = jnp.exp(s1 - m_new)
            rs = jnp.sum(p0 + p1, axis=1, keepdims=True)
            pv = jnp.dot(jnp.concatenate([p0, p1], axis=1).astype(BF16), v2, preferred_element_type=F32)
            if pi == 0:
                l_new = jnp.broadcast_to(rs, (2 * qb, LANES))
                acc = jnp.where(half, pv[qb:], pv[:qb])
            else:
                alpha = jnp.exp(m_old - m_new)
                l_old = jnp.concatenate([l_scr[0, rows, :], l_scr[1, rows, :]], axis=0)
                l_new = alpha * l_old + rs
                acc_old = o_ref[rows, :]
                acc = jnp.where(half, alpha[qb:] * acc_old + pv[qb:], alpha[:qb] * acc_old + pv[:qb])
            m_scr[0, rows, :] = m_new[:qb]
            m_scr[1, rows, :] = m_new[qb:]
            l_scr[0, rows, :] = l_new[:qb]
            l_scr[1, rows, :] = l_new[qb:]
            o_ref[rows, :] = acc
            return carry

        lax.fori_loop(0, n_blocks * d, body, 0)

    half_s = lax.broadcasted_iota(jnp.int32, (s_len, LANES), 1) >= HEAD_DIM
    o_ref[...] = o_ref[...] / jnp.where(half_s, l_scr[1], l_scr[0])


def _prompt_attention(q, k, v, slopes_pairs, bsz, s_len):
    da = q.shape[1]
    pairs = da // LANES
    spec = lambda: pl.BlockSpec((None, s_len, LANES), lambda b, h: (b, 0, h))
    rs = lambda a: a.reshape(bsz, s_len, da)
    out = pl.pallas_call(
        _attn_kernel,
        out_shape=jax.ShapeDtypeStruct((bsz, s_len, da), F32),
        grid=(bsz, pairs),
        in_specs=[spec(), spec(), spec(),
                  pl.BlockSpec((1, 2, 2 * Q_BLOCK), lambda b, h: (h, 0, 0))],
        out_specs=spec(),
        scratch_shapes=[pltpu.VMEM((2, s_len, LANES), F32), pltpu.VMEM((2, s_len, LANES), F32),
                        pltpu.VMEM((len(DILATED_PATTERNS), 2 * Q_BLOCK, 2 * Q_BLOCK), F32)],
        compiler_params=_cparams(("arbitrary", "arbitrary")),
        name="prompt_attention",
    )(rs(q), rs(k), rs(v), slopes_pairs)
    return out.reshape(bsz * s_len, da)


def _sattn_kernel(q_ref, kn_ref, vn_ref, kt_ref, vt_ref, sl_ref, o_ref):
    bb, nh, _, w = kt_ref.shape
    dist = w - lax.broadcasted_iota(jnp.int32, (1, w), 1)
    count = jnp.zeros((1, w), F32)
    for window, d in DILATED_PATTERNS:
        count = count + jnp.where((dist % d == 0) & (dist <= window), 1.0, 0.0)
    distf = dist.astype(F32)
    n_pat = float(len(DILATED_PATTERNS))
    scale = HEAD_DIM ** -0.5

    def body(i, carry):
        b = i // nh
        h = i % nh
        q = q_ref[b, h] * scale
        s = jnp.sum(kt_ref[b, h] * q, axis=0, keepdims=True)
        s = jnp.where(count > 0.0, s - sl_ref[h] * distf, -jnp.inf)
        s_self = jnp.sum(q * kn_ref[b, h], axis=0, keepdims=True)
        m = jnp.maximum(jnp.max(s, axis=1, keepdims=True), s_self)
        p = jnp.exp(s - m) * count
        p_self = n_pat * jnp.exp(s_self - m)
        den = jnp.sum(p, axis=1, keepdims=True) + p_self
        num = jnp.sum(vt_ref[b, h] * p, axis=1, keepdims=True) + p_self * vn_ref[b, h]
        o_ref[b, h] = num / den
        return carry

    lax.fori_loop(0, bb * nh, body, 0)


def _sample_attention(q, k, v, cache_kt, cache_vt, layer, slopes_col, *, bb=2):
    nb, da = q.shape
    _, nh, hd, w_buf = cache_kt.shape
    cols = lambda a: a.reshape(nb, nh, hd, 1)
    col = lambda: pl.BlockSpec((bb, nh, hd, 1), lambda i: (i, 0, 0, 0))
    steps = nb // bb
    off = layer * steps
    win = lambda: pl.BlockSpec((bb, nh, hd, w_buf), lambda i: (off + i, 0, 0, 0))
    out = pl.pallas_call(
        _sattn_kernel,
        out_shape=jax.ShapeDtypeStruct((nb, nh, hd, 1), F32),
        grid=(steps,),
        in_specs=[col(), col(), col(), win(), win(), pl.BlockSpec(slopes_col.shape, lambda i: (0, 0, 0))],
        out_specs=col(),
        compiler_params=_cparams(("arbitrary",)),
        name="sample_attention",
    )(cols(q), cols(k), cols(v), cache_kt, cache_vt, slopes_col)
    return out.reshape(nb, da)


def _outproj_kernel(*refs, per_row, moe, router_rows, tiles_per_seq, alpha):
    it = iter(refs)
    bg_ref, z_ref, za_ref, zb_ref, att_ref, x_ref, mod_ref = (next(it) for _ in range(7))
    cw_ref, gc_ref, ga_ref, w_ref, lg_ref, lb_ref = (next(it) for _ in range(6))
    wr_ref = next(it) if moe else None
    x1_ref, u2_ref = next(it), next(it)
    if moe and router_rows:
        comb_ref = next(it)
    elif moe:
        comb_ref, li_ref, cum_ref, carry = (next(it) for _ in range(4))

    i = pl.program_id(0)
    z = z_ref[...]
    tm, dc = z.shape
    if per_row:
        z1, z2 = za_ref[...], zb_ref[...]
    else:
        keep = jnp.where(i % tiles_per_seq == 0, 0.0, 1.0)
        h1 = za_ref[7:8, :] * keep
        h2 = za_ref[6:7, :] * keep
        rid = lax.broadcasted_iota(jnp.int32, (tm, dc), 0)
        z1 = jnp.where(rid == 0, h1, pltpu.roll(z, 1, axis=0))
        z2 = jnp.where(rid == 0, h2, jnp.where(rid == 1, h1, pltpu.roll(z, 2, axis=0)))
    conv = cw_ref[0:1, :] * z2 + cw_ref[1:2, :] * z1 + cw_ref[2:3, :] * z
    yc = bg_ref[...] * conv
    yc = yc * lax.rsqrt(jnp.mean(yc * yc, axis=-1, keepdims=True) + LN_EPS) * gc_ref[...]
    ya = att_ref[...]
    ya = ya * lax.rsqrt(jnp.mean(ya * ya, axis=-1, keepdims=True) + LN_EPS) * ga_ref[...]
    h = (jnp.dot(yc.astype(BF16), w_ref[:dc, :], preferred_element_type=F32)
         + jnp.dot(ya.astype(BF16), w_ref[dc:, :], preferred_element_type=F32))
    gate1 = _mod_rows(mod_ref, 2, per_row)
    x1 = _ln(alpha * x_ref[...] + (1.0 + gate1) * h) * lg_ref[...] + lb_ref[...]
    x1_ref[...] = x1
    u2 = (_ln(x1) * (1.0 + _mod_rows(mod_ref, 4, per_row)) + _mod_rows(mod_ref, 3, per_row)).astype(BF16)
    u2_ref[...] = u2
    if not moe:
        return

    if router_rows:
        lg = jnp.dot(u2, wr_ref[...], preferred_element_type=F32)
        lane = lax.broadcasted_iota(jnp.int32, lg.shape, 1)
        lg = jnp.where(lane < N_EXPERTS, lg, -jnp.inf)
        m1 = jnp.max(lg, axis=1, keepdims=True)
        i1 = jnp.min(jnp.where(lg == m1, lane, LANES), axis=1, keepdims=True)
        lg2 = jnp.where(lane == i1, -jnp.inf, lg)
        m2 = jnp.max(lg2, axis=1, keepdims=True)
        i2 = jnp.min(jnp.where(lg2 == m2, lane, LANES), axis=1, keepdims=True)
        ex = jnp.exp(m2 - m1)
        g1 = 1.0 / (1.0 + ex)
        comb_ref[...] = jnp.where(lane == i1, g1, 0.0) + jnp.where(lane == i2, ex * g1, 0.0)
        return

    lg = lax.dot_general(wr_ref[...], u2, (((1,), (1,)), ((), ())), preferred_element_type=F32)
    row = lax.broadcasted_iota(jnp.int32, lg.shape, 0)
    lg = jnp.where(row < N_EXPERTS, lg, -jnp.inf)
    m1 = jnp.max(lg, axis=0, keepdims=True)
    i1 = jnp.min(jnp.where(lg == m1, row, lg.shape[0]), axis=0, keepdims=True)
    lg2 = jnp.where(row == i1, -jnp.inf, lg)
    m2 = jnp.max(lg2, axis=0, keepdims=True)
    i2 = jnp.min(jnp.where(lg2 == m2, row, lg.shape[0]), axis=0, keepdims=True)
    ex = jnp.exp(m2 - m1)
    g1 = 1.0 / (1.0 + ex)
    sel = (row == i1) | (row == i2)
    comb_ref[...] = jnp.where(row == i1, g1, 0.0) + jnp.where(row == i2, ex * g1, 0.0)
    onehot = jnp.where(sel, 1.0, 0.0)
    ta = lax.broadcasted_iota(jnp.int32, (tm, tm), 0)
    tb = lax.broadcasted_iota(jnp.int32, (tm, tm), 1)
    upper = jnp.where(ta < tb, 1.0, 0.0).astype(BF16)
    rank = jnp.dot(onehot.astype(BF16), upper, preferred_element_type=F32)
    li_ref[...] = jnp.where(sel, rank, -1.0)

    @pl.when(i == 0)
    def _():
        carry[...] = jnp.zeros_like(carry)

    count = jnp.sum(onehot, axis=1, keepdims=True)
    carry[...] = carry[...] + jnp.floor((count + (SUBLANES - 1.0)) * (1.0 / SUBLANES)) * SUBLANES
    cum_ref[...] = carry[...]


def _out_proj(bg, z, za, zb, att, x, mod, conv_w, g_conv, g_att, w_out_b, ln_g, ln_b, w_router_b,
              *, per_row, tm, tiles_per_seq, alpha):
    n, d = x.shape
    dc = z.shape[1]
    moe = w_router_b is not None
    router_rows = moe and per_row
    full = lambda a: pl.BlockSpec(a.shape, lambda i: (0,) * a.ndim)
    rowc = lambda: pl.BlockSpec((tm, dc), lambda i: (i, 0))
    rowd = lambda: pl.BlockSpec((tm, d), lambda i: (i, 0))
    if per_row:
        za_spec, zb_spec = rowc(), rowc()
    else:
        za_spec = pl.BlockSpec((8, dc), lambda i: (jnp.maximum(i * (tm // 8) - 1, 0), 0))
        zb_spec = pl.BlockSpec((8, dc), lambda i: (0, 0))
    args = [bg, z, za, zb, att, x, mod, conv_w, g_conv, g_att, w_out_b, ln_g, ln_b]
    in_specs = [rowc(), rowc(), za_spec, zb_spec, rowc(), rowd(), _mod_spec(per_row, tm, d, tiles_per_seq),
                full(conv_w), full(g_conv), full(g_att),
                pl.BlockSpec(w_out_b.shape, lambda i: (0, 0), pipeline_mode=pl.Buffered(1)),
                full(ln_g), full(ln_b)]
    out_shape = [jax.ShapeDtypeStruct((n, d), F32), jax.ShapeDtypeStruct((n, d), BF16)]
    out_specs = [rowd(), rowd()]
    scratch = []
    if moe:
        args.append(w_router_b)
        in_specs.append(full(w_router_b))
        if router_rows:
            out_shape.append(jax.ShapeDtypeStruct((n, LANES), F32))
            out_specs.append(pl.BlockSpec((tm, LANES), lambda i: (i, 0)))
        else:
            er = w_router_b.shape[0]
            out_shape += [jax.ShapeDtypeStruct((er, n), F32), jax.ShapeDtypeStruct((er, n), F32),
                          jax.ShapeDtypeStruct((er, (n // tm) * LANES), F32)]
            out_specs += [pl.BlockSpec((er, tm), lambda i: (0, i)), pl.BlockSpec((er, tm), lambda i: (0, i)),
                          pl.BlockSpec((er, LANES), lambda i: (0, i))]
            scratch.append(pltpu.VMEM((er, LANES), F32))
    return pl.pallas_call(
        functools.partial(_outproj_kernel, per_row=per_row, moe=moe, router_rows=router_rows,
                          tiles_per_seq=tiles_per_seq, alpha=alpha),
        out_shape=out_shape,
        grid=(n // tm,),
        in_specs=in_specs,
        out_specs=out_specs,
        scratch_shapes=scratch,
        compiler_params=_cparams(("arbitrary",)),
        name="out_proj",
    )(*args)


def _post_norm2(x1, h, mod_ref, lg_ref, lb_ref, per_row, alpha):
    gate2 = _mod_rows(mod_ref, 5, per_row)
    return _ln(alpha * x1 + (1.0 + gate2) * h) * lg_ref[...] + lb_ref[...]


def _ffn_kernel(u_ref, wg_ref, wu_ref, w2_ref, x1_ref, mod_ref, lg_ref, lb_ref, o_ref, acc,
                *, per_row, alpha):
    f = pl.program_id(1)

    @pl.when(f == 0)
    def _():
        acc[...] = jnp.zeros_like(acc)

    u = u_ref[...]
    g = jnp.dot(u, wg_ref[0].astype(BF16), preferred_element_type=F32)
    up = jnp.dot(u, wu_ref[0].astype(BF16), preferred_element_type=F32)
    h = (_silu(g) * up).astype(BF16)
    acc[...] += jnp.dot(h, w2_ref[0].astype(BF16), preferred_element_type=F32)

    @pl.when(f == pl.num_programs(1) - 1)
    def _():
        o_ref[...] = _post_norm2(x1_ref[...], acc[...], mod_ref, lg_ref, lb_ref, per_row, alpha)


def _dense_ffn(u2, w_ff1, w_ff2, layer, x1, mod, ln_g, ln_b, *, per_row, tm, tiles_per_seq, fc, alpha):
    n, d = x1.shape
    f_hidden = w_ff2.shape[1]
    nf = f_hidden // fc
    full = lambda a: pl.BlockSpec(a.shape, lambda i, f: (0,) * a.ndim)
    mod_spec = (pl.BlockSpec((6, tm, d), lambda i, f: (0, 0, 0)) if per_row
                else pl.BlockSpec((1, 6, d), lambda i, f: (i // tiles_per_seq, 0, 0)))
    return pl.pallas_call(
        functools.partial(_ffn_kernel, per_row=per_row, alpha=alpha),
        out_shape=jax.ShapeDtypeStruct((n, d), F32),
        grid=(n // tm, nf),
        in_specs=[pl.BlockSpec((tm, d), lambda i, f: (i, 0)),
                  pl.BlockSpec((1, d, fc), lambda i, f: (layer, 0, f)),
                  pl.BlockSpec((1, d, fc), lambda i, f: (layer, 0, nf + f)),
                  pl.BlockSpec((1, fc, d), lambda i, f: (layer, f, 0)),
                  pl.BlockSpec((tm, d), lambda i, f: (i, 0)),
                  mod_spec, full(ln_g), full(ln_b)],
        out_specs=pl.BlockSpec((tm, d), lambda i, f: (i, 0)),
        scratch_shapes=[pltpu.VMEM((tm, d), F32)],
        compiler_params=_cparams(("arbitrary", "arbitrary")),
        name="dense_ffn",
    )(u2, w_ff1, w_ff1, w_ff2, x1, mod, ln_g, ln_b)


def _moe_rows_kernel(u_ref, comb_ref, wg_ref, wu_ref, w2_ref, x1_ref, mod_ref, lg_ref, lb_ref, o_ref, acc,
                     *, alpha):
    e = pl.program_id(0)
    f = pl.program_id(1)

    @pl.when((e == 0) & (f == 0))
    def _():
        acc[...] = jnp.zeros_like(acc)

    u = u_ref[...]
    g = jnp.dot(u, wg_ref[0].astype(BF16), preferred_element_type=F32)
    up = jnp.dot(u, wu_ref[0].astype(BF16), preferred_element_type=F32)
    h = (_silu(g) * up).astype(BF16)
    y = jnp.dot(h, w2_ref[0].astype(BF16), preferred_element_type=F32)
    comb = comb_ref[...]
    lane = lax.broadcasted_iota(jnp.int32, comb.shape, 1)
    gate = jnp.sum(jnp.where(lane == e, comb, 0.0), axis=1, keepdims=True)
    acc[...] += gate * y

    @pl.when((e == pl.num_programs(0) - 1) & (f == pl.num_programs(1) - 1))
    def _():
        o_ref[...] = _post_norm2(x1_ref[...], acc[...], mod_ref, lg_ref, lb_ref, True, alpha)


def _moe_rows(u2, comb, w_e1, w_e2, layer, x1, mod, ln_g, ln_b, *, fc, alpha):
    n, d = x1.shape
    f_hidden = w_e2.shape[1]
    nf = f_hidden // fc
    ne = N_EXPERTS
    full = lambda a: pl.BlockSpec(a.shape, lambda e, f: (0,) * a.ndim)
    return pl.pallas_call(
        functools.partial(_moe_rows_kernel, alpha=alpha),
        out_shape=jax.ShapeDtypeStruct((n, d), F32),
        grid=(ne, nf),
        in_specs=[full(u2), full(comb),
                  pl.BlockSpec((1, d, fc), lambda e, f: (layer * ne + e, 0, f)),
                  pl.BlockSpec((1, d, fc), lambda e, f: (layer * ne + e, 0, nf + f)),
                  pl.BlockSpec((1, fc, d), lambda e, f: (layer * ne + e, f, 0)),
                  full(x1), full(mod), full(ln_g), full(ln_b)],
        out_specs=full(x1),
        scratch_shapes=[pltpu.VMEM((n, d), F32)],
        compiler_params=_cparams(("arbitrary", "arbitrary")),
        name="moe_rows",
    )(u2, comb, w_e1, w_e1, w_e2, x1, mod, ln_g, ln_b)


def _slot_copy(stage, xs_ref, sem, e, start):
    rows = pl.ds(pl.multiple_of(start, SUBLANES), MOE_BLOCK)
    return pltpu.make_async_copy(stage.at[e], xs_ref.at[rows, :], sem.at[e])


def _dispatch_kernel(start_ref, u_ref, li_ref, xs_in_ref, xs_ref, stage, sem):
    del xs_in_ref
    b = pl.program_id(0)
    u = u_ref[...]
    tb = u.shape[0]
    slot = lax.broadcasted_iota(jnp.int32, (tb, tb), 0).astype(F32)
    for e in range(N_EXPERTS):
        sel = jnp.where(slot == li_ref[e:e + 1, :], 1.0, 0.0).astype(BF16)
        rows = jnp.dot(sel, u, preferred_element_type=F32)

        @pl.when(b > 0)
        def _():
            _slot_copy(stage, xs_ref, sem, e, 0).wait()

        stage[e] = rows
        _slot_copy(stage, xs_ref, sem, e, start_ref[b * N_EXPERTS + e]).start()

    @pl.when(b == pl.num_programs(0) - 1)
    def _():
        for e in range(N_EXPERTS):
            _slot_copy(stage, xs_ref, sem, e, 0).wait()


def _dispatch(start, u2, li_t, n_slots):
    n, d = u2.shape
    er = li_t.shape[0]
    tb = MOE_BLOCK
    zeros = jnp.zeros((n_slots, d), F32)
    return pl.pallas_call(
        _dispatch_kernel,
        out_shape=jax.ShapeDtypeStruct((n_slots, d), F32),
        grid_spec=pltpu.PrefetchScalarGridSpec(
            num_scalar_prefetch=1,
            grid=(n // tb,),
            in_specs=[pl.BlockSpec((tb, d), lambda b, s: (b, 0)),
                      pl.BlockSpec((er, tb), lambda b, s: (0, b)),
                      pl.BlockSpec(memory_space=pl.ANY)],
            out_specs=pl.BlockSpec(memory_space=pl.ANY),
            scratch_shapes=[pltpu.VMEM((N_EXPERTS, tb, d), F32), pltpu.SemaphoreType.DMA((N_EXPERTS,))]),
        input_output_aliases={3: 0},
        compiler_params=_cparams(("arbitrary",)),
        name="moe_dispatch",
    )(start, u2, li_t, zeros)


def _expert_kernel(te_ref, tn_ref, x_ref, wg_ref, wu_ref, w2_ref, o_ref, xb, wgb, wub, w2b):
    j = pl.program_id(0)
    f = pl.program_id(1)
    n_sub = tn_ref[j]

    @pl.when(f == 0)
    def _():
        xb[...] = x_ref[...].astype(BF16)
        o_ref[...] = jnp.zeros_like(o_ref)

    @pl.when(n_sub > 0)
    def _():
        wgb[...] = wg_ref[0].astype(BF16)
        wub[...] = wu_ref[0].astype(BF16)
        w2b[...] = w2_ref[0].astype(BF16)

        def sub(i, carry):
            rows = pl.ds(pl.multiple_of(i * MOE_BLOCK, MOE_BLOCK), MOE_BLOCK)
            x = xb[rows, :]
            g = jnp.dot(x, wgb[...], preferred_element_type=F32)
            up = jnp.dot(x, wub[...], preferred_element_type=F32)
            h = (_silu(g) * up).astype(BF16)
            o_ref[rows, :] += jnp.dot(h, w2b[...], preferred_element_type=F32)
            return carry

        lax.fori_loop(0, n_sub, sub, 0)


def _experts(tile_e, tile_n, xs, w_e1, w_e2, layer, *, fc):
    n_slots, d = xs.shape
    f_hidden = w_e2.shape[1]
    nf = f_hidden // fc
    ne = N_EXPERTS
    tg = MOE_TILE

    def fsel(j, f, tn):
        return jnp.where(tn[j] > 0, f, nf - 1)

    return pl.pallas_call(
        _expert_kernel,
        out_shape=jax.ShapeDtypeStruct((n_slots, d), F32),
        grid_spec=pltpu.PrefetchScalarGridSpec(
            num_scalar_prefetch=2,
            grid=(n_slots // tg, nf),
            in_specs=[pl.BlockSpec((tg, d), lambda j, f, te, tn: (j, 0)),
                      pl.BlockSpec((1, d, fc), lambda j, f, te, tn: (layer * ne + te[j], 0, fsel(j, f, tn))),
                      pl.BlockSpec((1, d, fc), lambda j, f, te, tn: (layer * ne + te[j], 0, nf + fsel(j, f, tn))),
                      pl.BlockSpec((1, fc, d), lambda j, f, te, tn: (layer * ne + te[j], fsel(j, f, tn), 0))],
            out_specs=pl.BlockSpec((tg, d), lambda j, f, te, tn: (j, 0)),
            scratch_shapes=[pltpu.VMEM((tg, d), BF16), pltpu.VMEM((d, fc), BF16),
                            pltpu.VMEM((d, fc), BF16), pltpu.VMEM((fc, d), BF16)]),
        compiler_params=_cparams(("arbitrary", "arbitrary")),
        name="moe_experts",
    )(tile_e, tile_n, xs, w_e1, w_e1, w_e2)


def _fetch_copy(ys_ref, buf, sem, slot, e, start):
    rows = pl.ds(pl.multiple_of(start, SUBLANES), MOE_BLOCK)
    return pltpu.make_async_copy(ys_ref.at[rows, :], buf.at[slot, e], sem.at[slot, e])


def _combine_kernel(start_ref, li_ref, comb_ref, ys_ref, x1_ref, mod_ref, lg_ref, lb_ref, o_ref, buf, sem,
                    *, alpha):
    b = pl.program_id(0)
    nb = pl.num_programs(0)

    def fetch(blk, slot):
        for e in range(N_EXPERTS):
            _fetch_copy(ys_ref, buf, sem, slot, e, start_ref[blk * N_EXPERTS + e]).start()

    @pl.when(b == 0)
    def _():
        fetch(0, 0)

    @pl.when(b + 1 < nb)
    def _():
        fetch(b + 1, (b + 1) % 2)

    cur = b % 2
    tb = x1_ref.shape[0]
    slot = lax.broadcasted_iota(jnp.int32, (tb, tb), 0).astype(F32)
    acc = jnp.zeros(x1_ref.shape, F32)
    for e in range(N_EXPERTS):
        _fetch_copy(ys_ref, buf, sem, cur, e, 0).wait()
        gate_t = jnp.where(slot == li_ref[e:e + 1, :], comb_ref[e:e + 1, :], 0.0).astype(BF16)
        y = buf[cur, e].astype(BF16)
        acc = acc + lax.dot_general(gate_t, y, (((0,), (0,)), ((), ())), preferred_element_type=F32)
    o_ref[...] = _post_norm2(x1_ref[...], acc, mod_ref, lg_ref, lb_ref, False, alpha)


def _combine(start, li_t, comb_t, ys, x1, mod, ln_g, ln_b, *, tiles_per_seq, alpha):
    n, d = x1.shape
    er = li_t.shape[0]
    tb = MOE_BLOCK
    full = lambda a: pl.BlockSpec(a.shape, lambda b, s: (0,) * a.ndim)
    return pl.pallas_call(
        functools.partial(_combine_kernel, alpha=alpha),
        out_shape=jax.ShapeDtypeStruct((n, d), F32),
        grid_spec=pltpu.PrefetchScalarGridSpec(
            num_scalar_prefetch=1,
            grid=(n // tb,),
            in_specs=[pl.BlockSpec((er, tb), lambda b, s: (0, b)),
                      pl.BlockSpec((er, tb), lambda b, s: (0, b)),
                      pl.BlockSpec(memory_space=pl.ANY),
                      pl.BlockSpec((tb, d), lambda b, s: (b, 0)),
                      pl.BlockSpec((1, 6, d), lambda b, s: (b // tiles_per_seq, 0, 0)),
                      full(ln_g), full(ln_b)],
            out_specs=pl.BlockSpec((tb, d), lambda b, s: (b, 0)),
            scratch_shapes=[pltpu.VMEM((2, N_EXPERTS, tb, d), F32), pltpu.SemaphoreType.DMA((2, N_EXPERTS))]),
        compiler_params=_cparams(("arbitrary",)),
        name="moe_combine",
    )(start, li_t, comb_t, ys, x1, mod, ln_g, ln_b)


def _routing_tables(cum_t, n_tokens, n_slots):
    ne, tb, tg = N_EXPERTS, MOE_BLOCK, MOE_TILE
    nb = n_tokens // tb
    cum = cum_t.reshape(cum_t.shape[0], nb, LANES)[:ne, :, 0].T.astype(jnp.int32)
    total = cum[-1]
    base = jnp.concatenate([jnp.zeros((1, ne), jnp.int32), cum[:-1]], axis=0)
    region = (total + tb + tg - 1) // tg * tg
    end = jnp.cumsum(region)
    off = end - region
    start = (off[None, :] + base).reshape(-1)
    tile_row = jnp.arange(n_slots // tg, dtype=jnp.int32) * tg
    tile_e = jnp.minimum(jnp.sum(tile_row[:, None] >= end[None, :], axis=1), ne - 1).astype(jnp.int32)
    left = total[tile_e] - (tile_row - off[tile_e])
    tile_n = jnp.clip((left + tb - 1) // tb, 0, tg // tb).astype(jnp.int32)
    return start, tile_e, tile_n


def kernel(x_prompt, x_sample, cache_k, cache_v, state_conv, c_prompt, c_sample, w_ada, b_ada, w_in,
           conv_w, g_conv_out, g_att_out, w_out, ln1_g, ln1_b, ln2_g, ln2_b, w_ff1, w_ff2, w_router,
           w_e1, w_e2):
    bsz, s_len, d = x_prompt.shape
    nb_s, t_new, _ = x_sample.shape
    depth = w_in.shape[0]
    dc = conv_w.shape[2]
    da = w_in.shape[2] // 3 - dc
    assert t_new == 1 and da == N_HEADS * HEAD_DIM and da // LANES * LANES == da
    assert s_len % (Q_BLOCK * max(dl for _, dl in DILATED_PATTERNS)) == 0
    assert cache_k.shape[2] == max(w for w, _ in DILATED_PATTERNS)
    alpha = (2 * depth) ** 0.25
    n_p = bsz * s_len
    keep = min(cache_k.shape[2], s_len)
    n_exp, f_exp = w_e2.shape[1], w_e2.shape[2]
    assert n_exp == N_EXPERTS
    n_slots = n_p * TOP_K + n_exp * ((n_p // MOE_BLOCK) * (SUBLANES - 1) + MOE_BLOCK + MOE_TILE - 1)
    n_slots = (n_slots + MOE_TILE - 1) // MOE_TILE * MOE_TILE

    slopes = jnp.exp2(-8.0 * jnp.arange(1, N_HEADS + 1, dtype=F32) / N_HEADS)
    slopes_pairs = jnp.broadcast_to(slopes.reshape(N_HEADS // 2, 2, 1), (N_HEADS // 2, 2, 2 * Q_BLOCK))
    slopes_col = slopes.reshape(N_HEADS, 1, 1)
    w_buf = cache_k.shape[2]
    cache_kt = cache_k.transpose(0, 1, 3, 4, 2).reshape(depth * nb_s, N_HEADS, HEAD_DIM, w_buf)
    cache_vt = cache_v.transpose(0, 1, 3, 4, 2).reshape(depth * nb_s, N_HEADS, HEAD_DIM, w_buf)

    rows_c = (bsz + nb_s + 7) // 8 * 8
    c_all = jnp.zeros((rows_c, d), F32).at[:bsz].set(c_prompt).at[bsz:bsz + nb_s].set(c_sample)
    ada = _adaln_all(c_all, w_ada, b_ada)

    w_e1f = w_e1.reshape((-1,) + w_e1.shape[2:])
    w_e2f = w_e2.reshape((-1,) + w_e2.shape[2:])

    xp = x_prompt.reshape(n_p, d)
    xs = x_sample.reshape(nb_s, d)
    tm_p = 512
    outs = {k: [] for k in ("kp", "vp", "cp", "ks", "vs", "cs")}
    row2 = lambda a: a.reshape(1, -1)
    for l in range(depth):
        moe = l % 2 == 1
        li = l // 2
        mod_p = ada[l, :bsz].reshape(bsz, 6, d)
        mod_s = ada[l, bsz:bsz + nb_s].reshape(nb_s, 6, d).transpose(1, 0, 2)
        w_in_b = w_in[l].astype(BF16)
        w_out_b = w_out[l].astype(BF16)
        lnp = (row2(ln1_g[l]), row2(ln1_b[l]))
        ln2 = (row2(ln2_g[l]), row2(ln2_b[l]))
        norm_w = (conv_w[l], row2(g_conv_out[l]), row2(g_att_out[l]))

        bg, z, q, k, v = _in_proj(xp, mod_p, w_in_b, per_row=False, tm=tm_p, tiles_per_seq=s_len // tm_p)
        att = _prompt_attention(q, k, v, slopes_pairs, bsz, s_len)
        tm3 = MOE_BLOCK if moe else tm_p
        if moe:
            wr_t = jnp.zeros((16, d), F32).at[:n_exp].set(w_router[li].T).astype(BF16)
            x1, u2, comb_t, li_t, cum_t = _out_proj(
                bg, z, z, z, att, xp, mod_p, *norm_w, w_out_b, *lnp, wr_t,
                per_row=False, tm=tm3, tiles_per_seq=s_len // tm3, alpha=alpha)
            start, tile_e, tile_n = _routing_tables(cum_t, n_p, n_slots)
            xsorted = _dispatch(start, u2, li_t, n_slots)
            ysorted = _experts(tile_e, tile_n, xsorted, w_e1f, w_e2f, li, fc=512)
            xp = _combine(start, li_t, comb_t, ysorted, x1, mod_p, *ln2,
                          tiles_per_seq=s_len // MOE_BLOCK, alpha=alpha)
        else:
            x1, u2 = _out_proj(bg, z, z, z, att, xp, mod_p, *norm_w, w_out_b, *lnp, None,
                               per_row=False, tm=tm3, tiles_per_seq=s_len // tm3, alpha=alpha)
            xp = _dense_ffn(u2, w_ff1, w_ff2, li, x1, mod_p, *ln2, per_row=False, tm=1024,
                            tiles_per_seq=s_len // 1024, fc=256, alpha=alpha)
        k3 = k.reshape(bsz, s_len, N_HEADS, HEAD_DIM)
        v3 = v.reshape(bsz, s_len, N_HEADS, HEAD_DIM)
        outs["kp"].append(k3[:, s_len - keep:])
        outs["vp"].append(v3[:, s_len - keep:])
        outs["cp"].append(z.reshape(bsz, s_len, dc)[:, s_len - (CONV_WIDTH - 1):])

        bg, z, q, k, v = _in_proj(xs, mod_s, w_in_b, per_row=True, tm=nb_s, tiles_per_seq=1)
        att = _sample_attention(q, k, v, cache_kt, cache_vt, l, slopes_col)
        st = state_conv[l]
        if moe:
            wr = jnp.zeros((d, LANES), F32).at[:, :n_exp].set(w_router[li]).astype(BF16)
            x1, u2, comb = _out_proj(bg, z, st[:, 1], st[:, 0], att, xs, mod_s, *norm_w, w_out_b, *lnp, wr,
                                     per_row=True, tm=nb_s, tiles_per_seq=1, alpha=alpha)
            xs = _moe_rows(u2, comb, w_e1f, w_e2f, li, x1, mod_s, *ln2, fc=512, alpha=alpha)
        else:
            x1, u2 = _out_proj(bg, z, st[:, 1], st[:, 0], att, xs, mod_s, *norm_w, w_out_b, *lnp, None,
                               per_row=True, tm=nb_s, tiles_per_seq=1, alpha=alpha)
            xs = _dense_ffn(u2, w_ff1, w_ff2, li, x1, mod_s, *ln2, per_row=True, tm=nb_s,
                            tiles_per_seq=1, fc=256, alpha=alpha)
        outs["ks"].append(k.reshape(nb_s, 1, N_HEADS, HEAD_DIM))
        outs["vs"].append(v.reshape(nb_s, 1, N_HEADS, HEAD_DIM))
        outs["cs"].append(jnp.stack([st[:, 1], z], axis=1))

    return (xp.reshape(bsz, s_len, d), xs.reshape(nb_s, 1, d),
            jnp.stack(outs["kp"]), jnp.stack(outs["vp"]), jnp.stack(outs["cp"]),
            jnp.stack(outs["ks"]), jnp.stack(outs["vs"]), jnp.stack(outs["cs"]))
```

```python
import functools

import jax
import jax.numpy as jnp
from jax import lax
from jax.experimental import pallas as pl
from jax.experimental.pallas import tpu as pltpu

F32 = jnp.float32
BF16 = jnp.bfloat16

N_HEADS = 8
HEAD_DIM = 64
CONV_WIDTH = 3
DILATED_PATTERNS = ((128, 1), (512, 4), (2048, 16))
Q_BLOCK = 128
N_EXPERTS = 8
TOP_K = 2
LN_EPS = 1e-5
LOG2E = 1.4426950408889634

LANES = 128
SUBLANES = 8
ATTN_GROUP = 4
MOE_BLOCK = 256
MOE_TILE = 1024
VMEM_LIMIT = 56 * 1024 * 1024


def _cparams(sem, vmem=VMEM_LIMIT):
    return pltpu.CompilerParams(dimension_semantics=sem, vmem_limit_bytes=vmem)


def _ln(x):
    mu = jnp.mean(x, axis=-1, keepdims=True)
    xc = x - mu
    var = jnp.mean(xc * xc, axis=-1, keepdims=True)
    return xc * lax.rsqrt(var + LN_EPS)


def _silu(x):
    return x * jax.nn.sigmoid(x)


def _mod_rows(mod_ref, k, per_row):
    return mod_ref[k] if per_row else mod_ref[0, k:k + 1, :]


def _mod_spec(per_row, rows, d, tiles_per_seq):
    if per_row:
        return pl.BlockSpec((6, rows, d), lambda i, *_: (0, 0, 0))
    return pl.BlockSpec((1, 6, d), lambda i, *_: (i // tiles_per_seq, 0, 0))


def _ada_kernel(c_ref, w_ref, b_ref, o_ref):
    c = c_ref[...]
    s = _silu(c).astype(BF16)
    o_ref[0] = jnp.dot(s, w_ref[0].astype(BF16), preferred_element_type=F32) + b_ref[0]


def _adaln_all(c_all, w_ada, b_ada):
    depth, d, e6 = w_ada.shape
    rows = c_all.shape[0]
    tn = e6 // 4
    return pl.pallas_call(
        _ada_kernel,
        out_shape=jax.ShapeDtypeStruct((depth, rows, e6), F32),
        grid=(depth, e6 // tn),
        in_specs=[pl.BlockSpec((rows, d), lambda l, j: (0, 0)),
                  pl.BlockSpec((1, d, tn), lambda l, j: (l, 0, j)),
                  pl.BlockSpec((1, 1, tn), lambda l, j: (l, 0, j))],
        out_specs=pl.BlockSpec((1, rows, tn), lambda l, j: (l, 0, j)),
        compiler_params=_cparams(("arbitrary", "arbitrary")),
        name="adaln",
    )(c_all, w_ada, b_ada.reshape(depth, 1, e6))


def _inproj_kernel(*refs, prompt, dc, tiles_per_seq, first_kept):
    if prompt:
        x_ref, mod_ref, w_ref, _, _, bg_ref, z_ref, qe_ref, qo_ref, k_ref, v_ref, kt_ref, vt_ref = refs
    else:
        x_ref, mod_ref, w_ref, bg_ref, z_ref, q_ref, k_ref, v_ref = refs
    shift = _mod_rows(mod_ref, 0, not prompt)
    scale = _mod_rows(mod_ref, 1, not prompt)
    u = (_ln(x_ref[...]) * (1.0 + scale) + shift).astype(BF16)

    def proj(j):
        return jnp.dot(u, w_ref[:, j * dc:(j + 1) * dc], preferred_element_type=F32)

    bg_ref[...] = proj(0)
    z_ref[...] = proj(1) * proj(2)
    q = proj(3)
    k = proj(4)
    v = proj(5)
    k_ref[...] = k
    v_ref[...] = v
    if not prompt:
        q_ref[...] = q
        return
    q = q * (HEAD_DIM ** -0.5 * LOG2E)
    odd = (lax.broadcasted_iota(jnp.int32, q.shape, 1) // HEAD_DIM) % 2 == 1
    qe_ref[...] = jnp.where(odd, 0.0, q)
    qo_ref[...] = jnp.where(odd, q, 0.0)

    @pl.when(pl.program_id(0) % tiles_per_seq >= first_kept)
    def _():
        kt_ref[...] = k.T
        vt_ref[...] = v.T


def _in_proj_prompt(x, mod, w_in_b, kt_buf, vt_buf, layer, *, tm, s_len):
    n, d = x.shape
    dc = w_in_b.shape[1] // 6
    keep = kt_buf.shape[3]
    tps = s_len // tm
    first_kept = (s_len - keep) // tm
    row = lambda: pl.BlockSpec((tm, dc), lambda i: (i, 0))
    kept = lambda: pl.BlockSpec((None, None, dc, tm),
                                lambda i: (layer, i // tps, 0, jnp.maximum(i % tps - first_kept, 0)))
    rows = jax.ShapeDtypeStruct((n, dc), F32)
    outs = pl.pallas_call(
        functools.partial(_inproj_kernel, prompt=True, dc=dc, tiles_per_seq=tps, first_kept=first_kept),
        out_shape=[rows] * 6 + [jax.ShapeDtypeStruct(kt_buf.shape, F32)] * 2,
        grid=(n // tm,),
        in_specs=[pl.BlockSpec((tm, d), lambda i: (i, 0)),
                  _mod_spec(False, tm, d, tps),
                  pl.BlockSpec(w_in_b.shape, lambda i: (0, 0), pipeline_mode=pl.Buffered(1)),
                  pl.BlockSpec(memory_space=pl.ANY), pl.BlockSpec(memory_space=pl.ANY)],
        out_specs=[row() for _ in range(6)] + [kept(), kept()],
        input_output_aliases={3: 6, 4: 7},
        compiler_params=_cparams(("arbitrary",)),
        name="in_proj",
    )(x, mod, w_in_b, kt_buf, vt_buf)
    return outs


def _in_proj_sample(x, mod, w_in_b):
    n, d = x.shape
    dc = w_in_b.shape[1] // 6
    row = lambda: pl.BlockSpec((n, dc), lambda i: (0, 0))
    return pl.pallas_call(
        functools.partial(_inproj_kernel, prompt=False, dc=dc, tiles_per_seq=1, first_kept=0),
        out_shape=[jax.ShapeDtypeStruct((n, dc), F32)] * 5,
        grid=(1,),
        in_specs=[pl.BlockSpec((n, d), lambda i: (0, 0)),
                  _mod_spec(True, n, d, 1),
                  pl.BlockSpec(w_in_b.shape, lambda i: (0, 0), pipeline_mode=pl.Buffered(1))],
        out_specs=[row() for _ in range(5)],
        compiler_params=_cparams(("arbitrary",)),
        name="in_proj_rows",
    )(x, mod, w_in_b)


def _attn_kernel(qe_ref, qo_ref, k_ref, v_ref, sl_ref, o_ref, m_scr, l_scr, bias_scr):
    s_len = k_ref.shape[0]
    qb = Q_BLOCK
    half = lax.broadcasted_iota(jnp.int32, (qb, LANES), 1) >= HEAD_DIM
    ri = lax.broadcasted_iota(jnp.int32, (2 * qb, 2 * qb), 0)
    ji = lax.broadcasted_iota(jnp.int32, (2 * qb, 2 * qb), 1)
    step = qb + (ri % qb) - ji
    band = (step >= 0) & (step <= qb)
    slope = jnp.where(ri < qb, sl_ref[0, 0:1, :], sl_ref[0, 1:2, :]) * LOG2E
    for pi, (_, d) in enumerate(DILATED_PATTERNS):
        bias = jnp.where(band, -(slope * (d * step).astype(F32)), -jnp.inf)
        bias_scr[2 * pi] = bias
        bias_scr[2 * pi + 1] = jnp.where(ji < qb, -jnp.inf, bias)
    ones = jnp.ones((2 * qb, LANES), BF16)

    def block(idx, pi, d, first, last):
        r = idx % d
        n = idx // d
        base = n * (qb * d) + r
        prev = jnp.maximum(n - 1, 0) * (qb * d) + r
        rows = pl.ds(base, qb, stride=d)
        prow = pl.ds(prev, qb, stride=d)
        q2 = jnp.concatenate([qe_ref[rows, :], qo_ref[rows, :]], axis=0).astype(BF16)
        k2 = jnp.concatenate([k_ref[prow, :], k_ref[rows, :]], axis=0).astype(BF16)
        v2 = jnp.concatenate([jnp.concatenate([v_ref[prow, :], v_ref[rows, :]], axis=0).astype(BF16), ones],
                             axis=1)
        s = lax.dot_general(q2, k2, (((1,), (1,)), ((), ())), preferred_element_type=F32)
        s = s + bias_scr[2 * pi + jnp.where(n == 0, 1, 0)]
        s0, s1 = s[:, :qb], s[:, qb:]
        mb = jnp.max(jnp.maximum(s0, s1), axis=1, keepdims=True)
        if first:
            m_new = jnp.broadcast_to(mb, (2 * qb, LANES))
        else:
            m_old = jnp.concatenate([m_scr[0, rows, :], m_scr[1, rows, :]], axis=0)
            m_new = jnp.maximum(m_old, mb)
        p = jnp.concatenate([jnp.exp2(s0 - m_new), jnp.exp2(s1 - m_new)], axis=1).astype(BF16)
        pvs = jnp.dot(p, v2, preferred_element_type=F32)
        pv, rs = pvs[:, :LANES], pvs[:, LANES:]
        if first:
            l_new = rs
            acc = jnp.where(half, pv[qb:], pv[:qb])
        else:
            alpha = jnp.exp2(m_old - m_new)
            l_old = jnp.concatenate([l_scr[0, rows, :], l_scr[1, rows, :]], axis=0)
            l_new = alpha * l_old + rs
            acc_old = o_ref[rows, :]
            acc = jnp.where(half, alpha[qb:] * acc_old + pv[qb:], alpha[:qb] * acc_old + pv[:qb])
        if last:
            acc = acc / jnp.where(half, l_new[qb:], l_new[:qb])
        return rows, m_new, l_new, acc

    order = sorted(range(len(DILATED_PATTERNS)), key=lambda pi: -DILATED_PATTERNS[pi][1])
    for pos, pi in enumerate(order):
        d = DILATED_PATTERNS[pi][1]
        first, last = pos == 0, pos == len(order) - 1

        def body(it, carry, pi=pi, d=d, first=first, last=last):
            done = [block(it * ATTN_GROUP + g, pi, d, first, last) for g in range(ATTN_GROUP)]
            for rows, m_new, l_new, acc in done:
                if not last:
                    m_scr[0, rows, :] = m_new[:qb]
                    m_scr[1, rows, :] = m_new[qb:]
                    l_scr[0, rows, :] = l_new[:qb]
                    l_scr[1, rows, :] = l_new[qb:]
                o_ref[rows, :] = acc
            return carry

        lax.fori_loop(0, s_len // (qb * ATTN_GROUP), body, 0)


def _prompt_attention(qe, qo, k, v, slopes_pairs, bsz, s_len):
    da = k.shape[1]
    pairs = da // LANES
    spec = lambda: pl.BlockSpec((None, s_len, LANES), lambda b, h: (b, 0, h))
    rs = lambda a: a.reshape(bsz, s_len, da)
    out = pl.pallas_call(
        _attn_kernel,
        out_shape=jax.ShapeDtypeStruct((bsz, s_len, da), F32),
        grid=(bsz, pairs),
        in_specs=[spec(), spec(), spec(), spec(),
                  pl.BlockSpec((1, 2, 2 * Q_BLOCK), lambda b, h: (h, 0, 0))],
        out_specs=spec(),
        scratch_shapes=[pltpu.VMEM((2, s_len, LANES), F32), pltpu.VMEM((2, s_len, LANES), F32),
                        pltpu.VMEM((2 * len(DILATED_PATTERNS), 2 * Q_BLOCK, 2 * Q_BLOCK), F32)],
        compiler_params=_cparams(("arbitrary", "arbitrary")),
        name="prompt_attention",
    )(rs(qe), rs(qo), rs(k), rs(v), slopes_pairs)
    return out.reshape(bsz * s_len, da)


def _sattn_kernel(q_ref, kn_ref, vn_ref, kt_ref, vt_ref, sl_ref, o_ref):
    bb, nh, _, w = kt_ref.shape
    dist = w - lax.broadcasted_iota(jnp.int32, (1, w), 1)
    count = jnp.zeros((1, w), F32)
    for window, d in DILATED_PATTERNS:
        count = count + jnp.where((dist % d == 0) & (dist <= window), 1.0, 0.0)
    distf = dist.astype(F32)
    n_pat = float(len(DILATED_PATTERNS))
    scale = HEAD_DIM ** -0.5

    def body(i, carry):
        b = i // nh
        h = i % nh
        q = q_ref[b, h] * scale
        s = jnp.sum(kt_ref[b, h] * q, axis=0, keepdims=True)
        s = jnp.where(count > 0.0, s - sl_ref[h] * distf, -jnp.inf)
        s_self = jnp.sum(q * kn_ref[b, h], axis=0, keepdims=True)
        m = jnp.maximum(jnp.max(s, axis=1, keepdims=True), s_self)
        p = jnp.exp(s - m) * count
        p_self = n_pat * jnp.exp(s_self - m)
        den = jnp.sum(p, axis=1, keepdims=True) + p_self
        num = jnp.sum(vt_ref[b, h] * p, axis=1, keepdims=True) + p_self * vn_ref[b, h]
        o_ref[b, h] = num / den
        return carry

    lax.fori_loop(0, bb * nh, body, 0)


def _sample_attention(q, k, v, cache_kt, cache_vt, layer, slopes_col, *, bb=2):
    nb, da = q.shape
    _, nh, hd, w_buf = cache_kt.shape
    cols = lambda a: a.reshape(nb, nh, hd, 1)
    col = lambda: pl.BlockSpec((bb, nh, hd, 1), lambda i: (i, 0, 0, 0))
    steps = nb // bb
    off = layer * steps
    win = lambda: pl.BlockSpec((bb, nh, hd, w_buf), lambda i: (off + i, 0, 0, 0))
    out = pl.pallas_call(
        _sattn_kernel,
        out_shape=jax.ShapeDtypeStruct((nb, nh, hd, 1), F32),
        grid=(steps,),
        in_specs=[col(), col(), col(), win(), win(), pl.BlockSpec(slopes_col.shape, lambda i: (0, 0, 0))],
        out_specs=col(),
        compiler_params=_cparams(("arbitrary",)),
        name="sample_attention",
    )(cols(q), cols(k), cols(v), cache_kt, cache_vt, slopes_col)
    return out.reshape(nb, da)


def _outproj_kernel(*refs, per_row, moe, router_rows, tiles_per_seq, alpha):
    it = iter(refs)
    bg_ref, z_ref, za_ref, zb_ref, att_ref, x_ref, mod_ref = (next(it) for _ in range(7))
    cw_ref, gc_ref, ga_ref, w_ref, lg_ref, lb_ref = (next(it) for _ in range(6))
    wr_ref = next(it) if moe else None
    x1_ref, u2_ref = next(it), next(it)
    if moe and router_rows:
        comb_ref = next(it)
    elif moe:
        comb_ref, li_ref, cum_ref, carry = (next(it) for _ in range(4))

    i = pl.program_id(0)
    z = z_ref[...]
    tm, dc = z.shape
    if per_row:
        z1, z2 = za_ref[...], zb_ref[...]
    else:
        keep = jnp.where(i % tiles_per_seq == 0, 0.0, 1.0)
        h1 = za_ref[7:8, :] * keep
        h2 = za_ref[6:7, :] * keep
        rid = lax.broadcasted_iota(jnp.int32, (tm, dc), 0)
        z1 = jnp.where(rid == 0, h1, pltpu.roll(z, 1, axis=0))
        z2 = jnp.where(rid == 0, h2, jnp.where(rid == 1, h1, pltpu.roll(z, 2, axis=0)))
    conv = cw_ref[0:1, :] * z2 + cw_ref[1:2, :] * z1 + cw_ref[2:3, :] * z
    yc = bg_ref[...] * conv
    yc = yc * lax.rsqrt(jnp.mean(yc * yc, axis=-1, keepdims=True) + LN_EPS) * gc_ref[...]
    ya = att_ref[...]
    ya = ya * lax.rsqrt(jnp.mean(ya * ya, axis=-1, keepdims=True) + LN_EPS) * ga_ref[...]
    h = (jnp.dot(yc.astype(BF16), w_ref[:dc, :], preferred_element_type=F32)
         + jnp.dot(ya.astype(BF16), w_ref[dc:, :], preferred_element_type=F32))
    gate1 = _mod_rows(mod_ref, 2, per_row)
    x1 = _ln(alpha * x_ref[...] + (1.0 + gate1) * h) * lg_ref[...] + lb_ref[...]
    x1_ref[...] = x1
    u2 = (_ln(x1) * (1.0 + _mod_rows(mod_ref, 4, per_row)) + _mod_rows(mod_ref, 3, per_row)).astype(BF16)
    u2_ref[...] = u2
    if not moe:
        return

    if router_rows:
        lg = jnp.dot(u2, wr_ref[...], preferred_element_type=F32)
        lane = lax.broadcasted_iota(jnp.int32, lg.shape, 1)
        lg = jnp.where(lane < N_EXPERTS, lg, -jnp.inf)
        m1 = jnp.max(lg, axis=1, keepdims=True)
        i1 = jnp.min(jnp.where(lg == m1, lane, LANES), axis=1, keepdims=True)
        lg2 = jnp.where(lane == i1, -jnp.inf, lg)
        m2 = jnp.max(lg2, axis=1, keepdims=True)
        i2 = jnp.min(jnp.where(lg2 == m2, lane, LANES), axis=1, keepdims=True)
        ex = jnp.exp(m2 - m1)
        g1 = 1.0 / (1.0 + ex)
        comb_ref[...] = jnp.where(lane == i1, g1, 0.0) + jnp.where(lane == i2, ex * g1, 0.0)
        return

    lg = lax.dot_general(wr_ref[...], u2, (((1,), (1,)), ((), ())), preferred_element_type=F32)
    row = lax.broadcasted_iota(jnp.int32, lg.shape, 0)
    lg = jnp.where(row < N_EXPERTS, lg, -jnp.inf)
    m1 = jnp.max(lg, axis=0, keepdims=True)
    i1 = jnp.min(jnp.where(lg == m1, row, lg.shape[0]), axis=0, keepdims=True)
    lg2 = jnp.where(row == i1, -jnp.inf, lg)
    m2 = jnp.max(lg2, axis=0, keepdims=True)
    i2 = jnp.min(jnp.where(lg2 == m2, row, lg.shape[0]), axis=0, keepdims=True)
    ex = jnp.exp(m2 - m1)
    g1 = 1.0 / (1.0 + ex)
    sel = (row == i1) | (row == i2)
    comb_ref[...] = jnp.where(row == i1, g1, 0.0) + jnp.where(row == i2, ex * g1, 0.0)
    onehot = jnp.where(sel, 1.0, 0.0)
    ta = lax.broadcasted_iota(jnp.int32, (tm, tm), 0)
    tb = lax.broadcasted_iota(jnp.int32, (tm, tm), 1)
    upper = jnp.where(ta < tb, 1.0, 0.0).astype(BF16)
    rank = jnp.dot(onehot.astype(BF16), upper, preferred_element_type=F32)
    li_ref[...] = jnp.where(sel, rank, -1.0)

    @pl.when(i == 0)
    def _():
        carry[...] = jnp.zeros_like(carry)

    count = jnp.sum(onehot, axis=1, keepdims=True)
    carry[...] = carry[...] + jnp.floor((count + (SUBLANES - 1.0)) * (1.0 / SUBLANES)) * SUBLANES
    cum_ref[...] = carry[...]


def _out_proj(bg, z, za, zb, att, x, mod, conv_w, g_conv, g_att, w_out_b, ln_g, ln_b, w_router_b,
              *, per_row, tm, tiles_per_seq, alpha):
    n, d = x.shape
    dc = z.shape[1]
    moe = w_router_b is not None
    router_rows = moe and per_row
    full = lambda a: pl.BlockSpec(a.shape, lambda i: (0,) * a.ndim)
    rowc = lambda: pl.BlockSpec((tm, dc), lambda i: (i, 0))
    rowd = lambda: pl.BlockSpec((tm, d), lambda i: (i, 0))
    if per_row:
        za_spec, zb_spec = rowc(), rowc()
    else:
        za_spec = pl.BlockSpec((8, dc), lambda i: (jnp.maximum(i * (tm // 8) - 1, 0), 0))
        zb_spec = pl.BlockSpec((8, dc), lambda i: (0, 0))
    args = [bg, z, za, zb, att, x, mod, conv_w, g_conv, g_att, w_out_b, ln_g, ln_b]
    in_specs = [rowc(), rowc(), za_spec, zb_spec, rowc(), rowd(), _mod_spec(per_row, tm, d, tiles_per_seq),
                full(conv_w), full(g_conv), full(g_att),
                pl.BlockSpec(w_out_b.shape, lambda i: (0, 0), pipeline_mode=pl.Buffered(1)),
                full(ln_g), full(ln_b)]
    out_shape = [jax.ShapeDtypeStruct((n, d), F32), jax.ShapeDtypeStruct((n, d), BF16)]
    out_specs = [rowd(), rowd()]
    scratch = []
    if moe:
        args.append(w_router_b)
        in_specs.append(full(w_router_b))
        if router_rows:
            out_shape.append(jax.ShapeDtypeStruct((n, LANES), F32))
            out_specs.append(pl.BlockSpec((tm, LANES), lambda i: (i, 0)))
        else:
            er = w_router_b.shape[0]
            out_shape += [jax.ShapeDtypeStruct((er, n), F32), jax.ShapeDtypeStruct((er, n), F32),
                          jax.ShapeDtypeStruct((er, (n // tm) * LANES), F32)]
            out_specs += [pl.BlockSpec((er, tm), lambda i: (0, i)), pl.BlockSpec((er, tm), lambda i: (0, i)),
                          pl.BlockSpec((er, LANES), lambda i: (0, i))]
            scratch.append(pltpu.VMEM((er, LANES), F32))
    return pl.pallas_call(
        functools.partial(_outproj_kernel, per_row=per_row, moe=moe, router_rows=router_rows,
                          tiles_per_seq=tiles_per_seq, alpha=alpha),
        out_shape=out_shape,
        grid=(n // tm,),
        in_specs=in_specs,
        out_specs=out_specs,
        scratch_shapes=scratch,
        compiler_params=_cparams(("arbitrary",)),
        name="out_proj",
    )(*args)


def _post_norm2(x1, h, mod_ref, lg_ref, lb_ref, per_row, alpha):
    gate2 = _mod_rows(mod_ref, 5, per_row)
    return _ln(alpha * x1 + (1.0 + gate2) * h) * lg_ref[...] + lb_ref[...]


def _ffn_kernel(u_ref, wg_ref, wu_ref, w2_ref, x1_ref, mod_ref, lg_ref, lb_ref, o_ref, acc,
                *, per_row, alpha):
    f = pl.program_id(1)

    @pl.when(f == 0)
    def _():
        acc[...] = jnp.zeros_like(acc)

    u = u_ref[...]
    g = jnp.dot(u, wg_ref[0].astype(BF16), preferred_element_type=F32)
    up = jnp.dot(u, wu_ref[0].astype(BF16), preferred_element_type=F32)
    h = (_silu(g) * up).astype(BF16)
    acc[...] += jnp.dot(h, w2_ref[0].astype(BF16), preferred_element_type=F32)

    @pl.when(f == pl.num_programs(1) - 1)
    def _():
        o_ref[...] = _post_norm2(x1_ref[...], acc[...], mod_ref, lg_ref, lb_ref, per_row, alpha)


def _dense_ffn(u2, w_ff1, w_ff2, layer, x1, mod, ln_g, ln_b, *, per_row, tm, tiles_per_seq, fc, alpha):
    n, d = x1.shape
    f_hidden = w_ff2.shape[1]
    nf = f_hidden // fc
    full = lambda a: pl.BlockSpec(a.shape, lambda i, f: (0,) * a.ndim)
    mod_spec = (pl.BlockSpec((6, tm, d), lambda i, f: (0, 0, 0)) if per_row
                else pl.BlockSpec((1, 6, d), lambda i, f: (i // tiles_per_seq, 0, 0)))
    return pl.pallas_call(
        functools.partial(_ffn_kernel, per_row=per_row, alpha=alpha),
        out_shape=jax.ShapeDtypeStruct((n, d), F32),
        grid=(n // tm, nf),
        in_specs=[pl.BlockSpec((tm, d), lambda i, f: (i, 0)),
                  pl.BlockSpec((1, d, fc), lambda i, f: (layer, 0, f)),
                  pl.BlockSpec((1, d, fc), lambda i, f: (layer, 0, nf + f)),
                  pl.BlockSpec((1, fc, d), lambda i, f: (layer, f, 0)),
                  pl.BlockSpec((tm, d), lambda i, f: (i, 0)),
                  mod_spec, full(ln_g), full(ln_b)],
        out_specs=pl.BlockSpec((tm, d), lambda i, f: (i, 0)),
        scratch_shapes=[pltpu.VMEM((tm, d), F32)],
        compiler_params=_cparams(("arbitrary", "arbitrary")),
        name="dense_ffn",
    )(u2, w_ff1, w_ff1, w_ff2, x1, mod, ln_g, ln_b)


def _moe_rows_kernel(u_ref, comb_ref, wg_ref, wu_ref, w2_ref, x1_ref, mod_ref, lg_ref, lb_ref, o_ref, acc,
                     *, alpha):
    e = pl.program_id(0)
    f = pl.program_id(1)

    @pl.when((e == 0) & (f == 0))
    def _():
        acc[...] = jnp.zeros_like(acc)

    u = u_ref[...]
    g = jnp.dot(u, wg_ref[0].astype(BF16), preferred_element_type=F32)
    up = jnp.dot(u, wu_ref[0].astype(BF16), preferred_element_type=F32)
    h = (_silu(g) * up).astype(BF16)
    y = jnp.dot(h, w2_ref[0].astype(BF16), preferred_element_type=F32)
    comb = comb_ref[...]
    lane = lax.broadcasted_iota(jnp.int32, comb.shape, 1)
    gate = jnp.sum(jnp.where(lane == e, comb, 0.0), axis=1, keepdims=True)
    acc[...] += gate * y

    @pl.when((e == pl.num_programs(0) - 1) & (f == pl.num_programs(1) - 1))
    def _():
        o_ref[...] = _post_norm2(x1_ref[...], acc[...], mod_ref, lg_ref, lb_ref, True, alpha)


def _moe_rows(u2, comb, w_e1, w_e2, layer, x1, mod, ln_g, ln_b, *, fc, alpha):
    n, d = x1.shape
    f_hidden = w_e2.shape[1]
    nf = f_hidden // fc
    ne = N_EXPERTS
    full = lambda a: pl.BlockSpec(a.shape, lambda e, f: (0,) * a.ndim)
    return pl.pallas_call(
        functools.partial(_moe_rows_kernel, alpha=alpha),
        out_shape=jax.ShapeDtypeStruct((n, d), F32),
        grid=(ne, nf),
        in_specs=[full(u2), full(comb),
                  pl.BlockSpec((1, d, fc), lambda e, f: (layer * ne + e, 0, f)),
                  pl.BlockSpec((1, d, fc), lambda e, f: (layer * ne + e, 0, nf + f)),
                  pl.BlockSpec((1, fc, d), lambda e, f: (layer * ne + e, f, 0)),
                  full(x1), full(mod), full(ln_g), full(ln_b)],
        out_specs=full(x1),
        scratch_shapes=[pltpu.VMEM((n, d), F32)],
        compiler_params=_cparams(("arbitrary", "arbitrary")),
        name="moe_rows",
    )(u2, comb, w_e1, w_e1, w_e2, x1, mod, ln_g, ln_b)


def _slot_copy(stage, xs_ref, sem, e, start):
    rows = pl.ds(pl.multiple_of(start, SUBLANES), MOE_BLOCK)
    return pltpu.make_async_copy(stage.at[e], xs_ref.at[rows, :], sem.at[e])


def _dispatch_kernel(start_ref, u_ref, li_ref, xs_in_ref, xs_ref, stage, sem):
    del xs_in_ref
    b = pl.program_id(0)
    u = u_ref[...]
    tb = u.shape[0]
    slot = lax.broadcasted_iota(jnp.int32, (tb, tb), 0).astype(F32)
    for e in range(N_EXPERTS):
        sel = jnp.where(slot == li_ref[e:e + 1, :], 1.0, 0.0).astype(BF16)
        rows = jnp.dot(sel, u, preferred_element_type=F32)

        @pl.when(b > 0)
        def _():
            _slot_copy(stage, xs_ref, sem, e, 0).wait()

        stage[e] = rows
        _slot_copy(stage, xs_ref, sem, e, start_ref[b * N_EXPERTS + e]).start()

    @pl.when(b == pl.num_programs(0) - 1)
    def _():
        for e in range(N_EXPERTS):
            _slot_copy(stage, xs_ref, sem, e, 0).wait()


def _dispatch(start, u2, li_t, n_slots):
    n, d = u2.shape
    er = li_t.shape[0]
    tb = MOE_BLOCK
    zeros = jnp.zeros((n_slots, d), F32)
    return pl.pallas_call(
        _dispatch_kernel,
        out_shape=jax.ShapeDtypeStruct((n_slots, d), F32),
        grid_spec=pltpu.PrefetchScalarGridSpec(
            num_scalar_prefetch=1,
            grid=(n // tb,),
            in_specs=[pl.BlockSpec((tb, d), lambda b, s: (b, 0)),
                      pl.BlockSpec((er, tb), lambda b, s: (0, b)),
                      pl.BlockSpec(memory_space=pl.ANY)],
            out_specs=pl.BlockSpec(memory_space=pl.ANY),
            scratch_shapes=[pltpu.VMEM((N_EXPERTS, tb, d), F32), pltpu.SemaphoreType.DMA((N_EXPERTS,))]),
        input_output_aliases={3: 0},
        compiler_params=_cparams(("arbitrary",)),
        name="moe_dispatch",
    )(start, u2, li_t, zeros)


def _expert_kernel(te_ref, tn_ref, x_ref, wg_ref, wu_ref, w2_ref, o_ref, xb, wgb, wub, w2b):
    j = pl.program_id(0)
    f = pl.program_id(1)
    n_sub = tn_ref[j]

    @pl.when(f == 0)
    def _():
        xb[...] = x_ref[...].astype(BF16)
        o_ref[...] = jnp.zeros_like(o_ref)

    @pl.when(n_sub > 0)
    def _():
        wgb[...] = wg_ref[0].astype(BF16)
        wub[...] = wu_ref[0].astype(BF16)
        w2b[...] = w2_ref[0].astype(BF16)

        def sub(i, carry):
            rows = pl.ds(pl.multiple_of(i * MOE_BLOCK, MOE_BLOCK), MOE_BLOCK)
            x = xb[rows, :]
            g = jnp.dot(x, wgb[...], preferred_element_type=F32)
            up = jnp.dot(x, wub[...], preferred_element_type=F32)
            h = (_silu(g) * up).astype(BF16)
            o_ref[rows, :] += jnp.dot(h, w2b[...], preferred_element_type=F32)
            return carry

        lax.fori_loop(0, n_sub, sub, 0)


def _experts(tile_e, tile_n, xs, w_e1, w_e2, layer, *, fc):
    n_slots, d = xs.shape
    f_hidden = w_e2.shape[1]
    nf = f_hidden // fc
    ne = N_EXPERTS
    tg = MOE_TILE

    def fsel(j, f, tn):
        return jnp.where(tn[j] > 0, f, nf - 1)

    return pl.pallas_call(
        _expert_kernel,
        out_shape=jax.ShapeDtypeStruct((n_slots, d), F32),
        grid_spec=pltpu.PrefetchScalarGridSpec(
            num_scalar_prefetch=2,
            grid=(n_slots // tg, nf),
            in_specs=[pl.BlockSpec((tg, d), lambda j, f, te, tn: (j, 0)),
                      pl.BlockSpec((1, d, fc), lambda j, f, te, tn: (layer * ne + te[j], 0, fsel(j, f, tn))),
                      pl.BlockSpec((1, d, fc), lambda j, f, te, tn: (layer * ne + te[j], 0, nf + fsel(j, f, tn))),
                      pl.BlockSpec((1, fc, d), lambda j, f, te, tn: (layer * ne + te[j], fsel(j, f, tn), 0))],
            out_specs=pl.BlockSpec((tg, d), lambda j, f, te, tn: (j, 0)),
            scratch_shapes=[pltpu.VMEM((tg, d), BF16), pltpu.VMEM((d, fc), BF16),
                            pltpu.VMEM((d, fc), BF16), pltpu.VMEM((fc, d), BF16)]),
        compiler_params=_cparams(("arbitrary", "arbitrary")),
        name="moe_experts",
    )(tile_e, tile_n, xs, w_e1, w_e1, w_e2)


def _fetch_copy(ys_ref, buf, sem, slot, e, start):
    rows = pl.ds(pl.multiple_of(start, SUBLANES), MOE_BLOCK)
    return pltpu.make_async_copy(ys_ref.at[rows, :], buf.at[slot, e], sem.at[slot, e])


def _combine_kernel(start_ref, li_ref, comb_ref, ys_ref, x1_ref, mod_ref, lg_ref, lb_ref, o_ref, buf, sem,
                    *, alpha):
    b = pl.program_id(0)
    nb = pl.num_programs(0)

    def fetch(blk, slot):
        for e in range(N_EXPERTS):
            _fetch_copy(ys_ref, buf, sem, slot, e, start_ref[blk * N_EXPERTS + e]).start()

    @pl.when(b == 0)
    def _():
        fetch(0, 0)

    @pl.when(b + 1 < nb)
    def _():
        fetch(b + 1, (b + 1) % 2)

    cur = b % 2
    tb = x1_ref.shape[0]
    slot = lax.broadcasted_iota(jnp.int32, (tb, tb), 0).astype(F32)
    acc = jnp.zeros(x1_ref.shape, F32)
    for e in range(N_EXPERTS):
        _fetch_copy(ys_ref, buf, sem, cur, e, 0).wait()
        gate_t = jnp.where(slot == li_ref[e:e + 1, :], comb_ref[e:e + 1, :], 0.0).astype(BF16)
        y = buf[cur, e].astype(BF16)
        acc = acc + lax.dot_general(gate_t, y, (((0,), (0,)), ((), ())), preferred_element_type=F32)
    o_ref[...] = _post_norm2(x1_ref[...], acc, mod_ref, lg_ref, lb_ref, False, alpha)


def _combine(start, li_t, comb_t, ys, x1, mod, ln_g, ln_b, *, tiles_per_seq, alpha):
    n, d = x1.shape
    er = li_t.shape[0]
    tb = MOE_BLOCK
    full = lambda a: pl.BlockSpec(a.shape, lambda b, s: (0,) * a.ndim)
    return pl.pallas_call(
        functools.partial(_combine_kernel, alpha=alpha),
        out_shape=jax.ShapeDtypeStruct((n, d), F32),
        grid_spec=pltpu.PrefetchScalarGridSpec(
            num_scalar_prefetch=1,
            grid=(n // tb,),
            in_specs=[pl.BlockSpec((er, tb), lambda b, s: (0, b)),
                      pl.BlockSpec((er, tb), lambda b, s: (0, b)),
                      pl.BlockSpec(memory_space=pl.ANY),
                      pl.BlockSpec((tb, d), lambda b, s: (b, 0)),
                      pl.BlockSpec((1, 6, d), lambda b, s: (b // tiles_per_seq, 0, 0)),
                      full(ln_g), full(ln_b)],
            out_specs=pl.BlockSpec((tb, d), lambda b, s: (b, 0)),
            scratch_shapes=[pltpu.VMEM((2, N_EXPERTS, tb, d), F32), pltpu.SemaphoreType.DMA((2, N_EXPERTS))]),
        compiler_params=_cparams(("arbitrary",)),
        name="moe_combine",
    )(start, li_t, comb_t, ys, x1, mod, ln_g, ln_b)


def _routing_tables(cum_t, n_tokens, n_slots):
    ne, tb, tg = N_EXPERTS, MOE_BLOCK, MOE_TILE
    nb = n_tokens // tb
    cum = cum_t.reshape(cum_t.shape[0], nb, LANES)[:ne, :, 0].T.astype(jnp.int32)
    total = cum[-1]
    base = jnp.concatenate([jnp.zeros((1, ne), jnp.int32), cum[:-1]], axis=0)
    region = (total + tb + tg - 1) // tg * tg
    end = jnp.cumsum(region)
    off = end - region
    start = (off[None, :] + base).reshape(-1)
    tile_row = jnp.arange(n_slots // tg, dtype=jnp.int32) * tg
    tile_e = jnp.minimum(jnp.sum(tile_row[:, None] >= end[None, :], axis=1), ne - 1).astype(jnp.int32)
    left = total[tile_e] - (tile_row - off[tile_e])
    tile_n = jnp.clip((left + tb - 1) // tb, 0, tg // tb).astype(jnp.int32)
    return start, tile_e, tile_n


def kernel(x_prompt, x_sample, cache_k, cache_v, state_conv, c_prompt, c_sample, w_ada, b_ada, w_in,
           conv_w, g_conv_out, g_att_out, w_out, ln1_g, ln1_b, ln2_g, ln2_b, w_ff1, w_ff2, w_router,
           w_e1, w_e2):
    bsz, s_len, d = x_prompt.shape
    nb_s, t_new, _ = x_sample.shape
    depth = w_in.shape[0]
    dc = conv_w.shape[2]
    da = w_in.shape[2] // 3 - dc
    assert t_new == 1 and da == N_HEADS * HEAD_DIM and da // LANES * LANES == da
    assert s_len % (Q_BLOCK * max(dl for _, dl in DILATED_PATTERNS)) == 0
    assert cache_k.shape[2] == max(w for w, _ in DILATED_PATTERNS)
    alpha = (2 * depth) ** 0.25
    n_p = bsz * s_len
    keep = min(cache_k.shape[2], s_len)
    n_exp, f_exp = w_e2.shape[1], w_e2.shape[2]
    assert n_exp == N_EXPERTS
    n_slots = n_p * TOP_K + n_exp * ((n_p // MOE_BLOCK) * (SUBLANES - 1) + MOE_BLOCK + MOE_TILE - 1)
    n_slots = (n_slots + MOE_TILE - 1) // MOE_TILE * MOE_TILE

    slopes = jnp.exp2(-8.0 * jnp.arange(1, N_HEADS + 1, dtype=F32) / N_HEADS)
    slopes_pairs = jnp.broadcast_to(slopes.reshape(N_HEADS // 2, 2, 1), (N_HEADS // 2, 2, 2 * Q_BLOCK))
    slopes_col = slopes.reshape(N_HEADS, 1, 1)
    w_buf = cache_k.shape[2]
    cache_kt = cache_k.transpose(0, 1, 3, 4, 2).reshape(depth * nb_s, N_HEADS, HEAD_DIM, w_buf)
    cache_vt = cache_v.transpose(0, 1, 3, 4, 2).reshape(depth * nb_s, N_HEADS, HEAD_DIM, w_buf)

    rows_c = (bsz + nb_s + 7) // 8 * 8
    c_all = jnp.zeros((rows_c, d), F32).at[:bsz].set(c_prompt).at[bsz:bsz + nb_s].set(c_sample)
    ada = _adaln_all(c_all, w_ada, b_ada)

    w_e1f = w_e1.reshape((-1,) + w_e1.shape[2:])
    w_e2f = w_e2.reshape((-1,) + w_e2.shape[2:])

    xp = x_prompt.reshape(n_p, d)
    xs = x_sample.reshape(nb_s, d)
    tm_p = 512
    assert (s_len - keep) % tm_p == 0
    kt_all = jnp.zeros((depth, bsz, da, keep), F32)
    vt_all = jnp.zeros((depth, bsz, da, keep), F32)
    outs = {k: [] for k in ("cp", "ks", "vs", "cs")}
    row2 = lambda a: a.reshape(1, -1)
    for l in range(depth):
        moe = l % 2 == 1
        li = l // 2
        mod_p = ada[l, :bsz].reshape(bsz, 6, d)
        mod_s = ada[l, bsz:bsz + nb_s].reshape(nb_s, 6, d).transpose(1, 0, 2)
        w_in_b = w_in[l].astype(BF16)
        w_out_b = w_out[l].astype(BF16)
        lnp = (row2(ln1_g[l]), row2(ln1_b[l]))
        ln2 = (row2(ln2_g[l]), row2(ln2_b[l]))
        norm_w = (conv_w[l], row2(g_conv_out[l]), row2(g_att_out[l]))

        bg, z, qe, qo, k, v, kt_all, vt_all = _in_proj_prompt(xp, mod_p, w_in_b, kt_all, vt_all, l,
                                                              tm=tm_p, s_len=s_len)
        att = _prompt_attention(qe, qo, k, v, slopes_pairs, bsz, s_len)
        tm3 = MOE_BLOCK if moe else tm_p
        if moe:
            wr_t = jnp.zeros((16, d), F32).at[:n_exp].set(w_router[li].T).astype(BF16)
            x1, u2, comb_t, li_t, cum_t = _out_proj(
                bg, z, z, z, att, xp, mod_p, *norm_w, w_out_b, *lnp, wr_t,
                per_row=False, tm=tm3, tiles_per_seq=s_len // tm3, alpha=alpha)
            start, tile_e, tile_n = _routing_tables(cum_t, n_p, n_slots)
            xsorted = _dispatch(start, u2, li_t, n_slots)
            ysorted = _experts(tile_e, tile_n, xsorted, w_e1f, w_e2f, li, fc=512)
            xp = _combine(start, li_t, comb_t, ysorted, x1, mod_p, *ln2,
                          tiles_per_seq=s_len // MOE_BLOCK, alpha=alpha)
        else:
            x1, u2 = _out_proj(bg, z, z, z, att, xp, mod_p, *norm_w, w_out_b, *lnp, None,
                               per_row=False, tm=tm3, tiles_per_seq=s_len // tm3, alpha=alpha)
            xp = _dense_ffn(u2, w_ff1, w_ff2, li, x1, mod_p, *ln2, per_row=False, tm=1024,
                            tiles_per_seq=s_len // 1024, fc=256, alpha=alpha)
        outs["cp"].append(z.reshape(bsz, s_len, dc)[:, s_len - (CONV_WIDTH - 1):])

        bg, z, q, k, v = _in_proj_sample(xs, mod_s, w_in_b)
        att = _sample_attention(q, k, v, cache_kt, cache_vt, l, slopes_col)
        st = state_conv[l]
        if moe:
            wr = jnp.zeros((d, LANES), F32).at[:, :n_exp].set(w_router[li]).astype(BF16)
            x1, u2, comb = _out_proj(bg, z, st[:, 1], st[:, 0], att, xs, mod_s, *norm_w, w_out_b, *lnp, wr,
                                     per_row=True, tm=nb_s, tiles_per_seq=1, alpha=alpha)
            xs = _moe_rows(u2, comb, w_e1f, w_e2f, li, x1, mod_s, *ln2, fc=512, alpha=alpha)
        else:
            x1, u2 = _out_proj(bg, z, st[:, 1], st[:, 0], att, xs, mod_s, *norm_w, w_out_b, *lnp, None,
                               per_row=True, tm=nb_s, tiles_per_seq=1, alpha=alpha)
            xs = _dense_ffn(u2, w_ff1, w_ff2, li, x1, mod_s, *ln2, per_row=True, tm=nb_s,
                            tiles_per_seq=1, fc=256, alpha=alpha)
        outs["ks"].append(k.reshape(nb_s, 1, N_HEADS, HEAD_DIM))
        outs["vs"].append(v.reshape(nb_s, 1, N_HEADS, HEAD_DIM))
        outs["cs"].append(jnp.stack([st[:, 1], z], axis=1))

    to_rows = lambda t: t.reshape(depth, bsz, N_HEADS, HEAD_DIM, keep).transpose(0, 1, 4, 2, 3)
    return (xp.reshape(bsz, s_len, d), xs.reshape(nb_s, 1, d),
            to_rows(kt_all), to_rows(vt_all), jnp.stack(outs["cp"]),
            jnp.stack(outs["ks"]), jnp.stack(outs["vs"]), jnp.stack(outs["cs"]))
```

```python
import functools

import jax
import jax.numpy as jnp
from jax import lax
from jax.experimental import pallas as pl
from jax.experimental.pallas import tpu as pltpu

F32 = jnp.float32
BF16 = jnp.bfloat16

N_HEADS = 8
HEAD_DIM = 64
CONV_WIDTH = 3
DILATED_PATTERNS = ((128, 1), (512, 4), (2048, 16))
Q_BLOCK = 128
N_EXPERTS = 8
TOP_K = 2
LN_EPS = 1e-5
LOG2E = 1.4426950408889634

LANES = 128
SUBLANES = 8
ATTN_GROUP = 8
MOE_BLOCK = 256
MOE_TILE = 1024
BLOCK_ALIGN = 16
BLOCK_ROWS = -(-(TOP_K * MOE_BLOCK + N_EXPERTS * (BLOCK_ALIGN - 1)) // LANES) * LANES
VMEM_LIMIT = 56 * 1024 * 1024


def _cparams(sem, vmem=VMEM_LIMIT):
    return pltpu.CompilerParams(dimension_semantics=sem, vmem_limit_bytes=vmem)


def _ln(x):
    mu = jnp.mean(x, axis=-1, keepdims=True)
    xc = x - mu
    var = jnp.mean(xc * xc, axis=-1, keepdims=True)
    return xc * lax.rsqrt(var + LN_EPS)


def _silu(x):
    return x * jax.nn.sigmoid(x)


def _mod_rows(mod_ref, k, per_row):
    return mod_ref[k] if per_row else mod_ref[0, k:k + 1, :]


def _mod_spec(per_row, rows, d, tiles_per_seq):
    if per_row:
        return pl.BlockSpec((6, rows, d), lambda i, *_: (0, 0, 0))
    return pl.BlockSpec((1, 6, d), lambda i, *_: (i // tiles_per_seq, 0, 0))


def _ada_kernel(c_ref, w_ref, b_ref, o_ref):
    c = c_ref[...]
    s = _silu(c).astype(BF16)
    o_ref[0] = jnp.dot(s, w_ref[0].astype(BF16), preferred_element_type=F32) + b_ref[0]


def _adaln_all(c_all, w_ada, b_ada):
    depth, d, e6 = w_ada.shape
    rows = c_all.shape[0]
    tn = e6 // 4
    return pl.pallas_call(
        _ada_kernel,
        out_shape=jax.ShapeDtypeStruct((depth, rows, e6), F32),
        grid=(depth, e6 // tn),
        in_specs=[pl.BlockSpec((rows, d), lambda l, j: (0, 0)),
                  pl.BlockSpec((1, d, tn), lambda l, j: (l, 0, j)),
                  pl.BlockSpec((1, 1, tn), lambda l, j: (l, 0, j))],
        out_specs=pl.BlockSpec((1, rows, tn), lambda l, j: (l, 0, j)),
        compiler_params=_cparams(("arbitrary", "arbitrary")),
        name="adaln",
    )(c_all, w_ada, b_ada.reshape(depth, 1, e6))


def _inproj_kernel(*refs, prompt, dc, tiles_per_seq, first_kept):
    if prompt:
        x_ref, mod_ref, w_ref, _, _, bg_ref, z_ref, qe_ref, qo_ref, k_ref, v_ref, kt_ref, vt_ref = refs
    else:
        x_ref, mod_ref, w_ref, bg_ref, z_ref, q_ref, k_ref, v_ref = refs
    shift = _mod_rows(mod_ref, 0, not prompt)
    scale = _mod_rows(mod_ref, 1, not prompt)
    u = (_ln(x_ref[...]) * (1.0 + scale) + shift).astype(BF16)

    def proj(j):
        return jnp.dot(u, w_ref[:, j * dc:(j + 1) * dc], preferred_element_type=F32)

    bg_ref[...] = proj(0)
    z_ref[...] = proj(1) * proj(2)
    q = proj(3)
    k = proj(4)
    v = proj(5)
    k_ref[...] = k
    v_ref[...] = v
    if not prompt:
        q_ref[...] = q
        return
    q = q * (HEAD_DIM ** -0.5 * LOG2E)
    odd = (lax.broadcasted_iota(jnp.int32, q.shape, 1) // HEAD_DIM) % 2 == 1
    qe_ref[...] = jnp.where(odd, 0.0, q)
    qo_ref[...] = jnp.where(odd, q, 0.0)

    @pl.when(pl.program_id(0) % tiles_per_seq >= first_kept)
    def _():
        kt_ref[...] = k.T
        vt_ref[...] = v.T


def _in_proj_prompt(x, mod, w_in_b, kt_buf, vt_buf, layer, *, tm, s_len):
    n, d = x.shape
    dc = w_in_b.shape[1] // 6
    keep = kt_buf.shape[3]
    tps = s_len // tm
    first_kept = (s_len - keep) // tm
    row = lambda: pl.BlockSpec((tm, dc), lambda i: (i, 0))
    kept = lambda: pl.BlockSpec((None, None, dc, tm),
                                lambda i: (layer, i // tps, 0, jnp.maximum(i % tps - first_kept, 0)))
    rows = jax.ShapeDtypeStruct((n, dc), F32)
    outs = pl.pallas_call(
        functools.partial(_inproj_kernel, prompt=True, dc=dc, tiles_per_seq=tps, first_kept=first_kept),
        out_shape=[rows] * 6 + [jax.ShapeDtypeStruct(kt_buf.shape, F32)] * 2,
        grid=(n // tm,),
        in_specs=[pl.BlockSpec((tm, d), lambda i: (i, 0)),
                  _mod_spec(False, tm, d, tps),
                  pl.BlockSpec(w_in_b.shape, lambda i: (0, 0), pipeline_mode=pl.Buffered(1)),
                  pl.BlockSpec(memory_space=pl.ANY), pl.BlockSpec(memory_space=pl.ANY)],
        out_specs=[row() for _ in range(6)] + [kept(), kept()],
        input_output_aliases={3: 6, 4: 7},
        compiler_params=_cparams(("arbitrary",)),
        name="in_proj",
    )(x, mod, w_in_b, kt_buf, vt_buf)
    return outs


def _in_proj_sample(x, mod, w_in_b):
    n, d = x.shape
    dc = w_in_b.shape[1] // 6
    row = lambda: pl.BlockSpec((n, dc), lambda i: (0, 0))
    return pl.pallas_call(
        functools.partial(_inproj_kernel, prompt=False, dc=dc, tiles_per_seq=1, first_kept=0),
        out_shape=[jax.ShapeDtypeStruct((n, dc), F32)] * 5,
        grid=(1,),
        in_specs=[pl.BlockSpec((n, d), lambda i: (0, 0)),
                  _mod_spec(True, n, d, 1),
                  pl.BlockSpec(w_in_b.shape, lambda i: (0, 0), pipeline_mode=pl.Buffered(1))],
        out_specs=[row() for _ in range(5)],
        compiler_params=_cparams(("arbitrary",)),
        name="in_proj_rows",
    )(x, mod, w_in_b)


def _attn_kernel(qe_ref, qo_ref, k_ref, v_ref, sl_ref, o_ref, m_scr, l_scr, bias_scr):
    s_len = k_ref.shape[0]
    qb = Q_BLOCK
    half = lax.broadcasted_iota(jnp.int32, (qb, LANES), 1) >= HEAD_DIM
    ri = lax.broadcasted_iota(jnp.int32, (2 * qb, 2 * qb), 0)
    ji = lax.broadcasted_iota(jnp.int32, (2 * qb, 2 * qb), 1)
    step = qb + (ri % qb) - ji
    band = (step >= 0) & (step <= qb)
    slope = jnp.where(ri < qb, sl_ref[0, 0:1, :], sl_ref[0, 1:2, :]) * LOG2E
    for pi, (_, d) in enumerate(DILATED_PATTERNS):
        bias = jnp.where(band, -(slope * (d * step).astype(F32)), -jnp.inf)
        bias_scr[2 * pi] = bias
        bias_scr[2 * pi + 1] = jnp.where(ji < qb, -jnp.inf, bias)
    ones = jnp.ones((2 * qb, LANES), BF16)

    def block(idx, pi, d, first, last):
        r = idx % d
        n = idx // d
        base = n * (qb * d) + r
        prev = jnp.maximum(n - 1, 0) * (qb * d) + r
        rows = pl.ds(base, qb, stride=d)
        prow = pl.ds(prev, qb, stride=d)
        q2 = jnp.concatenate([qe_ref[rows, :], qo_ref[rows, :]], axis=0).astype(BF16)
        k2 = jnp.concatenate([k_ref[prow, :], k_ref[rows, :]], axis=0).astype(BF16)
        v2 = jnp.concatenate([jnp.concatenate([v_ref[prow, :], v_ref[rows, :]], axis=0).astype(BF16), ones],
                             axis=1)
        s = lax.dot_general(q2, k2, (((1,), (1,)), ((), ())), preferred_element_type=F32)
        s = s + bias_scr[2 * pi + jnp.where(n == 0, 1, 0)]
        s0, s1 = s[:, :qb], s[:, qb:]
        mb = jnp.max(jnp.maximum(s0, s1), axis=1, keepdims=True)
        if first:
            m_new = jnp.broadcast_to(mb, (2 * qb, LANES))
        else:
            m_old = jnp.concatenate([m_scr[0, rows, :], m_scr[1, rows, :]], axis=0)
            m_new = jnp.maximum(m_old, mb)
        p = jnp.concatenate([jnp.exp2(s0 - m_new), jnp.exp2(s1 - m_new)], axis=1).astype(BF16)
        pvs = jnp.dot(p, v2, preferred_element_type=F32)
        pv, rs = pvs[:, :LANES], pvs[:, LANES:]
        if first:
            l_new = rs
            acc = jnp.where(half, pv[qb:], pv[:qb])
        else:
            alpha = jnp.exp2(m_old - m_new)
            l_old = jnp.concatenate([l_scr[0, rows, :], l_scr[1, rows, :]], axis=0)
            l_new = alpha * l_old + rs
            acc_old = o_ref[rows, :]
            acc = jnp.where(half, alpha[qb:] * acc_old + pv[qb:], alpha[:qb] * acc_old + pv[:qb])
        if last:
            acc = acc / jnp.where(half, l_new[qb:], l_new[:qb])
        return rows, m_new, l_new, acc

    order = sorted(range(len(DILATED_PATTERNS)), key=lambda pi: -DILATED_PATTERNS[pi][1])
    for pos, pi in enumerate(order):
        d = DILATED_PATTERNS[pi][1]
        first, last = pos == 0, pos == len(order) - 1

        def body(it, carry, pi=pi, d=d, first=first, last=last):
            done = [block(it * ATTN_GROUP + g, pi, d, first, last) for g in range(ATTN_GROUP)]
            for rows, m_new, l_new, acc in done:
                if not last:
                    m_scr[0, rows, :] = m_new[:qb]
                    m_scr[1, rows, :] = m_new[qb:]
                    l_scr[0, rows, :] = l_new[:qb]
                    l_scr[1, rows, :] = l_new[qb:]
                o_ref[rows, :] = acc
            return carry

        lax.fori_loop(0, s_len // (qb * ATTN_GROUP), body, 0)


def _prompt_attention(qe, qo, k, v, slopes_pairs, bsz, s_len):
    da = k.shape[1]
    pairs = da // LANES
    spec = lambda: pl.BlockSpec((None, s_len, LANES), lambda b, h: (b, 0, h))
    rs = lambda a: a.reshape(bsz, s_len, da)
    out = pl.pallas_call(
        _attn_kernel,
        out_shape=jax.ShapeDtypeStruct((bsz, s_len, da), F32),
        grid=(bsz, pairs),
        in_specs=[spec(), spec(), spec(), spec(),
                  pl.BlockSpec((1, 2, 2 * Q_BLOCK), lambda b, h: (h, 0, 0))],
        out_specs=spec(),
        scratch_shapes=[pltpu.VMEM((2, s_len, LANES), F32), pltpu.VMEM((2, s_len, LANES), F32),
                        pltpu.VMEM((2 * len(DILATED_PATTERNS), 2 * Q_BLOCK, 2 * Q_BLOCK), F32)],
        compiler_params=_cparams(("arbitrary", "arbitrary")),
        name="prompt_attention",
    )(rs(qe), rs(qo), rs(k), rs(v), slopes_pairs)
    return out.reshape(bsz * s_len, da)


def _sattn_kernel(q_ref, kn_ref, vn_ref, kt_ref, vt_ref, sl_ref, o_ref):
    bb, nh, _, w = kt_ref.shape
    dist = w - lax.broadcasted_iota(jnp.int32, (1, w), 1)
    count = jnp.zeros((1, w), F32)
    for window, d in DILATED_PATTERNS:
        count = count + jnp.where((dist % d == 0) & (dist <= window), 1.0, 0.0)
    distf = dist.astype(F32)
    n_pat = float(len(DILATED_PATTERNS))
    scale = HEAD_DIM ** -0.5

    def body(i, carry):
        b = i // nh
        h = i % nh
        q = q_ref[b, h] * scale
        s = jnp.sum(kt_ref[b, h] * q, axis=0, keepdims=True)
        s = jnp.where(count > 0.0, s - sl_ref[h] * distf, -jnp.inf)
        s_self = jnp.sum(q * kn_ref[b, h], axis=0, keepdims=True)
        m = jnp.maximum(jnp.max(s, axis=1, keepdims=True), s_self)
        p = jnp.exp(s - m) * count
        p_self = n_pat * jnp.exp(s_self - m)
        den = jnp.sum(p, axis=1, keepdims=True) + p_self
        num = jnp.sum(vt_ref[b, h] * p, axis=1, keepdims=True) + p_self * vn_ref[b, h]
        o_ref[b, h] = num / den
        return carry

    lax.fori_loop(0, bb * nh, body, 0)


def _sample_attention(q, k, v, cache_kt, cache_vt, layer, slopes_col, *, bb=2):
    nb, da = q.shape
    _, nh, hd, w_buf = cache_kt.shape
    cols = lambda a: a.reshape(nb, nh, hd, 1)
    col = lambda: pl.BlockSpec((bb, nh, hd, 1), lambda i: (i, 0, 0, 0))
    steps = nb // bb
    off = layer * steps
    win = lambda: pl.BlockSpec((bb, nh, hd, w_buf), lambda i: (off + i, 0, 0, 0))
    out = pl.pallas_call(
        _sattn_kernel,
        out_shape=jax.ShapeDtypeStruct((nb, nh, hd, 1), F32),
        grid=(steps,),
        in_specs=[col(), col(), col(), win(), win(), pl.BlockSpec(slopes_col.shape, lambda i: (0, 0, 0))],
        out_specs=col(),
        compiler_params=_cparams(("arbitrary",)),
        name="sample_attention",
    )(cols(q), cols(k), cols(v), cache_kt, cache_vt, slopes_col)
    return out.reshape(nb, da)


def _outproj_kernel(*refs, per_row, moe, router_rows, tiles_per_seq, alpha):
    it = iter(refs)
    bg_ref, z_ref, za_ref, zb_ref, att_ref, x_ref, mod_ref = (next(it) for _ in range(7))
    cw_ref, gc_ref, ga_ref, w_ref, lg_ref, lb_ref = (next(it) for _ in range(6))
    wr_ref = next(it) if moe else None
    x1_ref, u2_ref = next(it), next(it)
    if moe and router_rows:
        comb_ref = next(it)
    elif moe:
        route_ref, cum_ref, carry = (next(it) for _ in range(3))

    i = pl.program_id(0)
    z = z_ref[...]
    tm, dc = z.shape
    if per_row:
        z1, z2 = za_ref[...], zb_ref[...]
    else:
        keep = jnp.where(i % tiles_per_seq == 0, 0.0, 1.0)
        h1 = za_ref[7:8, :] * keep
        h2 = za_ref[6:7, :] * keep
        rid = lax.broadcasted_iota(jnp.int32, (tm, dc), 0)
        z1 = jnp.where(rid == 0, h1, pltpu.roll(z, 1, axis=0))
        z2 = jnp.where(rid == 0, h2, jnp.where(rid == 1, h1, pltpu.roll(z, 2, axis=0)))
    conv = cw_ref[0:1, :] * z2 + cw_ref[1:2, :] * z1 + cw_ref[2:3, :] * z
    yc = bg_ref[...] * conv
    yc = yc * lax.rsqrt(jnp.mean(yc * yc, axis=-1, keepdims=True) + LN_EPS) * gc_ref[...]
    ya = att_ref[...]
    ya = ya * lax.rsqrt(jnp.mean(ya * ya, axis=-1, keepdims=True) + LN_EPS) * ga_ref[...]
    h = (jnp.dot(yc.astype(BF16), w_ref[:dc, :], preferred_element_type=F32)
         + jnp.dot(ya.astype(BF16), w_ref[dc:, :], preferred_element_type=F32))
    gate1 = _mod_rows(mod_ref, 2, per_row)
    x1 = _ln(alpha * x_ref[...] + (1.0 + gate1) * h) * lg_ref[...] + lb_ref[...]
    x1_ref[...] = x1
    u2 = (_ln(x1) * (1.0 + _mod_rows(mod_ref, 4, per_row)) + _mod_rows(mod_ref, 3, per_row)).astype(BF16)
    u2_ref[...] = u2
    if not moe:
        return

    if router_rows:
        lg = jnp.dot(u2, wr_ref[...], preferred_element_type=F32)
        lane = lax.broadcasted_iota(jnp.int32, lg.shape, 1)
        lg = jnp.where(lane < N_EXPERTS, lg, -jnp.inf)
        m1 = jnp.max(lg, axis=1, keepdims=True)
        i1 = jnp.min(jnp.where(lg == m1, lane, LANES), axis=1, keepdims=True)
        lg2 = jnp.where(lane == i1, -jnp.inf, lg)
        m2 = jnp.max(lg2, axis=1, keepdims=True)
        i2 = jnp.min(jnp.where(lg2 == m2, lane, LANES), axis=1, keepdims=True)
        ex = jnp.exp(m2 - m1)
        g1 = 1.0 / (1.0 + ex)
        comb_ref[...] = jnp.where(lane == i1, g1, 0.0) + jnp.where(lane == i2, ex * g1, 0.0)
        return

    lg = lax.dot_general(wr_ref[...], u2, (((1,), (1,)), ((), ())), preferred_element_type=F32)
    row = lax.broadcasted_iota(jnp.int32, lg.shape, 0)
    lg = jnp.where(row < N_EXPERTS, lg, -jnp.inf)
    m1 = jnp.max(lg, axis=0, keepdims=True)
    i1 = jnp.min(jnp.where(lg == m1, row, lg.shape[0]), axis=0, keepdims=True)
    lg2 = jnp.where(row == i1, -jnp.inf, lg)
    m2 = jnp.max(lg2, axis=0, keepdims=True)
    i2 = jnp.min(jnp.where(lg2 == m2, row, lg.shape[0]), axis=0, keepdims=True)
    ex = jnp.exp(m2 - m1)
    g1 = 1.0 / (1.0 + ex)
    sel1, sel2 = row == i1, row == i2
    onehot = jnp.where(sel1 | sel2, 1.0, 0.0)
    ta = lax.broadcasted_iota(jnp.int32, (tm, tm), 0)
    tb = lax.broadcasted_iota(jnp.int32, (tm, tm), 1)
    upper = jnp.where(ta < tb, 1.0, 0.0).astype(BF16)
    rank = jnp.dot(onehot.astype(BF16), upper, preferred_element_type=F32)
    count = jnp.sum(onehot, axis=1, keepdims=True)
    run = jnp.floor((count + (BLOCK_ALIGN - 1.0)) * (1.0 / BLOCK_ALIGN)) * BLOCK_ALIGN
    ea = lax.broadcasted_iota(jnp.int32, (lg.shape[0], lg.shape[0]), 0)
    eb = lax.broadcasted_iota(jnp.int32, (lg.shape[0], lg.shape[0]), 1)
    before = jnp.where(eb < ea, 1.0, 0.0).astype(BF16)
    first = jnp.dot(before, jnp.broadcast_to(run, (lg.shape[0], LANES)).astype(BF16),
                    preferred_element_type=F32)[:, 0:1]
    pos = first + rank
    pos1 = jnp.sum(jnp.where(sel1, pos, 0.0), axis=0, keepdims=True)
    pos2 = jnp.sum(jnp.where(sel2, pos, 0.0), axis=0, keepdims=True)
    r8 = lax.broadcasted_iota(jnp.int32, route_ref.shape, 0)
    route_ref[...] = jnp.where(r8 == 0, pos1, jnp.where(r8 == 1, pos2, jnp.where(
        r8 == 2, g1, jnp.where(r8 == 3, ex * g1, 0.0))))

    @pl.when(i == 0)
    def _():
        carry[...] = jnp.zeros_like(carry)

    carry[...] = carry[...] + jnp.floor((count + (SUBLANES - 1.0)) * (1.0 / SUBLANES)) * SUBLANES
    cum_ref[...] = carry[...]


def _out_proj(bg, z, za, zb, att, x, mod, conv_w, g_conv, g_att, w_out_b, ln_g, ln_b, w_router_b,
              *, per_row, tm, tiles_per_seq, alpha):
    n, d = x.shape
    dc = z.shape[1]
    moe = w_router_b is not None
    router_rows = moe and per_row
    full = lambda a: pl.BlockSpec(a.shape, lambda i: (0,) * a.ndim)
    rowc = lambda: pl.BlockSpec((tm, dc), lambda i: (i, 0))
    rowd = lambda: pl.BlockSpec((tm, d), lambda i: (i, 0))
    if per_row:
        za_spec, zb_spec = rowc(), rowc()
    else:
        za_spec = pl.BlockSpec((8, dc), lambda i: (jnp.maximum(i * (tm // 8) - 1, 0), 0))
        zb_spec = pl.BlockSpec((8, dc), lambda i: (0, 0))
    args = [bg, z, za, zb, att, x, mod, conv_w, g_conv, g_att, w_out_b, ln_g, ln_b]
    in_specs = [rowc(), rowc(), za_spec, zb_spec, rowc(), rowd(), _mod_spec(per_row, tm, d, tiles_per_seq),
                full(conv_w), full(g_conv), full(g_att),
                pl.BlockSpec(w_out_b.shape, lambda i: (0, 0), pipeline_mode=pl.Buffered(1)),
                full(ln_g), full(ln_b)]
    out_shape = [jax.ShapeDtypeStruct((n, d), F32), jax.ShapeDtypeStruct((n, d), BF16)]
    out_specs = [rowd(), rowd()]
    scratch = []
    if moe:
        args.append(w_router_b)
        in_specs.append(full(w_router_b))
        if router_rows:
            out_shape.append(jax.ShapeDtypeStruct((n, LANES), F32))
            out_specs.append(pl.BlockSpec((tm, LANES), lambda i: (i, 0)))
        else:
            er = w_router_b.shape[0]
            out_shape += [jax.ShapeDtypeStruct((SUBLANES, n), F32),
                          jax.ShapeDtypeStruct((er, (n // tm) * LANES), F32)]
            out_specs += [pl.BlockSpec((SUBLANES, tm), lambda i: (0, i)),
                          pl.BlockSpec((er, LANES), lambda i: (0, i))]
            scratch.append(pltpu.VMEM((er, LANES), F32))
    return pl.pallas_call(
        functools.partial(_outproj_kernel, per_row=per_row, moe=moe, router_rows=router_rows,
                          tiles_per_seq=tiles_per_seq, alpha=alpha),
        out_shape=out_shape,
        grid=(n // tm,),
        in_specs=in_specs,
        out_specs=out_specs,
        scratch_shapes=scratch,
        compiler_params=_cparams(("arbitrary",)),
        name="out_proj",
    )(*args)


def _post_norm2(x1, h, mod_ref, lg_ref, lb_ref, per_row, alpha):
    gate2 = _mod_rows(mod_ref, 5, per_row)
    return _ln(alpha * x1 + (1.0 + gate2) * h) * lg_ref[...] + lb_ref[...]


def _ffn_kernel(u_ref, wg_ref, wu_ref, w2_ref, x1_ref, mod_ref, lg_ref, lb_ref, o_ref, acc, wgr, wur, w2r,
                *, per_row, alpha):
    i = pl.program_id(0)
    f = pl.program_id(1)

    @pl.when(f == 0)
    def _():
        acc[...] = jnp.zeros_like(acc)

    @pl.when(i == 0)
    def _():
        wgr[f] = wg_ref[0].astype(BF16)
        wur[f] = wu_ref[0].astype(BF16)
        w2r[f] = w2_ref[0].astype(BF16)

    u = u_ref[...]
    g = jnp.dot(u, wgr[f], preferred_element_type=F32)
    up = jnp.dot(u, wur[f], preferred_element_type=F32)
    h = (_silu(g) * up).astype(BF16)
    acc[...] += jnp.dot(h, w2r[f], preferred_element_type=F32)

    @pl.when(f == pl.num_programs(1) - 1)
    def _():
        o_ref[...] = _post_norm2(x1_ref[...], acc[...], mod_ref, lg_ref, lb_ref, per_row, alpha)


def _dense_ffn(u2, w_ff1, w_ff2, layer, x1, mod, ln_g, ln_b, *, per_row, tm, tiles_per_seq, fc, alpha):
    n, d = x1.shape
    f_hidden = w_ff2.shape[1]
    nf = f_hidden // fc
    full = lambda a: pl.BlockSpec(a.shape, lambda i, f: (0,) * a.ndim)
    mod_spec = (pl.BlockSpec((6, tm, d), lambda i, f: (0, 0, 0)) if per_row
                else pl.BlockSpec((1, 6, d), lambda i, f: (i // tiles_per_seq, 0, 0)))
    fsel = lambda i, f: jnp.where(i == 0, f, nf - 1)
    return pl.pallas_call(
        functools.partial(_ffn_kernel, per_row=per_row, alpha=alpha),
        out_shape=jax.ShapeDtypeStruct((n, d), F32),
        grid=(n // tm, nf),
        in_specs=[pl.BlockSpec((tm, d), lambda i, f: (i, 0)),
                  pl.BlockSpec((1, d, fc), lambda i, f: (layer, 0, fsel(i, f))),
                  pl.BlockSpec((1, d, fc), lambda i, f: (layer, 0, nf + fsel(i, f))),
                  pl.BlockSpec((1, fc, d), lambda i, f: (layer, fsel(i, f), 0)),
                  pl.BlockSpec((tm, d), lambda i, f: (i, 0)),
                  mod_spec, full(ln_g), full(ln_b)],
        out_specs=pl.BlockSpec((tm, d), lambda i, f: (i, 0)),
        scratch_shapes=[pltpu.VMEM((tm, d), F32), pltpu.VMEM((nf, d, fc), BF16),
                        pltpu.VMEM((nf, d, fc), BF16), pltpu.VMEM((nf, fc, d), BF16)],
        compiler_params=_cparams(("arbitrary", "arbitrary")),
        name="dense_ffn",
    )(u2, w_ff1, w_ff1, w_ff2, x1, mod, ln_g, ln_b)


def _moe_rows_kernel(u_ref, comb_ref, wg_ref, wu_ref, w2_ref, x1_ref, mod_ref, lg_ref, lb_ref, o_ref, acc,
                     *, alpha):
    e = pl.program_id(0)
    f = pl.program_id(1)

    @pl.when((e == 0) & (f == 0))
    def _():
        acc[...] = jnp.zeros_like(acc)

    u = u_ref[...]
    g = jnp.dot(u, wg_ref[0].astype(BF16), preferred_element_type=F32)
    up = jnp.dot(u, wu_ref[0].astype(BF16), preferred_element_type=F32)
    h = (_silu(g) * up).astype(BF16)
    y = jnp.dot(h, w2_ref[0].astype(BF16), preferred_element_type=F32)
    comb = comb_ref[...]
    lane = lax.broadcasted_iota(jnp.int32, comb.shape, 1)
    gate = jnp.sum(jnp.where(lane == e, comb, 0.0), axis=1, keepdims=True)
    acc[...] += gate * y

    @pl.when((e == pl.num_programs(0) - 1) & (f == pl.num_programs(1) - 1))
    def _():
        o_ref[...] = _post_norm2(x1_ref[...], acc[...], mod_ref, lg_ref, lb_ref, True, alpha)


def _moe_rows(u2, comb, w_e1, w_e2, layer, x1, mod, ln_g, ln_b, *, fc, alpha):
    n, d = x1.shape
    f_hidden = w_e2.shape[1]
    nf = f_hidden // fc
    ne = N_EXPERTS
    full = lambda a: pl.BlockSpec(a.shape, lambda e, f: (0,) * a.ndim)
    return pl.pallas_call(
        functools.partial(_moe_rows_kernel, alpha=alpha),
        out_shape=jax.ShapeDtypeStruct((n, d), F32),
        grid=(ne, nf),
        in_specs=[full(u2), full(comb),
                  pl.BlockSpec((1, d, fc), lambda e, f: (layer * ne + e, 0, f)),
                  pl.BlockSpec((1, d, fc), lambda e, f: (layer * ne + e, 0, nf + f)),
                  pl.BlockSpec((1, fc, d), lambda e, f: (layer * ne + e, f, 0)),
                  full(x1), full(mod), full(ln_g), full(ln_b)],
        out_specs=full(x1),
        scratch_shapes=[pltpu.VMEM((n, d), F32)],
        compiler_params=_cparams(("arbitrary", "arbitrary")),
        name="moe_rows",
    )(u2, comb, w_e1, w_e1, w_e2, x1, mod, ln_g, ln_b)


def _slot_copy(stage, xs_ref, sem, par, e, first, start):
    src = stage.at[par, pl.ds(pl.multiple_of(first, SUBLANES), MOE_BLOCK), :]
    dst = xs_ref.at[pl.ds(pl.multiple_of(start, SUBLANES), MOE_BLOCK), :]
    return pltpu.make_async_copy(src, dst, sem.at[par, e])


def _dispatch_kernel(start_ref, first_ref, fill_ref, u_ref, rt_ref, xs_ref, stage, sem):
    b = pl.program_id(0)
    nb = pl.num_programs(0)
    par = b % 2
    tb = u_ref.shape[0]

    @pl.when(b < 2)
    def _():
        stage[par, BLOCK_ROWS:, :] = jnp.zeros((MOE_BLOCK, stage.shape[2]), F32)

    slot = lax.broadcasted_iota(jnp.int32, (BLOCK_ROWS, tb), 0).astype(F32)
    sel = jnp.where((slot == rt_ref[0:1, :]) | (slot == rt_ref[1:2, :]), 1.0, 0.0).astype(BF16)
    stage[par, :BLOCK_ROWS, :] = jnp.dot(sel, u_ref[...], preferred_element_type=F32)
    for e in range(N_EXPERTS):
        @pl.when(b > 0)
        def _():
            _slot_copy(stage, xs_ref, sem, 1 - par, e, 0, 0).wait()

        _slot_copy(stage, xs_ref, sem, par, e, first_ref[b * (N_EXPERTS + 1) + e],
                   start_ref[b * N_EXPERTS + e]).start()

    @pl.when(b == nb - 1)
    def _():
        for e in range(N_EXPERTS):
            _slot_copy(stage, xs_ref, sem, par, e, 0, 0).wait()
        stage[1 - par, :MOE_BLOCK, :] = jnp.zeros((MOE_BLOCK, stage.shape[2]), F32)
        for e in range(N_EXPERTS):
            _slot_copy(stage, xs_ref, sem, 1 - par, e, 0, start_ref[nb * N_EXPERTS + e]).start()
        for e in range(N_EXPERTS):
            _slot_copy(stage, xs_ref, sem, 1 - par, e, 0, 0).wait()
        for g in range(N_EXPERTS + 1):
            lo = fill_ref[g]
            hi = fill_ref[N_EXPERTS + 1 + g]
            tiles = (hi - lo) // MOE_BLOCK

            def fill(t, carry, lo=lo):
                _slot_copy(stage, xs_ref, sem, 1 - par, 0, 0, lo + t * MOE_BLOCK).start()
                return carry

            def drain(t, carry):
                _slot_copy(stage, xs_ref, sem, 1 - par, 0, 0, 0).wait()
                return carry

            lax.fori_loop(0, tiles, fill, 0)
            lax.fori_loop(0, tiles, drain, 0)

            @pl.when(lo + tiles * MOE_BLOCK < hi)
            def _():
                _slot_copy(stage, xs_ref, sem, 1 - par, 0, 0, hi - MOE_BLOCK).start()
                _slot_copy(stage, xs_ref, sem, 1 - par, 0, 0, 0).wait()


def _dispatch(start, first, fill, u2, route_t, n_slots):
    n, d = u2.shape
    tb = MOE_BLOCK
    return pl.pallas_call(
        _dispatch_kernel,
        out_shape=jax.ShapeDtypeStruct((n_slots, d), F32),
        grid_spec=pltpu.PrefetchScalarGridSpec(
            num_scalar_prefetch=3,
            grid=(n // tb,),
            in_specs=[pl.BlockSpec((tb, d), lambda b, s, f, z: (b, 0)),
                      pl.BlockSpec((SUBLANES, tb), lambda b, s, f, z: (0, b))],
            out_specs=pl.BlockSpec(memory_space=pl.ANY),
            scratch_shapes=[pltpu.VMEM((2, BLOCK_ROWS + tb, d), F32),
                            pltpu.SemaphoreType.DMA((2, N_EXPERTS))]),
        compiler_params=_cparams(("arbitrary",)),
        name="moe_dispatch",
    )(start, first, fill, u2, route_t)


def _expert_kernel(te_ref, tn_ref, tf_ref, x_ref, wg_ref, wu_ref, w2_ref, o_ref, xb, wgr, wur, w2r):
    j = pl.program_id(0)
    f = pl.program_id(1)
    n_sub = tn_ref[j]
    subs = o_ref.shape[0] // MOE_BLOCK

    @pl.when(f == 0)
    def _():
        o_ref[...] = jnp.zeros_like(o_ref)

        def cast(i, carry):
            rows = pl.ds(pl.multiple_of(i * MOE_BLOCK, MOE_BLOCK), MOE_BLOCK)
            xb[rows, :] = x_ref[rows, :].astype(BF16)
            return carry

        lax.fori_loop(0, n_sub, cast, 0)

    @pl.when(tf_ref[j] == 1)
    def _():
        wgr[f] = wg_ref[0].astype(BF16)
        wur[f] = wu_ref[0].astype(BF16)
        w2r[f] = w2_ref[0].astype(BF16)

    def ffn(rows):
        x = xb[rows, :]
        g = jnp.dot(x, wgr[f], preferred_element_type=F32)
        up = jnp.dot(x, wur[f], preferred_element_type=F32)
        h = (_silu(g) * up).astype(BF16)
        o_ref[rows, :] += jnp.dot(h, w2r[f], preferred_element_type=F32)

    @pl.when(n_sub == subs)
    def _():
        ffn(slice(None))

    @pl.when((n_sub > 0) & (n_sub < subs))
    def _():
        def sub(i, carry):
            ffn(pl.ds(pl.multiple_of(i * MOE_BLOCK, MOE_BLOCK), MOE_BLOCK))
            return carry

        lax.fori_loop(0, n_sub, sub, 0)


def _experts(tile_e, tile_n, tile_first, xs, w_e1, w_e2, layer, *, fc):
    n_slots, d = xs.shape
    f_hidden = w_e2.shape[1]
    nf = f_hidden // fc
    ne = N_EXPERTS
    tg = MOE_TILE

    def fsel(j, f, tf):
        return jnp.where(tf[j] == 1, f, nf - 1)

    return pl.pallas_call(
        _expert_kernel,
        out_shape=jax.ShapeDtypeStruct((n_slots, d), F32),
        grid_spec=pltpu.PrefetchScalarGridSpec(
            num_scalar_prefetch=3,
            grid=(n_slots // tg, nf),
            in_specs=[pl.BlockSpec((tg, d), lambda j, f, te, tn, tf: (j, 0), pipeline_mode=pl.Buffered(1)),
                      pl.BlockSpec((1, d, fc), lambda j, f, te, tn, tf: (layer * ne + te[j], 0, fsel(j, f, tf))),
                      pl.BlockSpec((1, d, fc),
                                   lambda j, f, te, tn, tf: (layer * ne + te[j], 0, nf + fsel(j, f, tf))),
                      pl.BlockSpec((1, fc, d), lambda j, f, te, tn, tf: (layer * ne + te[j], fsel(j, f, tf), 0))],
            out_specs=pl.BlockSpec((tg, d), lambda j, f, te, tn, tf: (j, 0)),
            scratch_shapes=[pltpu.VMEM((tg, d), BF16), pltpu.VMEM((nf, d, fc), BF16),
                            pltpu.VMEM((nf, d, fc), BF16), pltpu.VMEM((nf, fc, d), BF16)]),
        compiler_params=_cparams(("arbitrary", "arbitrary")),
        name="moe_experts",
    )(tile_e, tile_n, tile_first, xs, w_e1, w_e1, w_e2)


def _fetch_copy(ys_ref, buf, sem, slot, e, start):
    rows = pl.ds(pl.multiple_of(start, SUBLANES), MOE_BLOCK)
    return pltpu.make_async_copy(ys_ref.at[rows, :], buf.at[slot, e], sem.at[slot, e])


def _combine_kernel(start_ref, first_ref, rt_ref, ys_ref, x1_ref, mod_ref, lg_ref, lb_ref, o_ref, buf, yc, sem,
                    *, alpha):
    b = pl.program_id(0)
    nb = pl.num_programs(0)

    def fetch(blk, slot):
        for e in range(N_EXPERTS):
            _fetch_copy(ys_ref, buf, sem, slot, e, start_ref[blk * N_EXPERTS + e]).start()

    @pl.when(b == 0)
    def _():
        fetch(0, 0)
        yc[...] = jnp.zeros_like(yc)

    @pl.when(b + 1 < nb)
    def _():
        fetch(b + 1, (b + 1) % 2)

    cur = b % 2
    tb = x1_ref.shape[0]
    for e in range(N_EXPERTS):
        _fetch_copy(ys_ref, buf, sem, cur, e, 0).wait()
        first = first_ref[b * (N_EXPERTS + 1) + e]
        groups = (first_ref[b * (N_EXPERTS + 1) + e + 1] - first) // BLOCK_ALIGN

        def pack(g, carry, e=e, first=first):
            src = pl.ds(pl.multiple_of(g * BLOCK_ALIGN, BLOCK_ALIGN), BLOCK_ALIGN)
            dst = pl.ds(pl.multiple_of(first + g * BLOCK_ALIGN, BLOCK_ALIGN), BLOCK_ALIGN)
            yc[dst, :] = buf[cur, e, src, :].astype(BF16)
            return carry

        lax.fori_loop(0, groups, pack, 0)

    slot = lax.broadcasted_iota(jnp.int32, (BLOCK_ROWS, tb), 0).astype(F32)
    gate_t = (jnp.where(slot == rt_ref[0:1, :], rt_ref[2:3, :], 0.0)
              + jnp.where(slot == rt_ref[1:2, :], rt_ref[3:4, :], 0.0)).astype(BF16)
    acc = lax.dot_general(gate_t, yc[...], (((0,), (0,)), ((), ())), preferred_element_type=F32)
    o_ref[...] = _post_norm2(x1_ref[...], acc, mod_ref, lg_ref, lb_ref, False, alpha)


def _combine(start, first, route_t, ys, x1, mod, ln_g, ln_b, *, tiles_per_seq, alpha):
    n, d = x1.shape
    tb = MOE_BLOCK
    full = lambda a: pl.BlockSpec(a.shape, lambda b, s, f: (0,) * a.ndim)
    return pl.pallas_call(
        functools.partial(_combine_kernel, alpha=alpha),
        out_shape=jax.ShapeDtypeStruct((n, d), F32),
        grid_spec=pltpu.PrefetchScalarGridSpec(
            num_scalar_prefetch=2,
            grid=(n // tb,),
            in_specs=[pl.BlockSpec((SUBLANES, tb), lambda b, s, f: (0, b)),
                      pl.BlockSpec(memory_space=pl.ANY),
                      pl.BlockSpec((tb, d), lambda b, s, f: (b, 0)),
                      pl.BlockSpec((1, 6, d), lambda b, s, f: (b // tiles_per_seq, 0, 0)),
                      full(ln_g), full(ln_b)],
            out_specs=pl.BlockSpec((tb, d), lambda b, s, f: (b, 0)),
            scratch_shapes=[pltpu.VMEM((2, N_EXPERTS, tb, d), F32), pltpu.VMEM((BLOCK_ROWS, d), BF16),
                            pltpu.SemaphoreType.DMA((2, N_EXPERTS))]),
        compiler_params=_cparams(("arbitrary",)),
        name="moe_combine",
    )(start, first, route_t, ys, x1, mod, ln_g, ln_b)


def _routing_tables(cum_t, n_tokens, n_slots):
    ne, tb, tg = N_EXPERTS, MOE_BLOCK, MOE_TILE
    nb = n_tokens // tb
    cum = cum_t.reshape(cum_t.shape[0], nb, LANES)[:ne, :, 0].T.astype(jnp.int32)
    total = cum[-1]
    base = jnp.concatenate([jnp.zeros((1, ne), jnp.int32), cum[:-1]], axis=0)
    region = (total + tb + tg - 1) // tg * tg
    end = jnp.cumsum(region)
    off = end - region
    start = jnp.concatenate([(off[None, :] + base).reshape(-1), off + total])
    run = (cum - base + BLOCK_ALIGN - 1) // BLOCK_ALIGN * BLOCK_ALIGN
    first = jnp.concatenate([jnp.zeros((nb, 1), jnp.int32), jnp.cumsum(run, axis=1)], axis=1).reshape(-1)
    fill = jnp.concatenate([off + total + tb, end[-1:], end, jnp.full((1,), n_slots, jnp.int32)]).astype(jnp.int32)
    tile_row = jnp.arange(n_slots // tg, dtype=jnp.int32) * tg
    tile_e = jnp.minimum(jnp.sum(tile_row[:, None] >= end[None, :], axis=1), ne - 1).astype(jnp.int32)
    left = total[tile_e] - (tile_row - off[tile_e])
    tile_n = jnp.clip((left + tb - 1) // tb, 0, tg // tb).astype(jnp.int32)
    used = tile_n > 0
    tile_first = (used & (tile_row == off[tile_e])).astype(jnp.int32)
    seen = lax.cummax(jnp.where(used, tile_e, -1), axis=0)
    tile_w = jnp.where(seen >= 0, seen, tile_e[jnp.argmax(used)]).astype(jnp.int32)
    return start, first, fill, tile_w, tile_n, tile_first


def kernel(x_prompt, x_sample, cache_k, cache_v, state_conv, c_prompt, c_sample, w_ada, b_ada, w_in,
           conv_w, g_conv_out, g_att_out, w_out, ln1_g, ln1_b, ln2_g, ln2_b, w_ff1, w_ff2, w_router,
           w_e1, w_e2):
    bsz, s_len, d = x_prompt.shape
    nb_s, t_new, _ = x_sample.shape
    depth = w_in.shape[0]
    dc = conv_w.shape[2]
    da = w_in.shape[2] // 3 - dc
    assert t_new == 1 and da == N_HEADS * HEAD_DIM and da // LANES * LANES == da
    assert s_len % (Q_BLOCK * max(dl for _, dl in DILATED_PATTERNS)) == 0
    assert cache_k.shape[2] == max(w for w, _ in DILATED_PATTERNS)
    alpha = (2 * depth) ** 0.25
    n_p = bsz * s_len
    keep = min(cache_k.shape[2], s_len)
    n_exp, f_exp = w_e2.shape[1], w_e2.shape[2]
    assert n_exp == N_EXPERTS
    n_slots = n_p * TOP_K + n_exp * ((n_p // MOE_BLOCK) * (SUBLANES - 1) + MOE_BLOCK + MOE_TILE - 1)
    n_slots = (n_slots + MOE_TILE - 1) // MOE_TILE * MOE_TILE

    slopes = jnp.exp2(-8.0 * jnp.arange(1, N_HEADS + 1, dtype=F32) / N_HEADS)
    slopes_pairs = jnp.broadcast_to(slopes.reshape(N_HEADS // 2, 2, 1), (N_HEADS // 2, 2, 2 * Q_BLOCK))
    slopes_col = slopes.reshape(N_HEADS, 1, 1)
    w_buf = cache_k.shape[2]
    cache_kt = cache_k.transpose(0, 1, 3, 4, 2).reshape(depth * nb_s, N_HEADS, HEAD_DIM, w_buf)
    cache_vt = cache_v.transpose(0, 1, 3, 4, 2).reshape(depth * nb_s, N_HEADS, HEAD_DIM, w_buf)

    rows_c = (bsz + nb_s + 7) // 8 * 8
    c_all = jnp.zeros((rows_c, d), F32).at[:bsz].set(c_prompt).at[bsz:bsz + nb_s].set(c_sample)
    ada = _adaln_all(c_all, w_ada, b_ada)

    w_e1f = w_e1.reshape((-1,) + w_e1.shape[2:])
    w_e2f = w_e2.reshape((-1,) + w_e2.shape[2:])

    xp = x_prompt.reshape(n_p, d)
    xs = x_sample.reshape(nb_s, d)
    tm_p = 512
    assert (s_len - keep) % tm_p == 0
    kt_all = jnp.zeros((depth, bsz, da, keep), F32)
    vt_all = jnp.zeros((depth, bsz, da, keep), F32)
    outs = {k: [] for k in ("cp", "ks", "vs", "cs")}
    row2 = lambda a: a.reshape(1, -1)
    for l in range(depth):
        moe = l % 2 == 1
        li = l // 2
        mod_p = ada[l, :bsz].reshape(bsz, 6, d)
        mod_s = ada[l, bsz:bsz + nb_s].reshape(nb_s, 6, d).transpose(1, 0, 2)
        w_in_b = w_in[l].astype(BF16)
        w_out_b = w_out[l].astype(BF16)
        lnp = (row2(ln1_g[l]), row2(ln1_b[l]))
        ln2 = (row2(ln2_g[l]), row2(ln2_b[l]))
        norm_w = (conv_w[l], row2(g_conv_out[l]), row2(g_att_out[l]))

        bg, z, qe, qo, k, v, kt_all, vt_all = _in_proj_prompt(xp, mod_p, w_in_b, kt_all, vt_all, l,
                                                              tm=tm_p, s_len=s_len)
        att = _prompt_attention(qe, qo, k, v, slopes_pairs, bsz, s_len)
        tm3 = MOE_BLOCK if moe else tm_p
        if moe:
            wr_t = jnp.zeros((16, d), F32).at[:n_exp].set(w_router[li].T).astype(BF16)
            x1, u2, route_t, cum_t = _out_proj(
                bg, z, z, z, att, xp, mod_p, *norm_w, w_out_b, *lnp, wr_t,
                per_row=False, tm=tm3, tiles_per_seq=s_len // tm3, alpha=alpha)
            start, first, fill, tile_w, tile_n, tile_first = _routing_tables(cum_t, n_p, n_slots)
            xsorted = _dispatch(start, first, fill, u2, route_t, n_slots)
            ysorted = _experts(tile_w, tile_n, tile_first, xsorted, w_e1f, w_e2f, li, fc=512)
            xp = _combine(start, first, route_t, ysorted, x1, mod_p, *ln2,
                          tiles_per_seq=s_len // MOE_BLOCK, alpha=alpha)
        else:
            x1, u2 = _out_proj(bg, z, z, z, att, xp, mod_p, *norm_w, w_out_b, *lnp, None,
                               per_row=False, tm=tm3, tiles_per_seq=s_len // tm3, alpha=alpha)
            xp = _dense_ffn(u2, w_ff1, w_ff2, li, x1, mod_p, *ln2, per_row=False, tm=1024,
                            tiles_per_seq=s_len // 1024, fc=256, alpha=alpha)
        outs["cp"].append(z.reshape(bsz, s_len, dc)[:, s_len - (CONV_WIDTH - 1):])

        bg, z, q, k, v = _in_proj_sample(xs, mod_s, w_in_b)
        att = _sample_attention(q, k, v, cache_kt, cache_vt, l, slopes_col)
        st = state_conv[l]
        if moe:
            wr = jnp.zeros((d, LANES), F32).at[:, :n_exp].set(w_router[li]).astype(BF16)
            x1, u2, comb = _out_proj(bg, z, st[:, 1], st[:, 0], att, xs, mod_s, *norm_w, w_out_b, *lnp, wr,
                                     per_row=True, tm=nb_s, tiles_per_seq=1, alpha=alpha)
            xs = _moe_rows(u2, comb, w_e1f, w_e2f, li, x1, mod_s, *ln2, fc=512, alpha=alpha)
        else:
            x1, u2 = _out_proj(bg, z, st[:, 1], st[:, 0], att, xs, mod_s, *norm_w, w_out_b, *lnp, None,
                               per_row=True, tm=nb_s, tiles_per_seq=1, alpha=alpha)
            xs = _dense_ffn(u2, w_ff1, w_ff2, li, x1, mod_s, *ln2, per_row=True, tm=nb_s,
                            tiles_per_seq=1, fc=256, alpha=alpha)
        outs["ks"].append(k.reshape(nb_s, 1, N_HEADS, HEAD_DIM))
        outs["vs"].append(v.reshape(nb_s, 1, N_HEADS, HEAD_DIM))
        outs["cs"].append(jnp.stack([st[:, 1], z], axis=1))

    to_rows = lambda t: t.reshape(depth, bsz, N_HEADS, HEAD_DIM, keep).transpose(0, 1, 4, 2, 3)
    return (xp.reshape(bsz, s_len, d), xs.reshape(nb_s, 1, d),
            to_rows(kt_all), to_rows(vt_all), jnp.stack(outs["cp"]),
            jnp.stack(outs["ks"]), jnp.stack(outs["vs"]), jnp.stack(outs["cs"]))
```

```python
import functools

import jax
import jax.numpy as jnp
from jax import lax
from jax.experimental import pallas as pl
from jax.experimental.pallas import tpu as pltpu

F32 = jnp.float32
BF16 = jnp.bfloat16

N_HEADS = 8
HEAD_DIM = 64
CONV_WIDTH = 3
DILATED_PATTERNS = ((128, 1), (512, 4), (2048, 16))
Q_BLOCK = 128
N_EXPERTS = 8
TOP_K = 2
LN_EPS = 1e-5
LOG2E = 1.4426950408889634

LANES = 128
SUBLANES = 8
SAMPLE_HEAD_GROUP = 4
ATTN_GROUP = 8
MOE_BLOCK = 256
MOE_TILE = 1024
BLOCK_ALIGN = 16
MOE_CHUNK = 32
BLOCK_ROWS = -(-(TOP_K * MOE_BLOCK + N_EXPERTS * (BLOCK_ALIGN - 1)) // LANES) * LANES
VMEM_LIMIT = 56 * 1024 * 1024
EXPERT_VMEM_LIMIT = 61 * 1024 * 1024


def _cparams(sem, vmem=VMEM_LIMIT):
    return pltpu.CompilerParams(dimension_semantics=sem, vmem_limit_bytes=vmem)


def _ln(x):
    mu = jnp.mean(x, axis=-1, keepdims=True)
    xc = x - mu
    var = jnp.mean(xc * xc, axis=-1, keepdims=True)
    return xc * lax.rsqrt(var + LN_EPS)


def _silu(x):
    return x * jax.nn.sigmoid(x)


def _mod_rows(mod_ref, k, per_row):
    return mod_ref[k] if per_row else mod_ref[0, k:k + 1, :]


def _mod_spec(per_row, rows, d, tiles_per_seq):
    if per_row:
        return pl.BlockSpec((6, rows, d), lambda i, *_: (0, 0, 0))
    return pl.BlockSpec((1, 6, d), lambda i, *_: (i // tiles_per_seq, 0, 0))


def _ada_kernel(c_ref, w_ref, b_ref, o_ref):
    c = c_ref[...]
    s = _silu(c).astype(BF16)
    o_ref[0] = jnp.dot(s, w_ref[0].astype(BF16), preferred_element_type=F32) + b_ref[0]


def _adaln_all(c_all, w_ada, b_ada):
    depth, d, e6 = w_ada.shape
    rows = c_all.shape[0]
    tn = e6 // 4
    return pl.pallas_call(
        _ada_kernel,
        out_shape=jax.ShapeDtypeStruct((depth, rows, e6), F32),
        grid=(depth, e6 // tn),
        in_specs=[pl.BlockSpec((rows, d), lambda l, j: (0, 0)),
                  pl.BlockSpec((1, d, tn), lambda l, j: (l, 0, j)),
                  pl.BlockSpec((1, 1, tn), lambda l, j: (l, 0, j))],
        out_specs=pl.BlockSpec((1, rows, tn), lambda l, j: (l, 0, j)),
        compiler_params=_cparams(("arbitrary", "arbitrary")),
        name="adaln",
    )(c_all, w_ada, b_ada.reshape(depth, 1, e6))


def _inproj_kernel(*refs, prompt, dc, tiles_per_seq, first_kept):
    if prompt:
        x_ref, mod_ref, w_ref, _, _, bg_ref, z_ref, qe_ref, qo_ref, k_ref, v_ref, kt_ref, vt_ref = refs
    else:
        x_ref, mod_ref, w_ref, bg_ref, z_ref, q_ref, k_ref, v_ref = refs
    shift = _mod_rows(mod_ref, 0, not prompt)
    scale = _mod_rows(mod_ref, 1, not prompt)
    u = (_ln(x_ref[...]) * (1.0 + scale) + shift).astype(BF16)

    def proj(j):
        return jnp.dot(u, w_ref[:, j * dc:(j + 1) * dc], preferred_element_type=F32)

    bg_ref[...] = proj(0)
    z_ref[...] = proj(1) * proj(2)
    q = proj(3)
    k = proj(4)
    v = proj(5)
    k_ref[...] = k
    v_ref[...] = v
    if not prompt:
        q_ref[...] = q
        return
    q = q * (HEAD_DIM ** -0.5 * LOG2E)
    odd = (lax.broadcasted_iota(jnp.int32, q.shape, 1) // HEAD_DIM) % 2 == 1
    qe_ref[...] = jnp.where(odd, 0.0, q)
    qo_ref[...] = jnp.where(odd, q, 0.0)

    @pl.when(pl.program_id(0) % tiles_per_seq >= first_kept)
    def _():
        kt_ref[...] = k.T
        vt_ref[...] = v.T


def _in_proj_prompt(x, mod, w_in_b, kt_buf, vt_buf, layer, *, tm, s_len):
    n, d = x.shape
    dc = w_in_b.shape[1] // 6
    keep = kt_buf.shape[3]
    tps = s_len // tm
    first_kept = (s_len - keep) // tm
    row = lambda: pl.BlockSpec((tm, dc), lambda i: (i, 0))
    kept = lambda: pl.BlockSpec((None, None, dc, tm),
                                lambda i: (layer, i // tps, 0, jnp.maximum(i % tps - first_kept, 0)))
    rows = jax.ShapeDtypeStruct((n, dc), F32)
    outs = pl.pallas_call(
        functools.partial(_inproj_kernel, prompt=True, dc=dc, tiles_per_seq=tps, first_kept=first_kept),
        out_shape=[rows] * 6 + [jax.ShapeDtypeStruct(kt_buf.shape, F32)] * 2,
        grid=(n // tm,),
        in_specs=[pl.BlockSpec((tm, d), lambda i: (i, 0)),
                  _mod_spec(False, tm, d, tps),
                  pl.BlockSpec(w_in_b.shape, lambda i: (0, 0), pipeline_mode=pl.Buffered(1)),
                  pl.BlockSpec(memory_space=pl.ANY), pl.BlockSpec(memory_space=pl.ANY)],
        out_specs=[row() for _ in range(6)] + [kept(), kept()],
        input_output_aliases={3: 6, 4: 7},
        compiler_params=_cparams(("arbitrary",)),
        name="in_proj",
    )(x, mod, w_in_b, kt_buf, vt_buf)
    return outs


def _in_proj_sample(x, mod, w_in_b):
    n, d = x.shape
    dc = w_in_b.shape[1] // 6
    row = lambda: pl.BlockSpec((n, dc), lambda i: (0, 0))
    return pl.pallas_call(
        functools.partial(_inproj_kernel, prompt=False, dc=dc, tiles_per_seq=1, first_kept=0),
        out_shape=[jax.ShapeDtypeStruct((n, dc), F32)] * 5,
        grid=(1,),
        in_specs=[pl.BlockSpec((n, d), lambda i: (0, 0)),
                  _mod_spec(True, n, d, 1),
                  pl.BlockSpec(w_in_b.shape, lambda i: (0, 0), pipeline_mode=pl.Buffered(1))],
        out_specs=[row() for _ in range(5)],
        compiler_params=_cparams(("arbitrary",)),
        name="in_proj_rows",
    )(x, mod, w_in_b)


def _attn_kernel(qe_ref, qo_ref, k_ref, v_ref, sl_ref, o_ref, m_scr, l_scr, bias_scr):
    s_len = k_ref.shape[0]
    qb = Q_BLOCK
    half = lax.broadcasted_iota(jnp.int32, (qb, LANES), 1) >= HEAD_DIM
    ri = lax.broadcasted_iota(jnp.int32, (2 * qb, 2 * qb), 0)
    ji = lax.broadcasted_iota(jnp.int32, (2 * qb, 2 * qb), 1)
    step = qb + (ri % qb) - ji
    band = (step >= 0) & (step <= qb)
    slope = jnp.where(ri < qb, sl_ref[0, 0:1, :], sl_ref[0, 1:2, :]) * LOG2E
    for pi, (_, d) in enumerate(DILATED_PATTERNS):
        bias = jnp.where(band, -(slope * (d * step).astype(F32)), -jnp.inf)
        bias_scr[2 * pi] = bias
        bias_scr[2 * pi + 1] = jnp.where(ji < qb, -jnp.inf, bias)
    ones = jnp.ones((2 * qb, LANES), BF16)

    def block(idx, pi, d, first, last):
        r = idx % d
        n = idx // d
        base = n * (qb * d) + r
        prev = jnp.maximum(n - 1, 0) * (qb * d) + r
        rows = pl.ds(base, qb, stride=d)
        prow = pl.ds(prev, qb, stride=d)
        q2 = jnp.concatenate([qe_ref[rows, :], qo_ref[rows, :]], axis=0).astype(BF16)
        k2 = jnp.concatenate([k_ref[prow, :], k_ref[rows, :]], axis=0).astype(BF16)
        v2 = jnp.concatenate([jnp.concatenate([v_ref[prow, :], v_ref[rows, :]], axis=0).astype(BF16), ones],
                             axis=1)
        s = lax.dot_general(q2, k2, (((1,), (1,)), ((), ())), preferred_element_type=F32)
        s = s + bias_scr[2 * pi + jnp.where(n == 0, 1, 0)]
        s0, s1 = s[:, :qb], s[:, qb:]
        mb = jnp.max(jnp.maximum(s0, s1), axis=1, keepdims=True)
        if first:
            m_new = jnp.broadcast_to(mb, (2 * qb, LANES))
        else:
            m_old = jnp.concatenate([m_scr[0, rows, :], m_scr[1, rows, :]], axis=0)
            m_new = jnp.maximum(m_old, mb)
        p = jnp.concatenate([jnp.exp2(s0 - m_new), jnp.exp2(s1 - m_new)], axis=1).astype(BF16)
        pvs = jnp.dot(p, v2, preferred_element_type=F32)
        pv, rs = pvs[:, :LANES], pvs[:, LANES:]
        if first:
            l_new = rs
            acc = jnp.where(half, pv[qb:], pv[:qb])
        else:
            alpha = jnp.exp2(m_old - m_new)
            l_old = jnp.concatenate([l_scr[0, rows, :], l_scr[1, rows, :]], axis=0)
            l_new = alpha * l_old + rs
            acc_old = o_ref[rows, :]
            acc = jnp.where(half, alpha[qb:] * acc_old + pv[qb:], alpha[:qb] * acc_old + pv[:qb])
        if last:
            acc = acc / jnp.where(half, l_new[qb:], l_new[:qb])
        return rows, m_new, l_new, acc

    order = sorted(range(len(DILATED_PATTERNS)), key=lambda pi: -DILATED_PATTERNS[pi][1])
    for pos, pi in enumerate(order):
        d = DILATED_PATTERNS[pi][1]
        first, last = pos == 0, pos == len(order) - 1

        def body(it, carry, pi=pi, d=d, first=first, last=last):
            done = [block(it * ATTN_GROUP + g, pi, d, first, last) for g in range(ATTN_GROUP)]
            for rows, m_new, l_new, acc in done:
                if not last:
                    m_scr[0, rows, :] = m_new[:qb]
                    m_scr[1, rows, :] = m_new[qb:]
                    l_scr[0, rows, :] = l_new[:qb]
                    l_scr[1, rows, :] = l_new[qb:]
                o_ref[rows, :] = acc
            return carry

        lax.fori_loop(0, s_len // (qb * ATTN_GROUP), body, 0)


def _prompt_attention(qe, qo, k, v, slopes_pairs, bsz, s_len):
    da = k.shape[1]
    pairs = da // LANES
    spec = lambda: pl.BlockSpec((None, s_len, LANES), lambda b, h: (b, 0, h))
    rs = lambda a: a.reshape(bsz, s_len, da)
    out = pl.pallas_call(
        _attn_kernel,
        out_shape=jax.ShapeDtypeStruct((bsz, s_len, da), F32),
        grid=(bsz, pairs),
        in_specs=[spec(), spec(), spec(), spec(),
                  pl.BlockSpec((1, 2, 2 * Q_BLOCK), lambda b, h: (h, 0, 0))],
        out_specs=spec(),
        scratch_shapes=[pltpu.VMEM((2, s_len, LANES), F32), pltpu.VMEM((2, s_len, LANES), F32),
                        pltpu.VMEM((2 * len(DILATED_PATTERNS), 2 * Q_BLOCK, 2 * Q_BLOCK), F32)],
        compiler_params=_cparams(("arbitrary", "arbitrary")),
        name="prompt_attention",
    )(rs(qe), rs(qo), rs(k), rs(v), slopes_pairs)
    return out.reshape(bsz * s_len, da)


def _sattn_kernel(q_ref, kn_ref, vn_ref, kt_ref, vt_ref, sl_ref, o_ref):
    bb, nh, _, w = kt_ref.shape
    dist = w - lax.broadcasted_iota(jnp.int32, (1, w), 1)
    count = jnp.zeros((1, w), F32)
    for window, d in DILATED_PATTERNS:
        count = count + jnp.where((dist % d == 0) & (dist <= window), 1.0, 0.0)
    distf = dist.astype(F32)
    n_pat = float(len(DILATED_PATTERNS))
    scale = HEAD_DIM ** -0.5

    def head(i):
        b = i // nh
        h = i % nh
        q = q_ref[b, h] * scale
        s = jnp.sum(kt_ref[b, h] * q, axis=0, keepdims=True)
        s = jnp.where(count > 0.0, s - sl_ref[h] * distf, -jnp.inf)
        s_self = jnp.sum(q * kn_ref[b, h], axis=0, keepdims=True)
        m = jnp.maximum(jnp.max(s, axis=1, keepdims=True), s_self)
        p = jnp.exp(s - m) * count
        p_self = n_pat * jnp.exp(s_self - m)
        den = jnp.sum(p, axis=1, keepdims=True) + p_self
        num = jnp.sum(vt_ref[b, h] * p, axis=1, keepdims=True) + p_self * vn_ref[b, h]
        return b, h, num / den

    def body(it, carry):
        for b, h, out in [head(it * SAMPLE_HEAD_GROUP + g) for g in range(SAMPLE_HEAD_GROUP)]:
            o_ref[b, h] = out
        return carry

    lax.fori_loop(0, bb * nh // SAMPLE_HEAD_GROUP, body, 0)


def _sample_attention(q, k, v, cache_kt, cache_vt, layer, slopes_col, *, bb=2):
    nb, da = q.shape
    _, nh, hd, w_buf = cache_kt.shape
    cols = lambda a: a.reshape(nb, nh, hd, 1)
    col = lambda: pl.BlockSpec((bb, nh, hd, 1), lambda i: (i, 0, 0, 0))
    steps = nb // bb
    off = layer * steps
    win = lambda: pl.BlockSpec((bb, nh, hd, w_buf), lambda i: (off + i, 0, 0, 0))
    out = pl.pallas_call(
        _sattn_kernel,
        out_shape=jax.ShapeDtypeStruct((nb, nh, hd, 1), F32),
        grid=(steps,),
        in_specs=[col(), col(), col(), win(), win(), pl.BlockSpec(slopes_col.shape, lambda i: (0, 0, 0))],
        out_specs=col(),
        compiler_params=_cparams(("arbitrary",)),
        name="sample_attention",
    )(cols(q), cols(k), cols(v), cache_kt, cache_vt, slopes_col)
    return out.reshape(nb, da)


def _outproj_kernel(*refs, per_row, moe, router_rows, tiles_per_seq, alpha):
    it = iter(refs)
    bg_ref, z_ref, za_ref, zb_ref, att_ref, x_ref, mod_ref = (next(it) for _ in range(7))
    cw_ref, gc_ref, ga_ref, w_ref, lg_ref, lb_ref = (next(it) for _ in range(6))
    wr_ref = next(it) if moe else None
    x1_ref, u2_ref = next(it), next(it)
    if moe and router_rows:
        comb_ref = next(it)
    elif moe:
        route_ref, cum_ref, carry = (next(it) for _ in range(3))

    i = pl.program_id(0)
    z = z_ref[...]
    tm, dc = z.shape
    if per_row:
        z1, z2 = za_ref[...], zb_ref[...]
    else:
        keep = jnp.where(i % tiles_per_seq == 0, 0.0, 1.0)
        h1 = za_ref[7:8, :] * keep
        h2 = za_ref[6:7, :] * keep
        rid = lax.broadcasted_iota(jnp.int32, (tm, dc), 0)
        z1 = jnp.where(rid == 0, h1, pltpu.roll(z, 1, axis=0))
        z2 = jnp.where(rid == 0, h2, jnp.where(rid == 1, h1, pltpu.roll(z, 2, axis=0)))
    conv = cw_ref[0:1, :] * z2 + cw_ref[1:2, :] * z1 + cw_ref[2:3, :] * z
    yc = bg_ref[...] * conv
    yc = yc * lax.rsqrt(jnp.mean(yc * yc, axis=-1, keepdims=True) + LN_EPS) * gc_ref[...]
    ya = att_ref[...]
    ya = ya * lax.rsqrt(jnp.mean(ya * ya, axis=-1, keepdims=True) + LN_EPS) * ga_ref[...]
    h = (jnp.dot(yc.astype(BF16), w_ref[:dc, :], preferred_element_type=F32)
         + jnp.dot(ya.astype(BF16), w_ref[dc:, :], preferred_element_type=F32))
    gate1 = _mod_rows(mod_ref, 2, per_row)
    x1 = _ln(alpha * x_ref[...] + (1.0 + gate1) * h) * lg_ref[...] + lb_ref[...]
    x1_ref[...] = x1
    u2 = (_ln(x1) * (1.0 + _mod_rows(mod_ref, 4, per_row)) + _mod_rows(mod_ref, 3, per_row)).astype(BF16)
    u2_ref[...] = u2
    if not moe:
        return

    if router_rows:
        lg = jnp.dot(u2, wr_ref[...], preferred_element_type=F32)
        lane = lax.broadcasted_iota(jnp.int32, lg.shape, 1)
        lg = jnp.where(lane < N_EXPERTS, lg, -jnp.inf)
        m1 = jnp.max(lg, axis=1, keepdims=True)
        i1 = jnp.min(jnp.where(lg == m1, lane, LANES), axis=1, keepdims=True)
        lg2 = jnp.where(lane == i1, -jnp.inf, lg)
        m2 = jnp.max(lg2, axis=1, keepdims=True)
        i2 = jnp.min(jnp.where(lg2 == m2, lane, LANES), axis=1, keepdims=True)
        ex = jnp.exp(m2 - m1)
        g1 = 1.0 / (1.0 + ex)
        comb_ref[...] = jnp.where(lane == i1, g1, 0.0) + jnp.where(lane == i2, ex * g1, 0.0)
        return

    lg = lax.dot_general(wr_ref[...], u2, (((1,), (1,)), ((), ())), preferred_element_type=F32)
    row = lax.broadcasted_iota(jnp.int32, lg.shape, 0)
    lg = jnp.where(row < N_EXPERTS, lg, -jnp.inf)
    m1 = jnp.max(lg, axis=0, keepdims=True)
    i1 = jnp.min(jnp.where(lg == m1, row, lg.shape[0]), axis=0, keepdims=True)
    lg2 = jnp.where(row == i1, -jnp.inf, lg)
    m2 = jnp.max(lg2, axis=0, keepdims=True)
    i2 = jnp.min(jnp.where(lg2 == m2, row, lg.shape[0]), axis=0, keepdims=True)
    ex = jnp.exp(m2 - m1)
    g1 = 1.0 / (1.0 + ex)
    sel1, sel2 = row == i1, row == i2
    onehot = jnp.where(sel1 | sel2, 1.0, 0.0)
    ta = lax.broadcasted_iota(jnp.int32, (tm, tm), 0)
    tb = lax.broadcasted_iota(jnp.int32, (tm, tm), 1)
    upper = jnp.where((ta < tb) & (ta // MOE_BLOCK == tb // MOE_BLOCK), 1.0, 0.0).astype(BF16)
    rank = jnp.dot(onehot.astype(BF16), upper, preferred_element_type=F32)
    ea = lax.broadcasted_iota(jnp.int32, (lg.shape[0], lg.shape[0]), 0)
    eb = lax.broadcasted_iota(jnp.int32, (lg.shape[0], lg.shape[0]), 1)
    before = jnp.where(eb < ea, 1.0, 0.0).astype(BF16)

    @pl.when(i == 0)
    def _():
        carry[...] = jnp.zeros_like(carry)

    firsts = []
    for blk in range(tm // MOE_BLOCK):
        count = jnp.sum(onehot[:, blk * MOE_BLOCK:(blk + 1) * MOE_BLOCK], axis=1, keepdims=True)
        run = jnp.floor((count + (BLOCK_ALIGN - 1.0)) * (1.0 / BLOCK_ALIGN)) * BLOCK_ALIGN
        first = jnp.dot(before, jnp.broadcast_to(run, (lg.shape[0], LANES)).astype(BF16),
                        preferred_element_type=F32)[:, 0:1]
        firsts.append(jnp.broadcast_to(first, (lg.shape[0], MOE_BLOCK)))
        carry[...] = carry[...] + jnp.floor((count + (SUBLANES - 1.0)) * (1.0 / SUBLANES)) * SUBLANES
        cum_ref[:, blk * LANES:(blk + 1) * LANES] = carry[...]
    pos = jnp.concatenate(firsts, axis=1) + rank
    pos1 = jnp.sum(jnp.where(sel1, pos, 0.0), axis=0, keepdims=True)
    pos2 = jnp.sum(jnp.where(sel2, pos, 0.0), axis=0, keepdims=True)
    r8 = lax.broadcasted_iota(jnp.int32, route_ref.shape, 0)
    route_ref[...] = jnp.where(r8 == 0, pos1, jnp.where(r8 == 1, pos2, jnp.where(
        r8 == 2, g1, jnp.where(r8 == 3, ex * g1, 0.0))))


def _out_proj(bg, z, za, zb, att, x, mod, conv_w, g_conv, g_att, w_out_b, ln_g, ln_b, w_router_b,
              *, per_row, tm, tiles_per_seq, alpha):
    n, d = x.shape
    dc = z.shape[1]
    moe = w_router_b is not None
    router_rows = moe and per_row
    full = lambda a: pl.BlockSpec(a.shape, lambda i: (0,) * a.ndim)
    rowc = lambda: pl.BlockSpec((tm, dc), lambda i: (i, 0))
    rowd = lambda: pl.BlockSpec((tm, d), lambda i: (i, 0))
    if per_row:
        za_spec, zb_spec = rowc(), rowc()
    else:
        za_spec = pl.BlockSpec((8, dc), lambda i: (jnp.maximum(i * (tm // 8) - 1, 0), 0))
        zb_spec = pl.BlockSpec((8, dc), lambda i: (0, 0))
    args = [bg, z, za, zb, att, x, mod, conv_w, g_conv, g_att, w_out_b, ln_g, ln_b]
    in_specs = [rowc(), rowc(), za_spec, zb_spec, rowc(), rowd(), _mod_spec(per_row, tm, d, tiles_per_seq),
                full(conv_w), full(g_conv), full(g_att),
                pl.BlockSpec(w_out_b.shape, lambda i: (0, 0), pipeline_mode=pl.Buffered(1)),
                full(ln_g), full(ln_b)]
    out_shape = [jax.ShapeDtypeStruct((n, d), F32), jax.ShapeDtypeStruct((n, d), BF16)]
    out_specs = [rowd(), rowd()]
    scratch = []
    if moe:
        args.append(w_router_b)
        in_specs.append(full(w_router_b))
        if router_rows:
            out_shape.append(jax.ShapeDtypeStruct((n, LANES), F32))
            out_specs.append(pl.BlockSpec((tm, LANES), lambda i: (i, 0)))
        else:
            er = w_router_b.shape[0]
            out_shape += [jax.ShapeDtypeStruct((SUBLANES, n), F32),
                          jax.ShapeDtypeStruct((er, (n // MOE_BLOCK) * LANES), F32)]
            out_specs += [pl.BlockSpec((SUBLANES, tm), lambda i: (0, i)),
                          pl.BlockSpec((er, (tm // MOE_BLOCK) * LANES), lambda i: (0, i))]
            scratch.append(pltpu.VMEM((er, LANES), F32))
    return pl.pallas_call(
        functools.partial(_outproj_kernel, per_row=per_row, moe=moe, router_rows=router_rows,
                          tiles_per_seq=tiles_per_seq, alpha=alpha),
        out_shape=out_shape,
        grid=(n // tm,),
        in_specs=in_specs,
        out_specs=out_specs,
        scratch_shapes=scratch,
        compiler_params=_cparams(("arbitrary",)),
        name="out_proj",
    )(*args)


def _post_norm2(x1, h, mod_ref, lg_ref, lb_ref, per_row, alpha):
    gate2 = _mod_rows(mod_ref, 5, per_row)
    return _ln(alpha * x1 + (1.0 + gate2) * h) * lg_ref[...] + lb_ref[...]


def _ffn_kernel(u_ref, wg_ref, wu_ref, w2_ref, x1_ref, mod_ref, lg_ref, lb_ref, o_ref, acc, wgr, wur, w2r,
                *, per_row, alpha):
    i = pl.program_id(0)
    f = pl.program_id(1)

    @pl.when(f == 0)
    def _():
        acc[...] = jnp.zeros_like(acc)

    @pl.when(i == 0)
    def _():
        wgr[f] = wg_ref[0].astype(BF16)
        wur[f] = wu_ref[0].astype(BF16)
        w2r[f] = w2_ref[0].astype(BF16)

    u = u_ref[...]
    g = jnp.dot(u, wgr[f], preferred_element_type=F32)
    up = jnp.dot(u, wur[f], preferred_element_type=F32)
    h = (_silu(g) * up).astype(BF16)
    acc[...] += jnp.dot(h, w2r[f], preferred_element_type=F32)

    @pl.when(f == pl.num_programs(1) - 1)
    def _():
        o_ref[...] = _post_norm2(x1_ref[...], acc[...], mod_ref, lg_ref, lb_ref, per_row, alpha)


def _dense_ffn(u2, w_ff1, w_ff2, layer, x1, mod, ln_g, ln_b, *, per_row, tm, tiles_per_seq, fc, alpha):
    n, d = x1.shape
    f_hidden = w_ff2.shape[1]
    nf = f_hidden // fc
    full = lambda a: pl.BlockSpec(a.shape, lambda i, f: (0,) * a.ndim)
    mod_spec = (pl.BlockSpec((6, tm, d), lambda i, f: (0, 0, 0)) if per_row
                else pl.BlockSpec((1, 6, d), lambda i, f: (i // tiles_per_seq, 0, 0)))
    fsel = lambda i, f: jnp.where(i == 0, f, nf - 1)
    return pl.pallas_call(
        functools.partial(_ffn_kernel, per_row=per_row, alpha=alpha),
        out_shape=jax.ShapeDtypeStruct((n, d), F32),
        grid=(n // tm, nf),
        in_specs=[pl.BlockSpec((tm, d), lambda i, f: (i, 0)),
                  pl.BlockSpec((1, d, fc), lambda i, f: (layer, 0, fsel(i, f))),
                  pl.BlockSpec((1, d, fc), lambda i, f: (layer, 0, nf + fsel(i, f))),
                  pl.BlockSpec((1, fc, d), lambda i, f: (layer, fsel(i, f), 0)),
                  pl.BlockSpec((tm, d), lambda i, f: (i, 0)),
                  mod_spec, full(ln_g), full(ln_b)],
        out_specs=pl.BlockSpec((tm, d), lambda i, f: (i, 0)),
        scratch_shapes=[pltpu.VMEM((tm, d), F32), pltpu.VMEM((nf, d, fc), BF16),
                        pltpu.VMEM((nf, d, fc), BF16), pltpu.VMEM((nf, fc, d), BF16)],
        compiler_params=_cparams(("arbitrary", "arbitrary")),
        name="dense_ffn",
    )(u2, w_ff1, w_ff1, w_ff2, x1, mod, ln_g, ln_b)


def _moe_rows_kernel(u_ref, comb_ref, wg_ref, wu_ref, w2_ref, x1_ref, mod_ref, lg_ref, lb_ref, o_ref, acc,
                     *, alpha):
    e = pl.program_id(0)
    f = pl.program_id(1)

    @pl.when((e == 0) & (f == 0))
    def _():
        acc[...] = jnp.zeros_like(acc)

    u = u_ref[...]
    g = jnp.dot(u, wg_ref[0].astype(BF16), preferred_element_type=F32)
    up = jnp.dot(u, wu_ref[0].astype(BF16), preferred_element_type=F32)
    h = (_silu(g) * up).astype(BF16)
    y = jnp.dot(h, w2_ref[0].astype(BF16), preferred_element_type=F32)
    comb = comb_ref[...]
    lane = lax.broadcasted_iota(jnp.int32, comb.shape, 1)
    gate = jnp.sum(jnp.where(lane == e, comb, 0.0), axis=1, keepdims=True)
    acc[...] += gate * y

    @pl.when((e == pl.num_programs(0) - 1) & (f == pl.num_programs(1) - 1))
    def _():
        o_ref[...] = _post_norm2(x1_ref[...], acc[...], mod_ref, lg_ref, lb_ref, True, alpha)


def _moe_rows(u2, comb, w_e1, w_e2, layer, x1, mod, ln_g, ln_b, *, fc, alpha):
    n, d = x1.shape
    f_hidden = w_e2.shape[1]
    nf = f_hidden // fc
    ne = N_EXPERTS
    full = lambda a: pl.BlockSpec(a.shape, lambda e, f: (0,) * a.ndim)
    return pl.pallas_call(
        functools.partial(_moe_rows_kernel, alpha=alpha),
        out_shape=jax.ShapeDtypeStruct((n, d), F32),
        grid=(ne, nf),
        in_specs=[full(u2), full(comb),
                  pl.BlockSpec((1, d, fc), lambda e, f: (layer * ne + e, 0, f)),
                  pl.BlockSpec((1, d, fc), lambda e, f: (layer * ne + e, 0, nf + f)),
                  pl.BlockSpec((1, fc, d), lambda e, f: (layer * ne + e, f, 0)),
                  full(x1), full(mod), full(ln_g), full(ln_b)],
        out_specs=full(x1),
        scratch_shapes=[pltpu.VMEM((n, d), F32)],
        compiler_params=_cparams(("arbitrary", "arbitrary")),
        name="moe_rows",
    )(u2, comb, w_e1, w_e1, w_e2, x1, mod, ln_g, ln_b)


def _slot_copy(stage, xs_ref, sem, par, e, first, start, rows):
    src = stage.at[par, pl.ds(pl.multiple_of(first, SUBLANES), rows), :]
    dst = xs_ref.at[pl.ds(pl.multiple_of(start, SUBLANES), rows), :]
    return pltpu.make_async_copy(src, dst, sem.at[par, e])


def _run_chunks(first_ref, b, e):
    base = b * (N_EXPERTS + 1) + e
    return (first_ref[base + 1] - first_ref[base] + MOE_CHUNK - 1) // MOE_CHUNK


def _dispatch_kernel(start_ref, first_ref, fill_ref, u_ref, rt_ref, xs_ref, stage, sem):
    b = pl.program_id(0)
    nb = pl.num_programs(0)
    par = b % 2
    tb = u_ref.shape[0]

    @pl.when(b < 2)
    def _():
        stage[par, BLOCK_ROWS:, :] = jnp.zeros((MOE_BLOCK, stage.shape[2]), F32)

    slot = lax.broadcasted_iota(jnp.int32, (BLOCK_ROWS, tb), 0).astype(F32)
    sel = jnp.where((slot == rt_ref[0:1, :]) | (slot == rt_ref[1:2, :]), 1.0, 0.0).astype(BF16)
    stage[par, :BLOCK_ROWS, :] = jnp.dot(sel, u_ref[...], preferred_element_type=F32)
    def wait_chunks(blk, p, e):
        def one(c, carry):
            _slot_copy(stage, xs_ref, sem, p, e, 0, 0, MOE_CHUNK).wait()
            return carry

        lax.fori_loop(0, _run_chunks(first_ref, blk, e), one, 0)

    for e in range(N_EXPERTS):
        @pl.when(b > 0)
        def _():
            wait_chunks(b - 1, 1 - par, e)

        first = first_ref[b * (N_EXPERTS + 1) + e]
        start = start_ref[b * N_EXPERTS + e]

        def send(c, carry, e=e, first=first, start=start):
            _slot_copy(stage, xs_ref, sem, par, e, first + c * MOE_CHUNK, start + c * MOE_CHUNK,
                       MOE_CHUNK).start()
            return carry

        lax.fori_loop(0, _run_chunks(first_ref, b, e), send, 0)

    @pl.when(b == nb - 1)
    def _():
        for e in range(N_EXPERTS):
            wait_chunks(b, par, e)
        stage[1 - par, :MOE_BLOCK, :] = jnp.zeros((MOE_BLOCK, stage.shape[2]), F32)
        for e in range(N_EXPERTS):
            _slot_copy(stage, xs_ref, sem, 1 - par, e, 0, start_ref[nb * N_EXPERTS + e], MOE_BLOCK).start()
        for e in range(N_EXPERTS):
            _slot_copy(stage, xs_ref, sem, 1 - par, e, 0, 0, MOE_BLOCK).wait()
        for g in range(N_EXPERTS + 1):
            lo = fill_ref[g]
            hi = fill_ref[N_EXPERTS + 1 + g]
            tiles = (hi - lo) // MOE_BLOCK

            def fill(t, carry, lo=lo):
                _slot_copy(stage, xs_ref, sem, 1 - par, 0, 0, lo + t * MOE_BLOCK, MOE_BLOCK).start()
                return carry

            def drain(t, carry):
                _slot_copy(stage, xs_ref, sem, 1 - par, 0, 0, 0, MOE_BLOCK).wait()
                return carry

            lax.fori_loop(0, tiles, fill, 0)
            lax.fori_loop(0, tiles, drain, 0)

            @pl.when(lo + tiles * MOE_BLOCK < hi)
            def _():
                _slot_copy(stage, xs_ref, sem, 1 - par, 0, 0, hi - MOE_BLOCK, MOE_BLOCK).start()
                _slot_copy(stage, xs_ref, sem, 1 - par, 0, 0, 0, MOE_BLOCK).wait()


def _dispatch(start, first, fill, u2, route_t, n_slots):
    n, d = u2.shape
    tb = MOE_BLOCK
    return pl.pallas_call(
        _dispatch_kernel,
        out_shape=jax.ShapeDtypeStruct((n_slots, d), F32),
        grid_spec=pltpu.PrefetchScalarGridSpec(
            num_scalar_prefetch=3,
            grid=(n // tb,),
            in_specs=[pl.BlockSpec((tb, d), lambda b, s, f, z: (b, 0)),
                      pl.BlockSpec((SUBLANES, tb), lambda b, s, f, z: (0, b))],
            out_specs=pl.BlockSpec(memory_space=pl.ANY),
            scratch_shapes=[pltpu.VMEM((2, BLOCK_ROWS + tb, d), F32),
                            pltpu.SemaphoreType.DMA((2, N_EXPERTS))]),
        compiler_params=_cparams(("arbitrary",)),
        name="moe_dispatch",
    )(start, first, fill, u2, route_t)


def _expert_kernel(te_ref, tn_ref, tf_ref, x_ref, wg_ref, wu_ref, w2_ref, o_ref, xb, wgr, wur, w2r):
    j = pl.program_id(0)
    f = pl.program_id(1)
    n_sub = tn_ref[j]
    subs = o_ref.shape[0] // MOE_BLOCK

    @pl.when(f == 0)
    def _():
        o_ref[...] = jnp.zeros_like(o_ref)

        def cast(i, carry):
            rows = pl.ds(pl.multiple_of(i * MOE_BLOCK, MOE_BLOCK), MOE_BLOCK)
            xb[rows, :] = x_ref[rows, :].astype(BF16)
            return carry

        lax.fori_loop(0, n_sub, cast, 0)

    @pl.when(tf_ref[j] == 1)
    def _():
        wgr[f] = wg_ref[0].astype(BF16)
        wur[f] = wu_ref[0].astype(BF16)
        w2r[f] = w2_ref[0].astype(BF16)

    def ffn(rows):
        x = xb[rows, :]
        g = jnp.dot(x, wgr[f], preferred_element_type=F32)
        up = jnp.dot(x, wur[f], preferred_element_type=F32)
        h = (_silu(g) * up).astype(BF16)
        o_ref[rows, :] += jnp.dot(h, w2r[f], preferred_element_type=F32)

    @pl.when(n_sub == subs)
    def _():
        ffn(slice(None))

    @pl.when((n_sub > 0) & (n_sub < subs))
    def _():
        def sub(i, carry):
            ffn(pl.ds(pl.multiple_of(i * MOE_BLOCK, MOE_BLOCK), MOE_BLOCK))
            return carry

        lax.fori_loop(0, n_sub, sub, 0)


def _experts(tile_e, tile_n, tile_first, xs, w_e1, w_e2, layer, *, fc):
    n_slots, d = xs.shape
    f_hidden = w_e2.shape[1]
    nf = f_hidden // fc
    ne = N_EXPERTS
    tg = MOE_TILE

    def fsel(j, f, tf):
        return jnp.where(tf[j] == 1, f, nf - 1)

    return pl.pallas_call(
        _expert_kernel,
        out_shape=jax.ShapeDtypeStruct((n_slots, d), F32),
        grid_spec=pltpu.PrefetchScalarGridSpec(
            num_scalar_prefetch=3,
            grid=(n_slots // tg, nf),
            in_specs=[pl.BlockSpec((tg, d), lambda j, f, te, tn, tf: (j, 0)),
                      pl.BlockSpec((1, d, fc), lambda j, f, te, tn, tf: (layer * ne + te[j], 0, fsel(j, f, tf))),
                      pl.BlockSpec((1, d, fc),
                                   lambda j, f, te, tn, tf: (layer * ne + te[j], 0, nf + fsel(j, f, tf))),
                      pl.BlockSpec((1, fc, d), lambda j, f, te, tn, tf: (layer * ne + te[j], fsel(j, f, tf), 0))],
            out_specs=pl.BlockSpec((tg, d), lambda j, f, te, tn, tf: (j, 0)),
            scratch_shapes=[pltpu.VMEM((tg, d), BF16), pltpu.VMEM((nf, d, fc), BF16),
                            pltpu.VMEM((nf, d, fc), BF16), pltpu.VMEM((nf, fc, d), BF16)]),
        compiler_params=_cparams(("arbitrary", "arbitrary"), EXPERT_VMEM_LIMIT),
        name="moe_experts",
    )(tile_e, tile_n, tile_first, xs, w_e1, w_e1, w_e2)


def _fetch_copy(ys_ref, buf, sem, slot, e, row, start):
    src = ys_ref.at[pl.ds(pl.multiple_of(start, SUBLANES), MOE_CHUNK), :]
    dst = buf.at[slot, e, pl.ds(pl.multiple_of(row, MOE_CHUNK), MOE_CHUNK), :]
    return pltpu.make_async_copy(src, dst, sem.at[slot, e])


def _combine_kernel(start_ref, first_ref, rt_ref, ys_ref, x1_ref, mod_ref, lg_ref, lb_ref, o_ref, buf, yc, sem,
                    *, alpha):
    b = pl.program_id(0)
    nb = pl.num_programs(0)

    def fetch(blk, slot):
        for e in range(N_EXPERTS):
            start = start_ref[blk * N_EXPERTS + e]

            def get(c, carry, e=e, start=start):
                _fetch_copy(ys_ref, buf, sem, slot, e, c * MOE_CHUNK, start + c * MOE_CHUNK).start()
                return carry

            lax.fori_loop(0, _run_chunks(first_ref, blk, e), get, 0)

    @pl.when(b == 0)
    def _():
        fetch(0, 0)
        yc[...] = jnp.zeros_like(yc)

    @pl.when(b + 1 < nb)
    def _():
        fetch(b + 1, (b + 1) % 2)

    cur = b % 2
    tb = x1_ref.shape[0]
    for e in range(N_EXPERTS):
        def landed(c, carry, e=e):
            _fetch_copy(ys_ref, buf, sem, cur, e, 0, 0).wait()
            return carry

        lax.fori_loop(0, _run_chunks(first_ref, b, e), landed, 0)
        first = first_ref[b * (N_EXPERTS + 1) + e]
        groups = (first_ref[b * (N_EXPERTS + 1) + e + 1] - first) // BLOCK_ALIGN

        def pack(g, carry, e=e, first=first):
            src = pl.ds(pl.multiple_of(g * BLOCK_ALIGN, BLOCK_ALIGN), BLOCK_ALIGN)
            dst = pl.ds(pl.multiple_of(first + g * BLOCK_ALIGN, BLOCK_ALIGN), BLOCK_ALIGN)
            yc[dst, :] = buf[cur, e, src, :].astype(BF16)
            return carry

        lax.fori_loop(0, groups, pack, 0)

    slot = lax.broadcasted_iota(jnp.int32, (BLOCK_ROWS, tb), 0).astype(F32)
    gate_t = (jnp.where(slot == rt_ref[0:1, :], rt_ref[2:3, :], 0.0)
              + jnp.where(slot == rt_ref[1:2, :], rt_ref[3:4, :], 0.0)).astype(BF16)
    acc = lax.dot_general(gate_t, yc[...], (((0,), (0,)), ((), ())), preferred_element_type=F32)
    o_ref[...] = _post_norm2(x1_ref[...], acc, mod_ref, lg_ref, lb_ref, False, alpha)


def _combine(start, first, route_t, ys, x1, mod, ln_g, ln_b, *, tiles_per_seq, alpha):
    n, d = x1.shape
    tb = MOE_BLOCK
    full = lambda a: pl.BlockSpec(a.shape, lambda b, s, f: (0,) * a.ndim)
    return pl.pallas_call(
        functools.partial(_combine_kernel, alpha=alpha),
        out_shape=jax.ShapeDtypeStruct((n, d), F32),
        grid_spec=pltpu.PrefetchScalarGridSpec(
            num_scalar_prefetch=2,
            grid=(n // tb,),
            in_specs=[pl.BlockSpec((SUBLANES, tb), lambda b, s, f: (0, b)),
                      pl.BlockSpec(memory_space=pl.ANY),
                      pl.BlockSpec((tb, d), lambda b, s, f: (b, 0)),
                      pl.BlockSpec((1, 6, d), lambda b, s, f: (b // tiles_per_seq, 0, 0)),
                      full(ln_g), full(ln_b)],
            out_specs=pl.BlockSpec((tb, d), lambda b, s, f: (b, 0)),
            scratch_shapes=[pltpu.VMEM((2, N_EXPERTS, tb, d), F32), pltpu.VMEM((BLOCK_ROWS, d), BF16),
                            pltpu.SemaphoreType.DMA((2, N_EXPERTS))]),
        compiler_params=_cparams(("arbitrary",)),
        name="moe_combine",
    )(start, first, route_t, ys, x1, mod, ln_g, ln_b)


def _routing_tables(cum_t, n_tokens, n_slots):
    ne, tb, tg = N_EXPERTS, MOE_BLOCK, MOE_TILE
    nb = n_tokens // tb
    cum = cum_t.reshape(cum_t.shape[0], nb, LANES)[:ne, :, 0].T.astype(jnp.int32)
    total = cum[-1]
    base = jnp.concatenate([jnp.zeros((1, ne), jnp.int32), cum[:-1]], axis=0)
    region = (total + tb + tg - 1) // tg * tg
    end = jnp.cumsum(region)
    off = end - region
    start = jnp.concatenate([(off[None, :] + base).reshape(-1), off + total])
    run = (cum - base + BLOCK_ALIGN - 1) // BLOCK_ALIGN * BLOCK_ALIGN
    first = jnp.concatenate([jnp.zeros((nb, 1), jnp.int32), jnp.cumsum(run, axis=1)], axis=1).reshape(-1)
    fill = jnp.concatenate([off + total + tb, end[-1:], end, jnp.full((1,), n_slots, jnp.int32)]).astype(jnp.int32)
    tile_row = jnp.arange(n_slots // tg, dtype=jnp.int32) * tg
    tile_e = jnp.minimum(jnp.sum(tile_row[:, None] >= end[None, :], axis=1), ne - 1).astype(jnp.int32)
    left = total[tile_e] - (tile_row - off[tile_e])
    tile_n = jnp.clip((left + tb - 1) // tb, 0, tg // tb).astype(jnp.int32)
    used = tile_n > 0
    tile_first = (used & (tile_row == off[tile_e])).astype(jnp.int32)
    seen = lax.cummax(jnp.where(used, tile_e, -1), axis=0)
    tile_w = jnp.where(seen >= 0, seen, tile_e[jnp.argmax(used)]).astype(jnp.int32)
    return start, first, fill, tile_w, tile_n, tile_first


def kernel(x_prompt, x_sample, cache_k, cache_v, state_conv, c_prompt, c_sample, w_ada, b_ada, w_in,
           conv_w, g_conv_out, g_att_out, w_out, ln1_g, ln1_b, ln2_g, ln2_b, w_ff1, w_ff2, w_router,
           w_e1, w_e2):
    bsz, s_len, d = x_prompt.shape
    nb_s, t_new, _ = x_sample.shape
    depth = w_in.shape[0]
    dc = conv_w.shape[2]
    da = w_in.shape[2] // 3 - dc
    assert t_new == 1 and da == N_HEADS * HEAD_DIM and da // LANES * LANES == da
    assert s_len % (Q_BLOCK * max(dl for _, dl in DILATED_PATTERNS)) == 0
    assert cache_k.shape[2] == max(w for w, _ in DILATED_PATTERNS)
    alpha = (2 * depth) ** 0.25
    n_p = bsz * s_len
    keep = min(cache_k.shape[2], s_len)
    n_exp, f_exp = w_e2.shape[1], w_e2.shape[2]
    assert n_exp == N_EXPERTS
    n_slots = n_p * TOP_K + n_exp * ((n_p // MOE_BLOCK) * (SUBLANES - 1) + MOE_BLOCK + MOE_TILE - 1)
    n_slots = (n_slots + MOE_TILE - 1) // MOE_TILE * MOE_TILE

    slopes = jnp.exp2(-8.0 * jnp.arange(1, N_HEADS + 1, dtype=F32) / N_HEADS)
    slopes_pairs = jnp.broadcast_to(slopes.reshape(N_HEADS // 2, 2, 1), (N_HEADS // 2, 2, 2 * Q_BLOCK))
    slopes_col = slopes.reshape(N_HEADS, 1, 1)
    w_buf = cache_k.shape[2]
    cache_kt = cache_k.transpose(0, 1, 3, 4, 2).reshape(depth * nb_s, N_HEADS, HEAD_DIM, w_buf)
    cache_vt = cache_v.transpose(0, 1, 3, 4, 2).reshape(depth * nb_s, N_HEADS, HEAD_DIM, w_buf)

    rows_c = (bsz + nb_s + 7) // 8 * 8
    c_all = jnp.zeros((rows_c, d), F32).at[:bsz].set(c_prompt).at[bsz:bsz + nb_s].set(c_sample)
    ada = _adaln_all(c_all, w_ada, b_ada)

    w_e1f = w_e1.reshape((-1,) + w_e1.shape[2:])
    w_e2f = w_e2.reshape((-1,) + w_e2.shape[2:])

    xp = x_prompt.reshape(n_p, d)
    xs = x_sample.reshape(nb_s, d)
    tm_p = 512
    assert (s_len - keep) % tm_p == 0
    kt_all = jnp.zeros((depth, bsz, da, keep), F32)
    vt_all = jnp.zeros((depth, bsz, da, keep), F32)
    outs = {k: [] for k in ("cp", "ks", "vs", "cs")}
    row2 = lambda a: a.reshape(1, -1)
    for l in range(depth):
        moe = l % 2 == 1
        li = l // 2
        mod_p = ada[l, :bsz].reshape(bsz, 6, d)
        mod_s = ada[l, bsz:bsz + nb_s].reshape(nb_s, 6, d).transpose(1, 0, 2)
        w_in_b = w_in[l].astype(BF16)
        w_out_b = w_out[l].astype(BF16)
        lnp = (row2(ln1_g[l]), row2(ln1_b[l]))
        ln2 = (row2(ln2_g[l]), row2(ln2_b[l]))
        norm_w = (conv_w[l], row2(g_conv_out[l]), row2(g_att_out[l]))

        bg, z, qe, qo, k, v, kt_all, vt_all = _in_proj_prompt(xp, mod_p, w_in_b, kt_all, vt_all, l,
                                                              tm=tm_p, s_len=s_len)
        att = _prompt_attention(qe, qo, k, v, slopes_pairs, bsz, s_len)
        tm3 = tm_p
        if moe:
            wr_t = jnp.zeros((16, d), F32).at[:n_exp].set(w_router[li].T).astype(BF16)
            x1, u2, route_t, cum_t = _out_proj(
                bg, z, z, z, att, xp, mod_p, *norm_w, w_out_b, *lnp, wr_t,
                per_row=False, tm=tm3, tiles_per_seq=s_len // tm3, alpha=alpha)
            start, first, fill, tile_w, tile_n, tile_first = _routing_tables(cum_t, n_p, n_slots)
            xsorted = _dispatch(start, first, fill, u2, route_t, n_slots)
            ysorted = _experts(tile_w, tile_n, tile_first, xsorted, w_e1f, w_e2f, li, fc=512)
            xp = _combine(start, first, route_t, ysorted, x1, mod_p, *ln2,
                          tiles_per_seq=s_len // MOE_BLOCK, alpha=alpha)
        else:
            x1, u2 = _out_proj(bg, z, z, z, att, xp, mod_p, *norm_w, w_out_b, *lnp, None,
                               per_row=False, tm=tm3, tiles_per_seq=s_len // tm3, alpha=alpha)
            xp = _dense_ffn(u2, w_ff1, w_ff2, li, x1, mod_p, *ln2, per_row=False, tm=1024,
                            tiles_per_seq=s_len // 1024, fc=256, alpha=alpha)
        outs["cp"].append(z.reshape(bsz, s_len, dc)[:, s_len - (CONV_WIDTH - 1):])

        bg, z, q, k, v = _in_proj_sample(xs, mod_s, w_in_b)
        att = _sample_attention(q, k, v, cache_kt, cache_vt, l, slopes_col)
        st = state_conv[l]
        if moe:
            wr = jnp.zeros((d, LANES), F32).at[:, :n_exp].set(w_router[li]).astype(BF16)
            x1, u2, comb = _out_proj(bg, z, st[:, 1], st[:, 0], att, xs, mod_s, *norm_w, w_out_b, *lnp, wr,
                                     per_row=True, tm=nb_s, tiles_per_seq=1, alpha=alpha)
            xs = _moe_rows(u2, comb, w_e1f, w_e2f, li, x1, mod_s, *ln2, fc=512, alpha=alpha)
        else:
            x1, u2 = _out_proj(bg, z, st[:, 1], st[:, 0], att, xs, mod_s, *norm_w, w_out_b, *lnp, None,
                               per_row=True, tm=nb_s, tiles_per_seq=1, alpha=alpha)
            xs = _dense_ffn(u2, w_ff1, w_ff2, li, x1, mod_s, *ln2, per_row=True, tm=nb_s,
                            tiles_per_seq=1, fc=256, alpha=alpha)
        outs["ks"].append(k.reshape(nb_s, 1, N_HEADS, HEAD_DIM))
        outs["vs"].append(v.reshape(nb_s, 1, N_HEADS, HEAD_DIM))
        outs["cs"].append(jnp.stack([st[:, 1], z], axis=1))

    to_rows = lambda t: t.reshape(depth, bsz, N_HEADS, HEAD_DIM, keep).transpose(0, 1, 4, 2, 3)
    return (xp.reshape(bsz, s_len, d), xs.reshape(nb_s, 1, d),
            to_rows(kt_all), to_rows(vt_all), jnp.stack(outs["cp"]),
            jnp.stack(outs["ks"]), jnp.stack(outs["vs"]), jnp.stack(outs["cs"]))
```

```python
import functools

import jax
import jax.numpy as jnp
from jax import lax
from jax.experimental import pallas as pl
from jax.experimental.pallas import tpu as pltpu

F32 = jnp.float32
BF16 = jnp.bfloat16

N_HEADS = 8
HEAD_DIM = 64
CONV_WIDTH = 3
DILATED_PATTERNS = ((128, 1), (512, 4), (2048, 16))
Q_BLOCK = 128
N_EXPERTS = 8
TOP_K = 2
LN_EPS = 1e-5
LOG2E = 1.4426950408889634

LANES = 128
SUBLANES = 8
SAMPLE_HEAD_GROUP = 4
ATTN_GROUP = 8
MOE_BLOCK = 256
MOE_TILE = 1024
BLOCK_ALIGN = 16
MOE_CHUNK = 32
BLOCK_ROWS = -(-(TOP_K * MOE_BLOCK + N_EXPERTS * (BLOCK_ALIGN - 1)) // LANES) * LANES
VMEM_LIMIT = 56 * 1024 * 1024
EXPERT_VMEM_LIMIT = 61 * 1024 * 1024


def _cparams(sem, vmem=VMEM_LIMIT):
    return pltpu.CompilerParams(dimension_semantics=sem, vmem_limit_bytes=vmem)


def _ln(x):
    mu = jnp.mean(x, axis=-1, keepdims=True)
    xc = x - mu
    var = jnp.mean(xc * xc, axis=-1, keepdims=True)
    return xc * lax.rsqrt(var + LN_EPS)


def _silu(x):
    return x * jax.nn.sigmoid(x)


def _mod_rows(mod_ref, k, per_row):
    return mod_ref[k] if per_row else mod_ref[0, k:k + 1, :]


def _mod_spec(per_row, rows, d, tiles_per_seq):
    if per_row:
        return pl.BlockSpec((6, rows, d), lambda i, *_: (0, 0, 0))
    return pl.BlockSpec((1, 6, d), lambda i, *_: (i // tiles_per_seq, 0, 0))


def _ada_kernel(c_ref, w_ref, b_ref, o_ref):
    c = c_ref[...]
    s = _silu(c).astype(BF16)
    o_ref[0] = jnp.dot(s, w_ref[0].astype(BF16), preferred_element_type=F32) + b_ref[0]


def _adaln_all(c_all, w_ada, b_ada):
    depth, d, e6 = w_ada.shape
    rows = c_all.shape[0]
    tn = e6 // 4
    return pl.pallas_call(
        _ada_kernel,
        out_shape=jax.ShapeDtypeStruct((depth, rows, e6), F32),
        grid=(depth, e6 // tn),
        in_specs=[pl.BlockSpec((rows, d), lambda l, j: (0, 0)),
                  pl.BlockSpec((1, d, tn), lambda l, j: (l, 0, j)),
                  pl.BlockSpec((1, 1, tn), lambda l, j: (l, 0, j))],
        out_specs=pl.BlockSpec((1, rows, tn), lambda l, j: (l, 0, j)),
        compiler_params=_cparams(("arbitrary", "arbitrary")),
        name="adaln",
    )(c_all, w_ada, b_ada.reshape(depth, 1, e6))


def _inproj_kernel(*refs, prompt, dc, tiles_per_seq, first_kept):
    if prompt:
        x_ref, mod_ref, w_ref, _, _, bg_ref, z_ref, qe_ref, qo_ref, k_ref, v_ref, kt_ref, vt_ref = refs
    else:
        x_ref, mod_ref, w_ref, bg_ref, z_ref, q_ref, k_ref, v_ref = refs
    shift = _mod_rows(mod_ref, 0, not prompt)
    scale = _mod_rows(mod_ref, 1, not prompt)
    u = (_ln(x_ref[...]) * (1.0 + scale) + shift).astype(BF16)

    def proj(j):
        return jnp.dot(u, w_ref[:, j * dc:(j + 1) * dc], preferred_element_type=F32)

    bg_ref[...] = proj(0)
    z_ref[...] = proj(1) * proj(2)
    q = proj(3)
    k = proj(4)
    v = proj(5)
    k_ref[...] = k
    v_ref[...] = v
    if not prompt:
        q_ref[...] = q
        return
    q = q * (HEAD_DIM ** -0.5 * LOG2E)
    odd = (lax.broadcasted_iota(jnp.int32, q.shape, 1) // HEAD_DIM) % 2 == 1
    qe_ref[...] = jnp.where(odd, 0.0, q)
    qo_ref[...] = jnp.where(odd, q, 0.0)

    @pl.when(pl.program_id(0) % tiles_per_seq >= first_kept)
    def _():
        kt_ref[...] = k.T
        vt_ref[...] = v.T


def _in_proj_prompt(x, mod, w_in_b, kt_buf, vt_buf, layer, *, tm, s_len):
    n, d = x.shape
    dc = w_in_b.shape[1] // 6
    keep = kt_buf.shape[3]
    tps = s_len // tm
    first_kept = (s_len - keep) // tm
    row = lambda: pl.BlockSpec((tm, dc), lambda i: (i, 0))
    kept = lambda: pl.BlockSpec((None, None, dc, tm),
                                lambda i: (layer, i // tps, 0, jnp.maximum(i % tps - first_kept, 0)))
    rows = jax.ShapeDtypeStruct((n, dc), F32)
    outs = pl.pallas_call(
        functools.partial(_inproj_kernel, prompt=True, dc=dc, tiles_per_seq=tps, first_kept=first_kept),
        out_shape=[rows] * 6 + [jax.ShapeDtypeStruct(kt_buf.shape, F32)] * 2,
        grid=(n // tm,),
        in_specs=[pl.BlockSpec((tm, d), lambda i: (i, 0)),
                  _mod_spec(False, tm, d, tps),
                  pl.BlockSpec(w_in_b.shape, lambda i: (0, 0), pipeline_mode=pl.Buffered(1)),
                  pl.BlockSpec(memory_space=pl.ANY), pl.BlockSpec(memory_space=pl.ANY)],
        out_specs=[row() for _ in range(6)] + [kept(), kept()],
        input_output_aliases={3: 6, 4: 7},
        compiler_params=_cparams(("arbitrary",)),
        name="in_proj",
    )(x, mod, w_in_b, kt_buf, vt_buf)
    return outs


def _in_proj_sample(x, mod, w_in_b):
    n, d = x.shape
    dc = w_in_b.shape[1] // 6
    row = lambda: pl.BlockSpec((n, dc), lambda i: (0, 0))
    return pl.pallas_call(
        functools.partial(_inproj_kernel, prompt=False, dc=dc, tiles_per_seq=1, first_kept=0),
        out_shape=[jax.ShapeDtypeStruct((n, dc), F32)] * 5,
        grid=(1,),
        in_specs=[pl.BlockSpec((n, d), lambda i: (0, 0)),
                  _mod_spec(True, n, d, 1),
                  pl.BlockSpec(w_in_b.shape, lambda i: (0, 0), pipeline_mode=pl.Buffered(1))],
        out_specs=[row() for _ in range(5)],
        compiler_params=_cparams(("arbitrary",)),
        name="in_proj_rows",
    )(x, mod, w_in_b)


def _attn_kernel(qe_ref, qo_ref, k_ref, v_ref, sl_ref, o_ref, m_scr, l_scr, bias_scr):
    s_len = k_ref.shape[0]
    qb = Q_BLOCK
    half = lax.broadcasted_iota(jnp.int32, (qb, LANES), 1) >= HEAD_DIM
    ri = lax.broadcasted_iota(jnp.int32, (2 * qb, 2 * qb), 0)
    ji = lax.broadcasted_iota(jnp.int32, (2 * qb, 2 * qb), 1)
    step = qb + (ri % qb) - ji
    band = (step >= 0) & (step <= qb)
    slope = jnp.where(ri < qb, sl_ref[0, 0:1, :], sl_ref[0, 1:2, :]) * LOG2E
    for pi, (_, d) in enumerate(DILATED_PATTERNS):
        bias = jnp.where(band, -(slope * (d * step).astype(F32)), -jnp.inf)
        bias_scr[2 * pi] = bias
        bias_scr[2 * pi + 1] = jnp.where(ji < qb, -jnp.inf, bias)
    ones = jnp.ones((2 * qb, LANES), BF16)

    def block(idx, pi, d, first, last):
        r = idx % d
        n = idx // d
        base = n * (qb * d) + r
        prev = jnp.maximum(n - 1, 0) * (qb * d) + r
        rows = pl.ds(base, qb, stride=d)
        prow = pl.ds(prev, qb, stride=d)
        q2 = jnp.concatenate([qe_ref[rows, :], qo_ref[rows, :]], axis=0).astype(BF16)
        k2 = jnp.concatenate([k_ref[prow, :], k_ref[rows, :]], axis=0).astype(BF16)
        v2 = jnp.concatenate([jnp.concatenate([v_ref[prow, :], v_ref[rows, :]], axis=0).astype(BF16), ones],
                             axis=1)
        s = lax.dot_general(q2, k2, (((1,), (1,)), ((), ())), preferred_element_type=F32)
        s = s + bias_scr[2 * pi + jnp.where(n == 0, 1, 0)]
        s0, s1 = s[:, :qb], s[:, qb:]
        mb = jnp.max(jnp.maximum(s0, s1), axis=1, keepdims=True)
        if first:
            m_new = jnp.broadcast_to(mb, (2 * qb, LANES))
        else:
            m_old = jnp.concatenate([m_scr[0, rows, :], m_scr[1, rows, :]], axis=0)
            m_new = jnp.maximum(m_old, mb)
        p = jnp.concatenate([jnp.exp2(s0 - m_new), jnp.exp2(s1 - m_new)], axis=1).astype(BF16)
        pvs = jnp.dot(p, v2, preferred_element_type=F32)
        pv, rs = pvs[:, :LANES], pvs[:, LANES:]
        if first:
            l_new = rs
            acc = jnp.where(half, pv[qb:], pv[:qb])
        else:
            alpha = jnp.exp2(m_old - m_new)
            l_old = jnp.concatenate([l_scr[0, rows, :], l_scr[1, rows, :]], axis=0)
            l_new = alpha * l_old + rs
            acc_old = o_ref[rows, :]
            acc = jnp.where(half, alpha[qb:] * acc_old + pv[qb:], alpha[:qb] * acc_old + pv[:qb])
        if last:
            acc = acc / jnp.where(half, l_new[qb:], l_new[:qb])
        return rows, m_new, l_new, acc

    order = sorted(range(len(DILATED_PATTERNS)), key=lambda pi: -DILATED_PATTERNS[pi][1])
    for pos, pi in enumerate(order):
        d = DILATED_PATTERNS[pi][1]
        first, last = pos == 0, pos == len(order) - 1

        def body(it, carry, pi=pi, d=d, first=first, last=last):
            done = [block(it * ATTN_GROUP + g, pi, d, first, last) for g in range(ATTN_GROUP)]
            for rows, m_new, l_new, acc in done:
                if not last:
                    m_scr[0, rows, :] = m_new[:qb]
                    m_scr[1, rows, :] = m_new[qb:]
                    l_scr[0, rows, :] = l_new[:qb]
                    l_scr[1, rows, :] = l_new[qb:]
                o_ref[rows, :] = acc
            return carry

        lax.fori_loop(0, s_len // (qb * ATTN_GROUP), body, 0)


def _prompt_attention(qe, qo, k, v, slopes_pairs, bsz, s_len):
    da = k.shape[1]
    pairs = da // LANES
    spec = lambda: pl.BlockSpec((None, s_len, LANES), lambda b, h: (b, 0, h))
    rs = lambda a: a.reshape(bsz, s_len, da)
    out = pl.pallas_call(
        _attn_kernel,
        out_shape=jax.ShapeDtypeStruct((bsz, s_len, da), F32),
        grid=(bsz, pairs),
        in_specs=[spec(), spec(), spec(), spec(),
                  pl.BlockSpec((1, 2, 2 * Q_BLOCK), lambda b, h: (h, 0, 0))],
        out_specs=spec(),
        scratch_shapes=[pltpu.VMEM((2, s_len, LANES), F32), pltpu.VMEM((2, s_len, LANES), F32),
                        pltpu.VMEM((2 * len(DILATED_PATTERNS), 2 * Q_BLOCK, 2 * Q_BLOCK), F32)],
        compiler_params=_cparams(("arbitrary", "arbitrary")),
        name="prompt_attention",
    )(rs(qe), rs(qo), rs(k), rs(v), slopes_pairs)
    return out.reshape(bsz * s_len, da)


def _sattn_kernel(q_ref, kn_ref, vn_ref, kt_ref, vt_ref, sl_ref, o_ref):
    bb, nh, _, w = kt_ref.shape
    dist = w - lax.broadcasted_iota(jnp.int32, (1, w), 1)
    count = jnp.zeros((1, w), F32)
    for window, d in DILATED_PATTERNS:
        count = count + jnp.where((dist % d == 0) & (dist <= window), 1.0, 0.0)
    distf = dist.astype(F32)
    n_pat = float(len(DILATED_PATTERNS))
    scale = HEAD_DIM ** -0.5

    def head(i):
        b = i // nh
        h = i % nh
        q = q_ref[b, h] * scale
        s = jnp.sum(kt_ref[b, h] * q, axis=0, keepdims=True)
        s = jnp.where(count > 0.0, s - sl_ref[h] * distf, -jnp.inf)
        s_self = jnp.sum(q * kn_ref[b, h], axis=0, keepdims=True)
        m = jnp.maximum(jnp.max(s, axis=1, keepdims=True), s_self)
        p = jnp.exp(s - m) * count
        p_self = n_pat * jnp.exp(s_self - m)
        den = jnp.sum(p, axis=1, keepdims=True) + p_self
        num = jnp.sum(vt_ref[b, h] * p, axis=1, keepdims=True) + p_self * vn_ref[b, h]
        return b, h, num / den

    def body(it, carry):
        for b, h, out in [head(it * SAMPLE_HEAD_GROUP + g) for g in range(SAMPLE_HEAD_GROUP)]:
            o_ref[b, h] = out
        return carry

    lax.fori_loop(0, bb * nh // SAMPLE_HEAD_GROUP, body, 0)


def _sample_attention(q, k, v, cache_kt, cache_vt, layer, slopes_col, *, bb=2):
    nb, da = q.shape
    _, nh, hd, w_buf = cache_kt.shape
    cols = lambda a: a.reshape(nb, nh, hd, 1)
    col = lambda: pl.BlockSpec((bb, nh, hd, 1), lambda i: (i, 0, 0, 0))
    steps = nb // bb
    off = layer * steps
    win = lambda: pl.BlockSpec((bb, nh, hd, w_buf), lambda i: (off + i, 0, 0, 0))
    out = pl.pallas_call(
        _sattn_kernel,
        out_shape=jax.ShapeDtypeStruct((nb, nh, hd, 1), F32),
        grid=(steps,),
        in_specs=[col(), col(), col(), win(), win(), pl.BlockSpec(slopes_col.shape, lambda i: (0, 0, 0))],
        out_specs=col(),
        compiler_params=_cparams(("arbitrary",)),
        name="sample_attention",
    )(cols(q), cols(k), cols(v), cache_kt, cache_vt, slopes_col)
    return out.reshape(nb, da)


def _outproj_kernel(*refs, per_row, moe, router_rows, tiles_per_seq, alpha):
    it = iter(refs)
    bg_ref, z_ref, za_ref, zb_ref, att_ref, x_ref, mod_ref = (next(it) for _ in range(7))
    cw_ref, gc_ref, ga_ref, w_ref, lg_ref, lb_ref = (next(it) for _ in range(6))
    wr_ref = next(it) if moe else None
    x1_ref, u2_ref = next(it), next(it)
    if moe and router_rows:
        comb_ref = next(it)
    elif moe:
        route_ref, cum_ref, carry = (next(it) for _ in range(3))

    i = pl.program_id(0)
    z = z_ref[...]
    tm, dc = z.shape
    if per_row:
        z1, z2 = za_ref[...], zb_ref[...]
    else:
        keep = jnp.where(i % tiles_per_seq == 0, 0.0, 1.0)
        h1 = za_ref[7:8, :] * keep
        h2 = za_ref[6:7, :] * keep
        rid = lax.broadcasted_iota(jnp.int32, (tm, dc), 0)
        z1 = jnp.where(rid == 0, h1, pltpu.roll(z, 1, axis=0))
        z2 = jnp.where(rid == 0, h2, jnp.where(rid == 1, h1, pltpu.roll(z, 2, axis=0)))
    conv = cw_ref[0:1, :] * z2 + cw_ref[1:2, :] * z1 + cw_ref[2:3, :] * z
    yc = bg_ref[...] * conv
    yc = yc * lax.rsqrt(jnp.mean(yc * yc, axis=-1, keepdims=True) + LN_EPS) * gc_ref[...]
    ya = att_ref[...]
    ya = ya * lax.rsqrt(jnp.mean(ya * ya, axis=-1, keepdims=True) + LN_EPS) * ga_ref[...]
    h = (jnp.dot(yc.astype(BF16), w_ref[:dc, :], preferred_element_type=F32)
         + jnp.dot(ya.astype(BF16), w_ref[dc:, :], preferred_element_type=F32))
    gate1 = _mod_rows(mod_ref, 2, per_row)
    x1 = _ln(alpha * x_ref[...] + (1.0 + gate1) * h) * lg_ref[...] + lb_ref[...]
    x1_ref[...] = x1
    u2 = (_ln(x1) * (1.0 + _mod_rows(mod_ref, 4, per_row)) + _mod_rows(mod_ref, 3, per_row)).astype(BF16)
    u2_ref[...] = u2
    if not moe:
        return

    if router_rows:
        lg = jnp.dot(u2, wr_ref[...], preferred_element_type=F32)
        lane = lax.broadcasted_iota(jnp.int32, lg.shape, 1)
        lg = jnp.where(lane < N_EXPERTS, lg, -jnp.inf)
        m1 = jnp.max(lg, axis=1, keepdims=True)
        i1 = jnp.min(jnp.where(lg == m1, lane, LANES), axis=1, keepdims=True)
        lg2 = jnp.where(lane == i1, -jnp.inf, lg)
        m2 = jnp.max(lg2, axis=1, keepdims=True)
        i2 = jnp.min(jnp.where(lg2 == m2, lane, LANES), axis=1, keepdims=True)
        ex = jnp.exp(m2 - m1)
        g1 = 1.0 / (1.0 + ex)
        comb_ref[...] = jnp.where(lane == i1, g1, 0.0) + jnp.where(lane == i2, ex * g1, 0.0)
        return

    lg = lax.dot_general(wr_ref[...], u2, (((1,), (1,)), ((), ())), preferred_element_type=F32)
    row = lax.broadcasted_iota(jnp.int32, lg.shape, 0)
    lg = jnp.where(row < N_EXPERTS, lg, -jnp.inf)
    m1 = jnp.max(lg, axis=0, keepdims=True)
    i1 = jnp.min(jnp.where(lg == m1, row, lg.shape[0]), axis=0, keepdims=True)
    lg2 = jnp.where(row == i1, -jnp.inf, lg)
    m2 = jnp.max(lg2, axis=0, keepdims=True)
    i2 = jnp.min(jnp.where(lg2 == m2, row, lg.shape[0]), axis=0, keepdims=True)
    ex = jnp.exp(m2 - m1)
    g1 = 1.0 / (1.0 + ex)
    sel1, sel2 = row == i1, row == i2
    onehot = jnp.where(sel1 | sel2, 1.0, 0.0)
    ta = lax.broadcasted_iota(jnp.int32, (tm, tm), 0)
    tb = lax.broadcasted_iota(jnp.int32, (tm, tm), 1)
    upper = jnp.where((ta < tb) & (ta // MOE_BLOCK == tb // MOE_BLOCK), 1.0, 0.0).astype(BF16)
    rank = jnp.dot(onehot.astype(BF16), upper, preferred_element_type=F32)
    ea = lax.broadcasted_iota(jnp.int32, (lg.shape[0], lg.shape[0]), 0)
    eb = lax.broadcasted_iota(jnp.int32, (lg.shape[0], lg.shape[0]), 1)
    before = jnp.where(eb < ea, 1.0, 0.0).astype(BF16)

    @pl.when(i == 0)
    def _():
        carry[...] = jnp.zeros_like(carry)

    firsts = []
    for blk in range(tm // MOE_BLOCK):
        count = jnp.sum(onehot[:, blk * MOE_BLOCK:(blk + 1) * MOE_BLOCK], axis=1, keepdims=True)
        run = jnp.floor((count + (BLOCK_ALIGN - 1.0)) * (1.0 / BLOCK_ALIGN)) * BLOCK_ALIGN
        first = jnp.dot(before, jnp.broadcast_to(run, (lg.shape[0], LANES)).astype(BF16),
                        preferred_element_type=F32)[:, 0:1]
        firsts.append(jnp.broadcast_to(first, (lg.shape[0], MOE_BLOCK)))
        carry[...] = carry[...] + jnp.floor((count + (SUBLANES - 1.0)) * (1.0 / SUBLANES)) * SUBLANES
        cum_ref[:, blk * LANES:(blk + 1) * LANES] = carry[...]
    pos = jnp.concatenate(firsts, axis=1) + rank
    pos1 = jnp.sum(jnp.where(sel1, pos, 0.0), axis=0, keepdims=True)
    pos2 = jnp.sum(jnp.where(sel2, pos, 0.0), axis=0, keepdims=True)
    r8 = lax.broadcasted_iota(jnp.int32, route_ref.shape, 0)
    route_ref[...] = jnp.where(r8 == 0, pos1, jnp.where(r8 == 1, pos2, jnp.where(
        r8 == 2, g1, jnp.where(r8 == 3, ex * g1, 0.0))))


def _out_proj(bg, z, za, zb, att, x, mod, conv_w, g_conv, g_att, w_out_b, ln_g, ln_b, w_router_b,
              *, per_row, tm, tiles_per_seq, alpha):
    n, d = x.shape
    dc = z.shape[1]
    moe = w_router_b is not None
    router_rows = moe and per_row
    full = lambda a: pl.BlockSpec(a.shape, lambda i: (0,) * a.ndim)
    rowc = lambda: pl.BlockSpec((tm, dc), lambda i: (i, 0))
    rowd = lambda: pl.BlockSpec((tm, d), lambda i: (i, 0))
    if per_row:
        za_spec, zb_spec = rowc(), rowc()
    else:
        za_spec = pl.BlockSpec((8, dc), lambda i: (jnp.maximum(i * (tm // 8) - 1, 0), 0))
        zb_spec = pl.BlockSpec((8, dc), lambda i: (0, 0))
    args = [bg, z, za, zb, att, x, mod, conv_w, g_conv, g_att, w_out_b, ln_g, ln_b]
    in_specs = [rowc(), rowc(), za_spec, zb_spec, rowc(), rowd(), _mod_spec(per_row, tm, d, tiles_per_seq),
                full(conv_w), full(g_conv), full(g_att),
                pl.BlockSpec(w_out_b.shape, lambda i: (0, 0), pipeline_mode=pl.Buffered(1)),
                full(ln_g), full(ln_b)]
    out_shape = [jax.ShapeDtypeStruct((n, d), F32), jax.ShapeDtypeStruct((n, d), BF16)]
    out_specs = [rowd(), rowd()]
    scratch = []
    if moe:
        args.append(w_router_b)
        in_specs.append(full(w_router_b))
        if router_rows:
            out_shape.append(jax.ShapeDtypeStruct((n, LANES), F32))
            out_specs.append(pl.BlockSpec((tm, LANES), lambda i: (i, 0)))
        else:
            er = w_router_b.shape[0]
            out_shape += [jax.ShapeDtypeStruct((SUBLANES, n), F32),
                          jax.ShapeDtypeStruct((er, (n // MOE_BLOCK) * LANES), F32)]
            out_specs += [pl.BlockSpec((SUBLANES, tm), lambda i: (0, i)),
                          pl.BlockSpec((er, (tm // MOE_BLOCK) * LANES), lambda i: (0, i))]
            scratch.append(pltpu.VMEM((er, LANES), F32))
    return pl.pallas_call(
        functools.partial(_outproj_kernel, per_row=per_row, moe=moe, router_rows=router_rows,
                          tiles_per_seq=tiles_per_seq, alpha=alpha),
        out_shape=out_shape,
        grid=(n // tm,),
        in_specs=in_specs,
        out_specs=out_specs,
        scratch_shapes=scratch,
        compiler_params=_cparams(("arbitrary",)),
        name="out_proj",
    )(*args)


def _post_norm2(x1, h, mod_ref, lg_ref, lb_ref, per_row, alpha):
    gate2 = _mod_rows(mod_ref, 5, per_row)
    return _ln(alpha * x1 + (1.0 + gate2) * h) * lg_ref[...] + lb_ref[...]


def _ffn_kernel(u_ref, w1_hbm, w2_hbm, x1_ref, mod_ref, lg_ref, lb_ref, o_ref, acc, wgr, wur, w2r, sg, su, s2,
                sem, *, layer, nf, fc, per_row, alpha):
    i = pl.program_id(0)
    f_hidden = nf * fc

    def chunk_copies(c, slot):
        return (pltpu.make_async_copy(w1_hbm.at[layer, :, pl.ds(c * fc, fc)], sg.at[slot], sem.at[0, slot]),
                pltpu.make_async_copy(w1_hbm.at[layer, :, pl.ds(f_hidden + c * fc, fc)], su.at[slot],
                                      sem.at[1, slot]),
                pltpu.make_async_copy(w2_hbm.at[layer, pl.ds(c * fc, fc), :], s2.at[slot], sem.at[2, slot]))

    acc[...] = jnp.zeros_like(acc)

    def ffn(f):
        u = u_ref[...]
        g = jnp.dot(u, wgr[f], preferred_element_type=F32)
        up = jnp.dot(u, wur[f], preferred_element_type=F32)
        h = (_silu(g) * up).astype(BF16)
        acc[...] += jnp.dot(h, w2r[f], preferred_element_type=F32)

    @pl.when(i == 0)
    def _():
        for cp in chunk_copies(0, 0):
            cp.start()
        for c in range(nf):
            if c + 1 < nf:
                for cp in chunk_copies(c + 1, (c + 1) % 2):
                    cp.start()
            for cp in chunk_copies(c, c % 2):
                cp.wait()
            wgr[c] = sg[c % 2].astype(BF16)
            wur[c] = su[c % 2].astype(BF16)
            w2r[c] = s2[c % 2].astype(BF16)
            ffn(c)

    @pl.when(i > 0)
    def _():
        def chunk(f, carry):
            ffn(f)
            return carry

        lax.fori_loop(0, nf, chunk, 0)

    o_ref[...] = _post_norm2(x1_ref[...], acc[...], mod_ref, lg_ref, lb_ref, per_row, alpha)


def _dense_ffn(u2, w_ff1, w_ff2, layer, x1, mod, ln_g, ln_b, *, per_row, tm, tiles_per_seq, fc, alpha):
    n, d = x1.shape
    f_hidden = w_ff2.shape[1]
    nf = f_hidden // fc
    full = lambda a: pl.BlockSpec(a.shape, lambda i: (0,) * a.ndim)
    return pl.pallas_call(
        functools.partial(_ffn_kernel, layer=layer, nf=nf, fc=fc, per_row=per_row, alpha=alpha),
        out_shape=jax.ShapeDtypeStruct((n, d), F32),
        grid=(n // tm,),
        in_specs=[pl.BlockSpec((tm, d), lambda i: (i, 0)),
                  pl.BlockSpec(memory_space=pl.ANY), pl.BlockSpec(memory_space=pl.ANY),
                  pl.BlockSpec((tm, d), lambda i: (i, 0)),
                  _mod_spec(per_row, tm, d, tiles_per_seq), full(ln_g), full(ln_b)],
        out_specs=pl.BlockSpec((tm, d), lambda i: (i, 0)),
        scratch_shapes=[pltpu.VMEM((tm, d), F32), pltpu.VMEM((nf, d, fc), BF16),
                        pltpu.VMEM((nf, d, fc), BF16), pltpu.VMEM((nf, fc, d), BF16),
                        pltpu.VMEM((2, d, fc), F32), pltpu.VMEM((2, d, fc), F32),
                        pltpu.VMEM((2, fc, d), F32), pltpu.SemaphoreType.DMA((3, 2))],
        compiler_params=_cparams(("arbitrary",)),
        name="dense_ffn",
    )(u2, w_ff1, w_ff2, x1, mod, ln_g, ln_b)


def _moe_rows_kernel(u_ref, comb_ref, wg_ref, wu_ref, w2_ref, x1_ref, mod_ref, lg_ref, lb_ref, o_ref, acc,
                     *, alpha):
    e = pl.program_id(0)
    f = pl.program_id(1)

    @pl.when((e == 0) & (f == 0))
    def _():
        acc[...] = jnp.zeros_like(acc)

    u = u_ref[...]
    g = jnp.dot(u, wg_ref[0].astype(BF16), preferred_element_type=F32)
    up = jnp.dot(u, wu_ref[0].astype(BF16), preferred_element_type=F32)
    h = (_silu(g) * up).astype(BF16)
    y = jnp.dot(h, w2_ref[0].astype(BF16), preferred_element_type=F32)
    comb = comb_ref[...]
    lane = lax.broadcasted_iota(jnp.int32, comb.shape, 1)
    gate = jnp.sum(jnp.where(lane == e, comb, 0.0), axis=1, keepdims=True)
    acc[...] += gate * y

    @pl.when((e == pl.num_programs(0) - 1) & (f == pl.num_programs(1) - 1))
    def _():
        o_ref[...] = _post_norm2(x1_ref[...], acc[...], mod_ref, lg_ref, lb_ref, True, alpha)


def _moe_rows(u2, comb, w_e1, w_e2, layer, x1, mod, ln_g, ln_b, *, fc, alpha):
    n, d = x1.shape
    f_hidden = w_e2.shape[1]
    nf = f_hidden // fc
    ne = N_EXPERTS
    full = lambda a: pl.BlockSpec(a.shape, lambda e, f: (0,) * a.ndim)
    return pl.pallas_call(
        functools.partial(_moe_rows_kernel, alpha=alpha),
        out_shape=jax.ShapeDtypeStruct((n, d), F32),
        grid=(ne, nf),
        in_specs=[full(u2), full(comb),
                  pl.BlockSpec((1, d, fc), lambda e, f: (layer * ne + e, 0, f)),
                  pl.BlockSpec((1, d, fc), lambda e, f: (layer * ne + e, 0, nf + f)),
                  pl.BlockSpec((1, fc, d), lambda e, f: (layer * ne + e, f, 0)),
                  full(x1), full(mod), full(ln_g), full(ln_b)],
        out_specs=full(x1),
        scratch_shapes=[pltpu.VMEM((n, d), F32)],
        compiler_params=_cparams(("arbitrary", "arbitrary")),
        name="moe_rows",
    )(u2, comb, w_e1, w_e1, w_e2, x1, mod, ln_g, ln_b)


def _slot_copy(stage, xs_ref, sem, par, e, first, start, rows):
    src = stage.at[par, pl.ds(pl.multiple_of(first, SUBLANES), rows), :]
    dst = xs_ref.at[pl.ds(pl.multiple_of(start, SUBLANES), rows), :]
    return pltpu.make_async_copy(src, dst, sem.at[par, e])


def _run_chunks(first_ref, b, e):
    base = b * (N_EXPERTS + 1) + e
    return (first_ref[base + 1] - first_ref[base] + MOE_CHUNK - 1) // MOE_CHUNK


def _dispatch_kernel(start_ref, first_ref, fill_ref, u_ref, rt_ref, xs_ref, stage, sem):
    b = pl.program_id(0)
    nb = pl.num_programs(0)
    par = b % 2
    tb = u_ref.shape[0]

    @pl.when(b < 2)
    def _():
        stage[par, BLOCK_ROWS:, :] = jnp.zeros((MOE_BLOCK, stage.shape[2]), F32)

    slot = lax.broadcasted_iota(jnp.int32, (BLOCK_ROWS, tb), 0).astype(F32)
    sel = jnp.where((slot == rt_ref[0:1, :]) | (slot == rt_ref[1:2, :]), 1.0, 0.0).astype(BF16)
    stage[par, :BLOCK_ROWS, :] = jnp.dot(sel, u_ref[...], preferred_element_type=F32)
    def wait_chunks(blk, p, e):
        def one(c, carry):
            _slot_copy(stage, xs_ref, sem, p, e, 0, 0, MOE_CHUNK).wait()
            return carry

        lax.fori_loop(0, _run_chunks(first_ref, blk, e), one, 0)

    for e in range(N_EXPERTS):
        @pl.when(b > 0)
        def _():
            wait_chunks(b - 1, 1 - par, e)

        first = first_ref[b * (N_EXPERTS + 1) + e]
        start = start_ref[b * N_EXPERTS + e]

        def send(c, carry, e=e, first=first, start=start):
            _slot_copy(stage, xs_ref, sem, par, e, first + c * MOE_CHUNK, start + c * MOE_CHUNK,
                       MOE_CHUNK).start()
            return carry

        lax.fori_loop(0, _run_chunks(first_ref, b, e), send, 0)

    @pl.when(b == nb - 1)
    def _():
        for e in range(N_EXPERTS):
            wait_chunks(b, par, e)
        stage[1 - par, :MOE_BLOCK, :] = jnp.zeros((MOE_BLOCK, stage.shape[2]), F32)
        for e in range(N_EXPERTS):
            _slot_copy(stage, xs_ref, sem, 1 - par, e, 0, start_ref[nb * N_EXPERTS + e], MOE_BLOCK).start()
        for e in range(N_EXPERTS):
            _slot_copy(stage, xs_ref, sem, 1 - par, e, 0, 0, MOE_BLOCK).wait()
        for g in range(N_EXPERTS + 1):
            lo = fill_ref[g]
            hi = fill_ref[N_EXPERTS + 1 + g]
            tiles = (hi - lo) // MOE_BLOCK

            def fill(t, carry, lo=lo):
                _slot_copy(stage, xs_ref, sem, 1 - par, 0, 0, lo + t * MOE_BLOCK, MOE_BLOCK).start()
                return carry

            def drain(t, carry):
                _slot_copy(stage, xs_ref, sem, 1 - par, 0, 0, 0, MOE_BLOCK).wait()
                return carry

            lax.fori_loop(0, tiles, fill, 0)
            lax.fori_loop(0, tiles, drain, 0)

            @pl.when(lo + tiles * MOE_BLOCK < hi)
            def _():
                _slot_copy(stage, xs_ref, sem, 1 - par, 0, 0, hi - MOE_BLOCK, MOE_BLOCK).start()
                _slot_copy(stage, xs_ref, sem, 1 - par, 0, 0, 0, MOE_BLOCK).wait()


def _dispatch(start, first, fill, u2, route_t, n_slots):
    n, d = u2.shape
    tb = MOE_BLOCK
    return pl.pallas_call(
        _dispatch_kernel,
        out_shape=jax.ShapeDtypeStruct((n_slots, d), F32),
        grid_spec=pltpu.PrefetchScalarGridSpec(
            num_scalar_prefetch=3,
            grid=(n // tb,),
            in_specs=[pl.BlockSpec((tb, d), lambda b, s, f, z: (b, 0)),
                      pl.BlockSpec((SUBLANES, tb), lambda b, s, f, z: (0, b))],
            out_specs=pl.BlockSpec(memory_space=pl.ANY),
            scratch_shapes=[pltpu.VMEM((2, BLOCK_ROWS + tb, d), F32),
                            pltpu.SemaphoreType.DMA((2, N_EXPERTS))]),
        compiler_params=_cparams(("arbitrary",)),
        name="moe_dispatch",
    )(start, first, fill, u2, route_t)


def _expert_kernel(te_ref, tn_ref, tf_ref, x_ref, w1_hbm, w2_hbm, o_ref, xb, wgr, wur, w2r, sg, su, s2, sem,
                   *, layer, nf, fc):
    j = pl.program_id(0)
    n_sub = tn_ref[j]
    subs = o_ref.shape[0] // MOE_BLOCK
    f_hidden = nf * fc
    expert = layer * N_EXPERTS + te_ref[j]

    def chunk_copies(c, slot):
        return (pltpu.make_async_copy(w1_hbm.at[expert, :, pl.ds(c * fc, fc)], sg.at[slot], sem.at[0, slot]),
                pltpu.make_async_copy(w1_hbm.at[expert, :, pl.ds(f_hidden + c * fc, fc)], su.at[slot],
                                      sem.at[1, slot]),
                pltpu.make_async_copy(w2_hbm.at[expert, pl.ds(c * fc, fc), :], s2.at[slot], sem.at[2, slot]))

    o_ref[...] = jnp.zeros_like(o_ref)

    def cast(i, carry):
        rows = pl.ds(pl.multiple_of(i * MOE_BLOCK, MOE_BLOCK), MOE_BLOCK)
        xb[rows, :] = x_ref[rows, :].astype(BF16)
        return carry

    lax.fori_loop(0, n_sub, cast, 0)

    def ffn(rows, f):
        x = xb[rows, :]
        g = jnp.dot(x, wgr[f], preferred_element_type=F32)
        up = jnp.dot(x, wur[f], preferred_element_type=F32)
        h = (_silu(g) * up).astype(BF16)
        o_ref[rows, :] += jnp.dot(h, w2r[f], preferred_element_type=F32)

    def ffn_blocks(f):
        def sub(i, carry):
            ffn(pl.ds(pl.multiple_of(i * MOE_BLOCK, MOE_BLOCK), MOE_BLOCK), f)
            return carry

        lax.fori_loop(0, n_sub, sub, 0)

    @pl.when(tf_ref[j] == 1)
    def _():
        for cp in chunk_copies(0, 0):
            cp.start()
        for c in range(nf):
            if c + 1 < nf:
                for cp in chunk_copies(c + 1, (c + 1) % 2):
                    cp.start()
            for cp in chunk_copies(c, c % 2):
                cp.wait()
            wgr[c] = sg[c % 2].astype(BF16)
            wur[c] = su[c % 2].astype(BF16)
            w2r[c] = s2[c % 2].astype(BF16)
            ffn_blocks(c)

    @pl.when((tf_ref[j] == 0) & (n_sub == subs))
    def _():
        def chunk(f, carry):
            ffn(slice(None), f)
            return carry

        lax.fori_loop(0, nf, chunk, 0)

    @pl.when((tf_ref[j] == 0) & (n_sub > 0) & (n_sub < subs))
    def _():
        def chunk(f, carry):
            ffn_blocks(f)
            return carry

        lax.fori_loop(0, nf, chunk, 0)


def _experts(tile_e, tile_n, tile_first, xs, w_e1, w_e2, layer, *, fc):
    n_slots, d = xs.shape
    f_hidden = w_e2.shape[1]
    nf = f_hidden // fc
    tg = MOE_TILE
    return pl.pallas_call(
        functools.partial(_expert_kernel, layer=layer, nf=nf, fc=fc),
        out_shape=jax.ShapeDtypeStruct((n_slots, d), F32),
        grid_spec=pltpu.PrefetchScalarGridSpec(
            num_scalar_prefetch=3,
            grid=(n_slots // tg,),
            in_specs=[pl.BlockSpec((tg, d), lambda j, te, tn, tf: (j, 0)),
                      pl.BlockSpec(memory_space=pl.ANY), pl.BlockSpec(memory_space=pl.ANY)],
            out_specs=pl.BlockSpec((tg, d), lambda j, te, tn, tf: (j, 0)),
            scratch_shapes=[pltpu.VMEM((tg, d), BF16), pltpu.VMEM((nf, d, fc), BF16),
                            pltpu.VMEM((nf, d, fc), BF16), pltpu.VMEM((nf, fc, d), BF16),
                            pltpu.VMEM((2, d, fc), F32), pltpu.VMEM((2, d, fc), F32),
                            pltpu.VMEM((2, fc, d), F32), pltpu.SemaphoreType.DMA((3, 2))]),
        compiler_params=_cparams(("arbitrary",), EXPERT_VMEM_LIMIT),
        name="moe_experts",
    )(tile_e, tile_n, tile_first, xs, w_e1, w_e2)


def _fetch_copy(ys_ref, buf, sem, slot, e, row, start):
    src = ys_ref.at[pl.ds(pl.multiple_of(start, SUBLANES), MOE_CHUNK), :]
    dst = buf.at[slot, e, pl.ds(pl.multiple_of(row, MOE_CHUNK), MOE_CHUNK), :]
    return pltpu.make_async_copy(src, dst, sem.at[slot, e])


def _combine_kernel(start_ref, first_ref, rt_ref, ys_ref, x1_ref, mod_ref, lg_ref, lb_ref, o_ref, buf, yc, sem,
                    *, alpha):
    b = pl.program_id(0)
    nb = pl.num_programs(0)

    def fetch(blk, slot):
        for e in range(N_EXPERTS):
            start = start_ref[blk * N_EXPERTS + e]

            def get(c, carry, e=e, start=start):
                _fetch_copy(ys_ref, buf, sem, slot, e, c * MOE_CHUNK, start + c * MOE_CHUNK).start()
                return carry

            lax.fori_loop(0, _run_chunks(first_ref, blk, e), get, 0)

    @pl.when(b == 0)
    def _():
        fetch(0, 0)
        yc[...] = jnp.zeros_like(yc)

    @pl.when(b + 1 < nb)
    def _():
        fetch(b + 1, (b + 1) % 2)

    cur = b % 2
    tb = x1_ref.shape[0]
    for e in range(N_EXPERTS):
        def landed(c, carry, e=e):
            _fetch_copy(ys_ref, buf, sem, cur, e, 0, 0).wait()
            return carry

        lax.fori_loop(0, _run_chunks(first_ref, b, e), landed, 0)
        first = first_ref[b * (N_EXPERTS + 1) + e]
        groups = (first_ref[b * (N_EXPERTS + 1) + e + 1] - first) // BLOCK_ALIGN

        def pack(g, carry, e=e, first=first):
            src = pl.ds(pl.multiple_of(g * BLOCK_ALIGN, BLOCK_ALIGN), BLOCK_ALIGN)
            dst = pl.ds(pl.multiple_of(first + g * BLOCK_ALIGN, BLOCK_ALIGN), BLOCK_ALIGN)
            yc[dst, :] = buf[cur, e, src, :].astype(BF16)
            return carry

        lax.fori_loop(0, groups, pack, 0)

    slot = lax.broadcasted_iota(jnp.int32, (BLOCK_ROWS, tb), 0).astype(F32)
    gate_t = (jnp.where(slot == rt_ref[0:1, :], rt_ref[2:3, :], 0.0)
              + jnp.where(slot == rt_ref[1:2, :], rt_ref[3:4, :], 0.0)).astype(BF16)
    acc = lax.dot_general(gate_t, yc[...], (((0,), (0,)), ((), ())), preferred_element_type=F32)
    o_ref[...] = _post_norm2(x1_ref[...], acc, mod_ref, lg_ref, lb_ref, False, alpha)


def _combine(start, first, route_t, ys, x1, mod, ln_g, ln_b, *, tiles_per_seq, alpha):
    n, d = x1.shape
    tb = MOE_BLOCK
    full = lambda a: pl.BlockSpec(a.shape, lambda b, s, f: (0,) * a.ndim)
    return pl.pallas_call(
        functools.partial(_combine_kernel, alpha=alpha),
        out_shape=jax.ShapeDtypeStruct((n, d), F32),
        grid_spec=pltpu.PrefetchScalarGridSpec(
            num_scalar_prefetch=2,
            grid=(n // tb,),
            in_specs=[pl.BlockSpec((SUBLANES, tb), lambda b, s, f: (0, b)),
                      pl.BlockSpec(memory_space=pl.ANY),
                      pl.BlockSpec((tb, d), lambda b, s, f: (b, 0)),
                      pl.BlockSpec((1, 6, d), lambda b, s, f: (b // tiles_per_seq, 0, 0)),
                      full(ln_g), full(ln_b)],
            out_specs=pl.BlockSpec((tb, d), lambda b, s, f: (b, 0)),
            scratch_shapes=[pltpu.VMEM((2, N_EXPERTS, tb, d), F32), pltpu.VMEM((BLOCK_ROWS, d), BF16),
                            pltpu.SemaphoreType.DMA((2, N_EXPERTS))]),
        compiler_params=_cparams(("arbitrary",)),
        name="moe_combine",
    )(start, first, route_t, ys, x1, mod, ln_g, ln_b)


def _routing_tables(cum_t, n_tokens, n_slots):
    ne, tb, tg = N_EXPERTS, MOE_BLOCK, MOE_TILE
    nb = n_tokens // tb
    cum = cum_t.reshape(cum_t.shape[0], nb, LANES)[:ne, :, 0].T.astype(jnp.int32)
    total = cum[-1]
    base = jnp.concatenate([jnp.zeros((1, ne), jnp.int32), cum[:-1]], axis=0)
    region = (total + tb + tg - 1) // tg * tg
    end = jnp.cumsum(region)
    off = end - region
    start = jnp.concatenate([(off[None, :] + base).reshape(-1), off + total])
    run = (cum - base + BLOCK_ALIGN - 1) // BLOCK_ALIGN * BLOCK_ALIGN
    first = jnp.concatenate([jnp.zeros((nb, 1), jnp.int32), jnp.cumsum(run, axis=1)], axis=1).reshape(-1)
    fill = jnp.concatenate([off + total + tb, end[-1:], end, jnp.full((1,), n_slots, jnp.int32)]).astype(jnp.int32)
    tile_row = jnp.arange(n_slots // tg, dtype=jnp.int32) * tg
    tile_e = jnp.minimum(jnp.sum(tile_row[:, None] >= end[None, :], axis=1), ne - 1).astype(jnp.int32)
    left = total[tile_e] - (tile_row - off[tile_e])
    tile_n = jnp.clip((left + tb - 1) // tb, 0, tg // tb).astype(jnp.int32)
    used = tile_n > 0
    tile_first = (used & (tile_row == off[tile_e])).astype(jnp.int32)
    seen = lax.cummax(jnp.where(used, tile_e, -1), axis=0)
    tile_w = jnp.where(seen >= 0, seen, tile_e[jnp.argmax(used)]).astype(jnp.int32)
    return start, first, fill, tile_w, tile_n, tile_first


def kernel(x_prompt, x_sample, cache_k, cache_v, state_conv, c_prompt, c_sample, w_ada, b_ada, w_in,
           conv_w, g_conv_out, g_att_out, w_out, ln1_g, ln1_b, ln2_g, ln2_b, w_ff1, w_ff2, w_router,
           w_e1, w_e2):
    bsz, s_len, d = x_prompt.shape
    nb_s, t_new, _ = x_sample.shape
    depth = w_in.shape[0]
    dc = conv_w.shape[2]
    da = w_in.shape[2] // 3 - dc
    assert t_new == 1 and da == N_HEADS * HEAD_DIM and da // LANES * LANES == da
    assert s_len % (Q_BLOCK * max(dl for _, dl in DILATED_PATTERNS)) == 0
    assert cache_k.shape[2] == max(w for w, _ in DILATED_PATTERNS)
    alpha = (2 * depth) ** 0.25
    n_p = bsz * s_len
    keep = min(cache_k.shape[2], s_len)
    n_exp, f_exp = w_e2.shape[1], w_e2.shape[2]
    assert n_exp == N_EXPERTS
    n_slots = n_p * TOP_K + n_exp * ((n_p // MOE_BLOCK) * (SUBLANES - 1) + MOE_BLOCK + MOE_TILE - 1)
    n_slots = (n_slots + MOE_TILE - 1) // MOE_TILE * MOE_TILE

    slopes = jnp.exp2(-8.0 * jnp.arange(1, N_HEADS + 1, dtype=F32) / N_HEADS)
    slopes_pairs = jnp.broadcast_to(slopes.reshape(N_HEADS // 2, 2, 1), (N_HEADS // 2, 2, 2 * Q_BLOCK))
    slopes_col = slopes.reshape(N_HEADS, 1, 1)
    w_buf = cache_k.shape[2]
    cache_kt = cache_k.transpose(0, 1, 3, 4, 2).reshape(depth * nb_s, N_HEADS, HEAD_DIM, w_buf)
    cache_vt = cache_v.transpose(0, 1, 3, 4, 2).reshape(depth * nb_s, N_HEADS, HEAD_DIM, w_buf)

    rows_c = (bsz + nb_s + 7) // 8 * 8
    c_all = jnp.zeros((rows_c, d), F32).at[:bsz].set(c_prompt).at[bsz:bsz + nb_s].set(c_sample)
    ada = _adaln_all(c_all, w_ada, b_ada)

    w_e1f = w_e1.reshape((-1,) + w_e1.shape[2:])
    w_e2f = w_e2.reshape((-1,) + w_e2.shape[2:])

    xp = x_prompt.reshape(n_p, d)
    xs = x_sample.reshape(nb_s, d)
    tm_p = 512
    assert (s_len - keep) % tm_p == 0
    kt_all = jnp.zeros((depth, bsz, da, keep), F32)
    vt_all = jnp.zeros((depth, bsz, da, keep), F32)
    outs = {k: [] for k in ("cp", "ks", "vs", "cs")}
    row2 = lambda a: a.reshape(1, -1)
    for l in range(depth):
        moe = l % 2 == 1
        li = l // 2
        mod_p = ada[l, :bsz].reshape(bsz, 6, d)
        mod_s = ada[l, bsz:bsz + nb_s].reshape(nb_s, 6, d).transpose(1, 0, 2)
        w_in_b = w_in[l].astype(BF16)
        w_out_b = w_out[l].astype(BF16)
        lnp = (row2(ln1_g[l]), row2(ln1_b[l]))
        ln2 = (row2(ln2_g[l]), row2(ln2_b[l]))
        norm_w = (conv_w[l], row2(g_conv_out[l]), row2(g_att_out[l]))

        bg, z, qe, qo, k, v, kt_all, vt_all = _in_proj_prompt(xp, mod_p, w_in_b, kt_all, vt_all, l,
                                                              tm=tm_p, s_len=s_len)
        att = _prompt_attention(qe, qo, k, v, slopes_pairs, bsz, s_len)
        tm3 = tm_p
        if moe:
            wr_t = jnp.zeros((16, d), F32).at[:n_exp].set(w_router[li].T).astype(BF16)
            x1, u2, route_t, cum_t = _out_proj(
                bg, z, z, z, att, xp, mod_p, *norm_w, w_out_b, *lnp, wr_t,
                per_row=False, tm=tm3, tiles_per_seq=s_len // tm3, alpha=alpha)
            start, first, fill, tile_w, tile_n, tile_first = _routing_tables(cum_t, n_p, n_slots)
            xsorted = _dispatch(start, first, fill, u2, route_t, n_slots)
            ysorted = _experts(tile_w, tile_n, tile_first, xsorted, w_e1f, w_e2f, li, fc=512)
            xp = _combine(start, first, route_t, ysorted, x1, mod_p, *ln2,
                          tiles_per_seq=s_len // MOE_BLOCK, alpha=alpha)
        else:
            x1, u2 = _out_proj(bg, z, z, z, att, xp, mod_p, *norm_w, w_out_b, *lnp, None,
                               per_row=False, tm=tm3, tiles_per_seq=s_len // tm3, alpha=alpha)
            xp = _dense_ffn(u2, w_ff1, w_ff2, li, x1, mod_p, *ln2, per_row=False, tm=1024,
                            tiles_per_seq=s_len // 1024, fc=256, alpha=alpha)
        outs["cp"].append(z.reshape(bsz, s_len, dc)[:, s_len - (CONV_WIDTH - 1):])

        bg, z, q, k, v = _in_proj_sample(xs, mod_s, w_in_b)
        att = _sample_attention(q, k, v, cache_kt, cache_vt, l, slopes_col)
        st = state_conv[l]
        if moe:
            wr = jnp.zeros((d, LANES), F32).at[:, :n_exp].set(w_router[li]).astype(BF16)
            x1, u2, comb = _out_proj(bg, z, st[:, 1], st[:, 0], att, xs, mod_s, *norm_w, w_out_b, *lnp, wr,
                                     per_row=True, tm=nb_s, tiles_per_seq=1, alpha=alpha)
            xs = _moe_rows(u2, comb, w_e1f, w_e2f, li, x1, mod_s, *ln2, fc=512, alpha=alpha)
        else:
            x1, u2 = _out_proj(bg, z, st[:, 1], st[:, 0], att, xs, mod_s, *norm_w, w_out_b, *lnp, None,
                               per_row=True, tm=nb_s, tiles_per_seq=1, alpha=alpha)
            xs = _dense_ffn(u2, w_ff1, w_ff2, li, x1, mod_s, *ln2, per_row=True, tm=nb_s,
                            tiles_per_seq=1, fc=256, alpha=alpha)
        outs["ks"].append(k.reshape(nb_s, 1, N_HEADS, HEAD_DIM))
        outs["vs"].append(v.reshape(nb_s, 1, N_HEADS, HEAD_DIM))
        outs["cs"].append(jnp.stack([st[:, 1], z], axis=1))

    to_rows = lambda t: t.reshape(depth, bsz, N_HEADS, HEAD_DIM, keep).transpose(0, 1, 4, 2, 3)
    return (xp.reshape(bsz, s_len, d), xs.reshape(nb_s, 1, d),
            to_rows(kt_all), to_rows(vt_all), jnp.stack(outs["cp"]),
            jnp.stack(outs["ks"]), jnp.stack(outs["vs"]), jnp.stack(outs["cs"]))
```

```python
import functools

import jax
import jax.numpy as jnp
from jax import lax
from jax.experimental import pallas as pl
from jax.experimental.pallas import tpu as pltpu

F32 = jnp.float32
BF16 = jnp.bfloat16

N_HEADS = 8
HEAD_DIM = 64
CONV_WIDTH = 3
DILATED_PATTERNS = ((128, 1), (512, 4), (2048, 16))
Q_BLOCK = 128
N_EXPERTS = 8
TOP_K = 2
LN_EPS = 1e-5
LOG2E = 1.4426950408889634

LANES = 128
SUBLANES = 8
SAMPLE_HEAD_GROUP = 4
ATTN_GROUP = 8
MOE_BLOCK = 256
MOE_TILE = 1024
BLOCK_ALIGN = 16
MOE_CHUNK = 32
BLOCK_ROWS = -(-(TOP_K * MOE_BLOCK + N_EXPERTS * (BLOCK_ALIGN - 1)) // LANES) * LANES
VMEM_LIMIT = 56 * 1024 * 1024
EXPERT_VMEM_LIMIT = 61 * 1024 * 1024


def _cparams(sem, vmem=VMEM_LIMIT):
    return pltpu.CompilerParams(dimension_semantics=sem, vmem_limit_bytes=vmem)


def _ln(x):
    mu = jnp.mean(x, axis=-1, keepdims=True)
    xc = x - mu
    var = jnp.mean(xc * xc, axis=-1, keepdims=True)
    return xc * lax.rsqrt(var + LN_EPS)


def _silu(x):
    return x * jax.nn.sigmoid(x)


def _mod_rows(mod_ref, k, per_row):
    return mod_ref[k] if per_row else mod_ref[0, k:k + 1, :]


def _mod_spec(per_row, rows, d, tiles_per_seq):
    if per_row:
        return pl.BlockSpec((6, rows, d), lambda i, *_: (0, 0, 0))
    return pl.BlockSpec((1, 6, d), lambda i, *_: (i // tiles_per_seq, 0, 0))


def _ada_kernel(c_ref, w_ref, b_ref, o_ref):
    c = c_ref[...]
    s = _silu(c).astype(BF16)
    o_ref[0] = jnp.dot(s, w_ref[0].astype(BF16), preferred_element_type=F32) + b_ref[0]


def _adaln_all(c_all, w_ada, b_ada):
    depth, d, e6 = w_ada.shape
    rows = c_all.shape[0]
    tn = e6 // 4
    return pl.pallas_call(
        _ada_kernel,
        out_shape=jax.ShapeDtypeStruct((depth, rows, e6), F32),
        grid=(depth, e6 // tn),
        in_specs=[pl.BlockSpec((rows, d), lambda l, j: (0, 0)),
                  pl.BlockSpec((1, d, tn), lambda l, j: (l, 0, j)),
                  pl.BlockSpec((1, 1, tn), lambda l, j: (l, 0, j))],
        out_specs=pl.BlockSpec((1, rows, tn), lambda l, j: (l, 0, j)),
        compiler_params=_cparams(("arbitrary", "arbitrary")),
        name="adaln",
    )(c_all, w_ada, b_ada.reshape(depth, 1, e6))


def _inproj_kernel(*refs, prompt, dc, tiles_per_seq, first_kept):
    if prompt:
        x_ref, mod_ref, w_ref, _, _, bg_ref, z_ref, qe_ref, qo_ref, k_ref, v_ref, kt_ref, vt_ref = refs
    else:
        x_ref, mod_ref, w_ref, bg_ref, z_ref, q_ref, k_ref, v_ref = refs
    shift = _mod_rows(mod_ref, 0, not prompt)
    scale = _mod_rows(mod_ref, 1, not prompt)
    u = (_ln(x_ref[...]) * (1.0 + scale) + shift).astype(BF16)

    def proj(j):
        return jnp.dot(u, w_ref[:, j * dc:(j + 1) * dc], preferred_element_type=F32)

    bg_ref[...] = proj(0)
    z_ref[...] = proj(1) * proj(2)
    q = proj(3)
    k = proj(4)
    v = proj(5)
    k_ref[...] = k
    v_ref[...] = v
    if not prompt:
        q_ref[...] = q
        return
    q = q * (HEAD_DIM ** -0.5 * LOG2E)
    odd = (lax.broadcasted_iota(jnp.int32, q.shape, 1) // HEAD_DIM) % 2 == 1
    qe_ref[...] = jnp.where(odd, 0.0, q)
    qo_ref[...] = jnp.where(odd, q, 0.0)

    @pl.when(pl.program_id(0) % tiles_per_seq >= first_kept)
    def _():
        kt_ref[...] = k.T
        vt_ref[...] = v.T


def _in_proj_prompt(x, mod, w_in_b, kt_buf, vt_buf, layer, *, tm, s_len):
    n, d = x.shape
    dc = w_in_b.shape[2] // 6
    keep = kt_buf.shape[3]
    tps = s_len // tm
    first_kept = (s_len - keep) // tm
    row = lambda: pl.BlockSpec((tm, dc), lambda i: (i, 0))
    kept = lambda: pl.BlockSpec((None, None, dc, tm),
                                lambda i: (layer, i // tps, 0, jnp.maximum(i % tps - first_kept, 0)))
    rows = jax.ShapeDtypeStruct((n, dc), F32)
    outs = pl.pallas_call(
        functools.partial(_inproj_kernel, prompt=True, dc=dc, tiles_per_seq=tps, first_kept=first_kept),
        out_shape=[rows] * 6 + [jax.ShapeDtypeStruct(kt_buf.shape, F32)] * 2,
        grid=(n // tm,),
        in_specs=[pl.BlockSpec((tm, d), lambda i: (i, 0)),
                  _mod_spec(False, tm, d, tps),
                  pl.BlockSpec((None,) + w_in_b.shape[1:], lambda i: (layer, 0, 0), pipeline_mode=pl.Buffered(1)),
                  pl.BlockSpec(memory_space=pl.ANY), pl.BlockSpec(memory_space=pl.ANY)],
        out_specs=[row() for _ in range(6)] + [kept(), kept()],
        input_output_aliases={3: 6, 4: 7},
        compiler_params=_cparams(("arbitrary",)),
        name="in_proj",
    )(x, mod, w_in_b, kt_buf, vt_buf)
    return outs


def _in_proj_sample(x, mod, w_in_b, layer):
    n, d = x.shape
    dc = w_in_b.shape[2] // 6
    row = lambda: pl.BlockSpec((n, dc), lambda i: (0, 0))
    return pl.pallas_call(
        functools.partial(_inproj_kernel, prompt=False, dc=dc, tiles_per_seq=1, first_kept=0),
        out_shape=[jax.ShapeDtypeStruct((n, dc), F32)] * 5,
        grid=(1,),
        in_specs=[pl.BlockSpec((n, d), lambda i: (0, 0)),
                  _mod_spec(True, n, d, 1),
                  pl.BlockSpec((None,) + w_in_b.shape[1:], lambda i: (layer, 0, 0), pipeline_mode=pl.Buffered(1))],
        out_specs=[row() for _ in range(5)],
        compiler_params=_cparams(("arbitrary",)),
        name="in_proj_rows",
    )(x, mod, w_in_b)


def _attn_kernel(qe_ref, qo_ref, k_ref, v_ref, sl_ref, o_ref, m_scr, l_scr, bias_scr):
    s_len = k_ref.shape[0]
    qb = Q_BLOCK
    half = lax.broadcasted_iota(jnp.int32, (qb, LANES), 1) >= HEAD_DIM
    ri = lax.broadcasted_iota(jnp.int32, (2 * qb, 2 * qb), 0)
    ji = lax.broadcasted_iota(jnp.int32, (2 * qb, 2 * qb), 1)
    step = qb + (ri % qb) - ji
    band = (step >= 0) & (step <= qb)
    slope = jnp.where(ri < qb, sl_ref[0, 0:1, :], sl_ref[0, 1:2, :]) * LOG2E
    for pi, (_, d) in enumerate(DILATED_PATTERNS):
        bias = jnp.where(band, -(slope * (d * step).astype(F32)), -jnp.inf)
        bias_scr[2 * pi] = bias
        bias_scr[2 * pi + 1] = jnp.where(ji < qb, -jnp.inf, bias)
    ones = jnp.ones((2 * qb, LANES), BF16)

    def block(idx, pi, d, first, last):
        r = idx % d
        n = idx // d
        base = n * (qb * d) + r
        prev = jnp.maximum(n - 1, 0) * (qb * d) + r
        rows = pl.ds(base, qb, stride=d)
        prow = pl.ds(prev, qb, stride=d)
        q2 = jnp.concatenate([qe_ref[rows, :], qo_ref[rows, :]], axis=0).astype(BF16)
        k2 = jnp.concatenate([k_ref[prow, :], k_ref[rows, :]], axis=0).astype(BF16)
        v2 = jnp.concatenate([jnp.concatenate([v_ref[prow, :], v_ref[rows, :]], axis=0).astype(BF16), ones],
                             axis=1)
        s = lax.dot_general(q2, k2, (((1,), (1,)), ((), ())), preferred_element_type=F32)
        s = s + bias_scr[2 * pi + jnp.where(n == 0, 1, 0)]
        s0, s1 = s[:, :qb], s[:, qb:]
        mb = jnp.max(jnp.maximum(s0, s1), axis=1, keepdims=True)
        if first:
            m_new = jnp.broadcast_to(mb, (2 * qb, LANES))
        else:
            m_old = jnp.concatenate([m_scr[0, rows, :], m_scr[1, rows, :]], axis=0)
            m_new = jnp.maximum(m_old, mb)
        p = jnp.concatenate([jnp.exp2(s0 - m_new), jnp.exp2(s1 - m_new)], axis=1).astype(BF16)
        pvs = jnp.dot(p, v2, preferred_element_type=F32)
        pv, rs = pvs[:, :LANES], pvs[:, LANES:]
        if first:
            l_new = rs
            acc = jnp.where(half, pv[qb:], pv[:qb])
        else:
            alpha = jnp.exp2(m_old - m_new)
            l_old = jnp.concatenate([l_scr[0, rows, :], l_scr[1, rows, :]], axis=0)
            l_new = alpha * l_old + rs
            acc_old = o_ref[rows, :]
            acc = jnp.where(half, alpha[qb:] * acc_old + pv[qb:], alpha[:qb] * acc_old + pv[:qb])
        if last:
            acc = acc / jnp.where(half, l_new[qb:], l_new[:qb])
        return rows, m_new, l_new, acc

    order = sorted(range(len(DILATED_PATTERNS)), key=lambda pi: -DILATED_PATTERNS[pi][1])
    for pos, pi in enumerate(order):
        d = DILATED_PATTERNS[pi][1]
        first, last = pos == 0, pos == len(order) - 1

        def body(it, carry, pi=pi, d=d, first=first, last=last):
            done = [block(it * ATTN_GROUP + g, pi, d, first, last) for g in range(ATTN_GROUP)]
            for rows, m_new, l_new, acc in done:
                if not last:
                    m_scr[0, rows, :] = m_new[:qb]
                    m_scr[1, rows, :] = m_new[qb:]
                    l_scr[0, rows, :] = l_new[:qb]
                    l_scr[1, rows, :] = l_new[qb:]
                o_ref[rows, :] = acc
            return carry

        lax.fori_loop(0, s_len // (qb * ATTN_GROUP), body, 0)


def _prompt_attention(qe, qo, k, v, slopes_pairs, bsz, s_len):
    da = k.shape[1]
    pairs = da // LANES
    spec = lambda: pl.BlockSpec((None, s_len, LANES), lambda b, h: (b, 0, h))
    rs = lambda a: a.reshape(bsz, s_len, da)
    out = pl.pallas_call(
        _attn_kernel,
        out_shape=jax.ShapeDtypeStruct((bsz, s_len, da), F32),
        grid=(bsz, pairs),
        in_specs=[spec(), spec(), spec(), spec(),
                  pl.BlockSpec((1, 2, 2 * Q_BLOCK), lambda b, h: (h, 0, 0))],
        out_specs=spec(),
        scratch_shapes=[pltpu.VMEM((2, s_len, LANES), F32), pltpu.VMEM((2, s_len, LANES), F32),
                        pltpu.VMEM((2 * len(DILATED_PATTERNS), 2 * Q_BLOCK, 2 * Q_BLOCK), F32)],
        compiler_params=_cparams(("arbitrary", "arbitrary")),
        name="prompt_attention",
    )(rs(qe), rs(qo), rs(k), rs(v), slopes_pairs)
    return out.reshape(bsz * s_len, da)


def _sattn_kernel(q_ref, kn_ref, vn_ref, kt_ref, vt_ref, sl_ref, o_ref):
    bb, nh, hd, w = kt_ref.shape
    da = q_ref.shape[1]
    step = pl.program_id(0)
    dist = w - lax.broadcasted_iota(jnp.int32, (1, w), 1)
    count = jnp.zeros((1, w), F32)
    for window, d in DILATED_PATTERNS:
        count = count + jnp.where((dist % d == 0) & (dist <= window), 1.0, 0.0)
    distf = dist.astype(F32)
    n_pat = float(len(DILATED_PATTERNS))
    scale = HEAD_DIM ** -0.5
    feat = lax.broadcasted_iota(jnp.int32, (hd, da), 0)
    lane = lax.broadcasted_iota(jnp.int32, (hd, da), 1)

    def head(i):
        b = i // nh
        h = i % nh
        row = pl.ds(step * bb + b, 1)
        mine = lane == feat + h * hd

        def column(ref):
            return jnp.sum(jnp.where(mine, ref[row, :], 0.0), axis=1, keepdims=True)

        q = column(q_ref) * scale
        s = jnp.sum(kt_ref[b, h] * q, axis=0, keepdims=True)
        s = jnp.where(count > 0.0, s - sl_ref[h] * distf, -jnp.inf)
        s_self = jnp.sum(q * column(kn_ref), axis=0, keepdims=True)
        m = jnp.maximum(jnp.max(s, axis=1, keepdims=True), s_self)
        p = jnp.exp(s - m) * count
        p_self = n_pat * jnp.exp(s_self - m)
        den = jnp.sum(p, axis=1, keepdims=True) + p_self
        num = jnp.sum(vt_ref[b, h] * p, axis=1, keepdims=True) + p_self * column(vn_ref)
        return jnp.sum(jnp.where(mine, num / den, 0.0), axis=0, keepdims=True)

    def body(it, carry):
        first = it * SAMPLE_HEAD_GROUP
        out = head(first)
        for g in range(1, SAMPLE_HEAD_GROUP):
            out = out + head(first + g)
        row = pl.ds(step * bb + first // nh, 1)

        @pl.when(first % nh == 0)
        def _():
            o_ref[row, :] = out

        @pl.when(first % nh != 0)
        def _():
            o_ref[row, :] += out

        return carry

    lax.fori_loop(0, bb * nh // SAMPLE_HEAD_GROUP, body, 0)


def _sample_attention(q, k, v, cache_kt, cache_vt, layer, slopes_col, *, bb=2):
    nb, da = q.shape
    _, nh, hd, w_buf = cache_kt.shape
    assert nh % SAMPLE_HEAD_GROUP == 0
    rows = lambda: pl.BlockSpec((nb, da), lambda i: (0, 0))
    steps = nb // bb
    off = layer * steps
    win = lambda: pl.BlockSpec((bb, nh, hd, w_buf), lambda i: (off + i, 0, 0, 0))
    return pl.pallas_call(
        _sattn_kernel,
        out_shape=jax.ShapeDtypeStruct((nb, da), F32),
        grid=(steps,),
        in_specs=[rows(), rows(), rows(), win(), win(), pl.BlockSpec(slopes_col.shape, lambda i: (0, 0, 0))],
        out_specs=rows(),
        compiler_params=_cparams(("arbitrary",)),
        name="sample_attention",
    )(q, k, v, cache_kt, cache_vt, slopes_col)


def _outproj_kernel(*refs, per_row, moe, router_rows, tiles_per_seq, alpha):
    it = iter(refs)
    bg_ref, z_ref, za_ref, zb_ref, att_ref, x_ref, mod_ref = (next(it) for _ in range(7))
    cw_ref, gc_ref, ga_ref, w_ref, lg_ref, lb_ref = (next(it) for _ in range(6))
    wr_ref = next(it) if moe else None
    x1_ref, u2_ref = next(it), next(it)
    if moe and router_rows:
        comb_ref = next(it)
    elif moe:
        route_ref, cum_ref, carry = (next(it) for _ in range(3))

    i = pl.program_id(0)
    z = z_ref[...]
    tm, dc = z.shape
    if per_row:
        z1, z2 = za_ref[...], zb_ref[...]
    else:
        keep = jnp.where(i % tiles_per_seq == 0, 0.0, 1.0)
        h1 = za_ref[7:8, :] * keep
        h2 = za_ref[6:7, :] * keep
        rid = lax.broadcasted_iota(jnp.int32, (tm, dc), 0)
        z1 = jnp.where(rid == 0, h1, pltpu.roll(z, 1, axis=0))
        z2 = jnp.where(rid == 0, h2, jnp.where(rid == 1, h1, pltpu.roll(z, 2, axis=0)))
    conv = cw_ref[0:1, :] * z2 + cw_ref[1:2, :] * z1 + cw_ref[2:3, :] * z
    yc = bg_ref[...] * conv
    yc = yc * lax.rsqrt(jnp.mean(yc * yc, axis=-1, keepdims=True) + LN_EPS) * gc_ref[...]
    ya = att_ref[...]
    ya = ya * lax.rsqrt(jnp.mean(ya * ya, axis=-1, keepdims=True) + LN_EPS) * ga_ref[...]
    h = (jnp.dot(yc.astype(BF16), w_ref[:dc, :], preferred_element_type=F32)
         + jnp.dot(ya.astype(BF16), w_ref[dc:, :], preferred_element_type=F32))
    gate1 = _mod_rows(mod_ref, 2, per_row)
    x1 = _ln(alpha * x_ref[...] + (1.0 + gate1) * h) * lg_ref[...] + lb_ref[...]
    x1_ref[...] = x1
    u2 = (_ln(x1) * (1.0 + _mod_rows(mod_ref, 4, per_row)) + _mod_rows(mod_ref, 3, per_row)).astype(BF16)
    u2_ref[...] = u2
    if not moe:
        return

    if router_rows:
        lg = jnp.dot(u2, wr_ref[...], preferred_element_type=F32)
        lane = lax.broadcasted_iota(jnp.int32, lg.shape, 1)
        lg = jnp.where(lane < N_EXPERTS, lg, -jnp.inf)
        m1 = jnp.max(lg, axis=1, keepdims=True)
        i1 = jnp.min(jnp.where(lg == m1, lane, LANES), axis=1, keepdims=True)
        lg2 = jnp.where(lane == i1, -jnp.inf, lg)
        m2 = jnp.max(lg2, axis=1, keepdims=True)
        i2 = jnp.min(jnp.where(lg2 == m2, lane, LANES), axis=1, keepdims=True)
        ex = jnp.exp(m2 - m1)
        g1 = 1.0 / (1.0 + ex)
        comb_ref[...] = jnp.where(lane == i1, g1, 0.0) + jnp.where(lane == i2, ex * g1, 0.0)
        return

    lg = lax.dot_general(wr_ref[...], u2, (((1,), (1,)), ((), ())), preferred_element_type=F32)
    row = lax.broadcasted_iota(jnp.int32, lg.shape, 0)
    lg = jnp.where(row < N_EXPERTS, lg, -jnp.inf)
    m1 = jnp.max(lg, axis=0, keepdims=True)
    i1 = jnp.min(jnp.where(lg == m1, row, lg.shape[0]), axis=0, keepdims=True)
    lg2 = jnp.where(row == i1, -jnp.inf, lg)
    m2 = jnp.max(lg2, axis=0, keepdims=True)
    i2 = jnp.min(jnp.where(lg2 == m2, row, lg.shape[0]), axis=0, keepdims=True)
    ex = jnp.exp(m2 - m1)
    g1 = 1.0 / (1.0 + ex)
    sel1, sel2 = row == i1, row == i2
    onehot = jnp.where(sel1 | sel2, 1.0, 0.0)
    ta = lax.broadcasted_iota(jnp.int32, (tm, tm), 0)
    tb = lax.broadcasted_iota(jnp.int32, (tm, tm), 1)
    upper = jnp.where((ta < tb) & (ta // MOE_BLOCK == tb // MOE_BLOCK), 1.0, 0.0).astype(BF16)
    rank = jnp.dot(onehot.astype(BF16), upper, preferred_element_type=F32)
    ea = lax.broadcasted_iota(jnp.int32, (lg.shape[0], lg.shape[0]), 0)
    eb = lax.broadcasted_iota(jnp.int32, (lg.shape[0], lg.shape[0]), 1)
    before = jnp.where(eb < ea, 1.0, 0.0).astype(BF16)

    @pl.when(i == 0)
    def _():
        carry[...] = jnp.zeros_like(carry)

    firsts = []
    for blk in range(tm // MOE_BLOCK):
        count = jnp.sum(onehot[:, blk * MOE_BLOCK:(blk + 1) * MOE_BLOCK], axis=1, keepdims=True)
        run = jnp.floor((count + (BLOCK_ALIGN - 1.0)) * (1.0 / BLOCK_ALIGN)) * BLOCK_ALIGN
        first = jnp.dot(before, jnp.broadcast_to(run, (lg.shape[0], LANES)).astype(BF16),
                        preferred_element_type=F32)[:, 0:1]
        firsts.append(jnp.broadcast_to(first, (lg.shape[0], MOE_BLOCK)))
        carry[...] = carry[...] + jnp.floor((count + (SUBLANES - 1.0)) * (1.0 / SUBLANES)) * SUBLANES
        cum_ref[:, blk * LANES:(blk + 1) * LANES] = carry[...]
    pos = jnp.concatenate(firsts, axis=1) + rank
    pos1 = jnp.sum(jnp.where(sel1, pos, 0.0), axis=0, keepdims=True)
    pos2 = jnp.sum(jnp.where(sel2, pos, 0.0), axis=0, keepdims=True)
    r8 = lax.broadcasted_iota(jnp.int32, route_ref.shape, 0)
    route_ref[...] = jnp.where(r8 == 0, pos1, jnp.where(r8 == 1, pos2, jnp.where(
        r8 == 2, g1, jnp.where(r8 == 3, ex * g1, 0.0))))


def _out_proj(bg, z, za, zb, att, x, mod, conv_w, g_conv, g_att, w_out_b, ln_g, ln_b, w_router_b,
              *, layer, per_row, tm, tiles_per_seq, alpha):
    n, d = x.shape
    dc = z.shape[1]
    moe = w_router_b is not None
    router_rows = moe and per_row
    full = lambda a: pl.BlockSpec(a.shape, lambda i: (0,) * a.ndim)
    rowc = lambda: pl.BlockSpec((tm, dc), lambda i: (i, 0))
    rowd = lambda: pl.BlockSpec((tm, d), lambda i: (i, 0))
    if per_row:
        za_spec, zb_spec = rowc(), rowc()
    else:
        za_spec = pl.BlockSpec((8, dc), lambda i: (jnp.maximum(i * (tm // 8) - 1, 0), 0))
        zb_spec = pl.BlockSpec((8, dc), lambda i: (0, 0))
    args = [bg, z, za, zb, att, x, mod, conv_w, g_conv, g_att, w_out_b, ln_g, ln_b]
    in_specs = [rowc(), rowc(), za_spec, zb_spec, rowc(), rowd(), _mod_spec(per_row, tm, d, tiles_per_seq),
                full(conv_w), full(g_conv), full(g_att),
                pl.BlockSpec((None,) + w_out_b.shape[1:], lambda i: (layer, 0, 0), pipeline_mode=pl.Buffered(1)),
                full(ln_g), full(ln_b)]
    out_shape = [jax.ShapeDtypeStruct((n, d), F32), jax.ShapeDtypeStruct((n, d), BF16)]
    out_specs = [rowd(), rowd()]
    scratch = []
    if moe:
        args.append(w_router_b)
        in_specs.append(full(w_router_b))
        if router_rows:
            out_shape.append(jax.ShapeDtypeStruct((n, LANES), F32))
            out_specs.append(pl.BlockSpec((tm, LANES), lambda i: (i, 0)))
        else:
            er = w_router_b.shape[0]
            out_shape += [jax.ShapeDtypeStruct((SUBLANES, n), F32),
                          jax.ShapeDtypeStruct((er, (n // MOE_BLOCK) * LANES), F32)]
            out_specs += [pl.BlockSpec((SUBLANES, tm), lambda i: (0, i)),
                          pl.BlockSpec((er, (tm // MOE_BLOCK) * LANES), lambda i: (0, i))]
            scratch.append(pltpu.VMEM((er, LANES), F32))
    return pl.pallas_call(
        functools.partial(_outproj_kernel, per_row=per_row, moe=moe, router_rows=router_rows,
                          tiles_per_seq=tiles_per_seq, alpha=alpha),
        out_shape=out_shape,
        grid=(n // tm,),
        in_specs=in_specs,
        out_specs=out_specs,
        scratch_shapes=scratch,
        compiler_params=_cparams(("arbitrary",)),
        name="out_proj",
    )(*args)


def _post_norm2(x1, h, mod_ref, lg_ref, lb_ref, per_row, alpha):
    gate2 = _mod_rows(mod_ref, 5, per_row)
    return _ln(alpha * x1 + (1.0 + gate2) * h) * lg_ref[...] + lb_ref[...]


def _ffn_kernel(u_ref, w1_hbm, w2_hbm, x1_ref, mod_ref, lg_ref, lb_ref, o_ref, acc, wgr, wur, w2r, sg, su, s2,
                sem, *, layer, nf, fc, per_row, alpha):
    i = pl.program_id(0)
    f_hidden = nf * fc

    def chunk_copies(c, slot):
        return (pltpu.make_async_copy(w1_hbm.at[layer, :, pl.ds(c * fc, fc)], sg.at[slot], sem.at[0, slot]),
                pltpu.make_async_copy(w1_hbm.at[layer, :, pl.ds(f_hidden + c * fc, fc)], su.at[slot],
                                      sem.at[1, slot]),
                pltpu.make_async_copy(w2_hbm.at[layer, pl.ds(c * fc, fc), :], s2.at[slot], sem.at[2, slot]))

    acc[...] = jnp.zeros_like(acc)

    def ffn(f):
        u = u_ref[...]
        g = jnp.dot(u, wgr[f], preferred_element_type=F32)
        up = jnp.dot(u, wur[f], preferred_element_type=F32)
        h = (_silu(g) * up).astype(BF16)
        acc[...] += jnp.dot(h, w2r[f], preferred_element_type=F32)

    @pl.when(i == 0)
    def _():
        for cp in chunk_copies(0, 0):
            cp.start()
        for c in range(nf):
            if c + 1 < nf:
                for cp in chunk_copies(c + 1, (c + 1) % 2):
                    cp.start()
            for cp in chunk_copies(c, c % 2):
                cp.wait()
            wgr[c] = sg[c % 2].astype(BF16)
            wur[c] = su[c % 2].astype(BF16)
            w2r[c] = s2[c % 2].astype(BF16)
            ffn(c)

    @pl.when(i > 0)
    def _():
        def chunk(f, carry):
            ffn(f)
            return carry

        lax.fori_loop(0, nf, chunk, 0)

    o_ref[...] = _post_norm2(x1_ref[...], acc[...], mod_ref, lg_ref, lb_ref, per_row, alpha)


def _dense_ffn(u2, w_ff1, w_ff2, layer, x1, mod, ln_g, ln_b, *, per_row, tm, tiles_per_seq, fc, alpha):
    n, d = x1.shape
    f_hidden = w_ff2.shape[1]
    nf = f_hidden // fc
    full = lambda a: pl.BlockSpec(a.shape, lambda i: (0,) * a.ndim)
    return pl.pallas_call(
        functools.partial(_ffn_kernel, layer=layer, nf=nf, fc=fc, per_row=per_row, alpha=alpha),
        out_shape=jax.ShapeDtypeStruct((n, d), F32),
        grid=(n // tm,),
        in_specs=[pl.BlockSpec((tm, d), lambda i: (i, 0)),
                  pl.BlockSpec(memory_space=pl.ANY), pl.BlockSpec(memory_space=pl.ANY),
                  pl.BlockSpec((tm, d), lambda i: (i, 0)),
                  _mod_spec(per_row, tm, d, tiles_per_seq), full(ln_g), full(ln_b)],
        out_specs=pl.BlockSpec((tm, d), lambda i: (i, 0)),
        scratch_shapes=[pltpu.VMEM((tm, d), F32), pltpu.VMEM((nf, d, fc), BF16),
                        pltpu.VMEM((nf, d, fc), BF16), pltpu.VMEM((nf, fc, d), BF16),
                        pltpu.VMEM((2, d, fc), F32), pltpu.VMEM((2, d, fc), F32),
                        pltpu.VMEM((2, fc, d), F32), pltpu.SemaphoreType.DMA((3, 2))],
        compiler_params=_cparams(("arbitrary",)),
        name="dense_ffn",
    )(u2, w_ff1, w_ff2, x1, mod, ln_g, ln_b)


def _post_norm_rows_kernel(x1_ref, h_ref, mod_ref, lg_ref, lb_ref, o_ref, *, alpha):
    o_ref[...] = _post_norm2(x1_ref[...], h_ref[...], mod_ref, lg_ref, lb_ref, True, alpha)


def _post_norm_rows(x1, h, mod, ln_g, ln_b, *, alpha):
    full = lambda a: pl.BlockSpec(a.shape, lambda i: (0,) * a.ndim)
    return pl.pallas_call(
        functools.partial(_post_norm_rows_kernel, alpha=alpha),
        out_shape=jax.ShapeDtypeStruct(x1.shape, F32),
        grid=(1,),
        in_specs=[full(x1), full(h), full(mod), full(ln_g), full(ln_b)],
        out_specs=full(x1),
        compiler_params=_cparams(("arbitrary",)),
        name="post_norm_rows",
    )(x1, h, mod, ln_g, ln_b)


def _slot_copy(stage, xs_ref, sem, par, e, first, start, rows):
    src = stage.at[par, pl.ds(pl.multiple_of(first, SUBLANES), rows), :]
    dst = xs_ref.at[pl.ds(pl.multiple_of(start, SUBLANES), rows), :]
    return pltpu.make_async_copy(src, dst, sem.at[par, e])


def _run_chunks(first_ref, b, e):
    base = b * (N_EXPERTS + 1) + e
    return (first_ref[base + 1] - first_ref[base] + MOE_CHUNK - 1) // MOE_CHUNK


def _dispatch_kernel(start_ref, first_ref, fill_ref, u_ref, rt_ref, xs_ref, stage, sem):
    b = pl.program_id(0)
    nb = pl.num_programs(0)
    par = b % 2
    tb = u_ref.shape[0]

    @pl.when(b < 2)
    def _():
        stage[par, BLOCK_ROWS:, :] = jnp.zeros((MOE_BLOCK, stage.shape[2]), F32)

    slot = lax.broadcasted_iota(jnp.int32, (BLOCK_ROWS, tb), 0).astype(F32)
    sel = jnp.where((slot == rt_ref[0:1, :]) | (slot == rt_ref[1:2, :]), 1.0, 0.0).astype(BF16)
    stage[par, :BLOCK_ROWS, :] = jnp.dot(sel, u_ref[...], preferred_element_type=F32)
    def wait_chunks(blk, p, e):
        def one(c, carry):
            _slot_copy(stage, xs_ref, sem, p, e, 0, 0, MOE_CHUNK).wait()
            return carry

        lax.fori_loop(0, _run_chunks(first_ref, blk, e), one, 0)

    for e in range(N_EXPERTS):
        @pl.when(b > 0)
        def _():
            wait_chunks(b - 1, 1 - par, e)

        first = first_ref[b * (N_EXPERTS + 1) + e]
        start = start_ref[b * N_EXPERTS + e]

        def send(c, carry, e=e, first=first, start=start):
            _slot_copy(stage, xs_ref, sem, par, e, first + c * MOE_CHUNK, start + c * MOE_CHUNK,
                       MOE_CHUNK).start()
            return carry

        lax.fori_loop(0, _run_chunks(first_ref, b, e), send, 0)

    @pl.when(b == nb - 1)
    def _():
        for e in range(N_EXPERTS):
            wait_chunks(b, par, e)
        stage[1 - par, :MOE_BLOCK, :] = jnp.zeros((MOE_BLOCK, stage.shape[2]), F32)
        for e in range(N_EXPERTS):
            _slot_copy(stage, xs_ref, sem, 1 - par, e, 0, start_ref[nb * N_EXPERTS + e], MOE_BLOCK).start()
        for e in range(N_EXPERTS):
            _slot_copy(stage, xs_ref, sem, 1 - par, e, 0, 0, MOE_BLOCK).wait()
        for g in range(N_EXPERTS + 1):
            lo = fill_ref[g]
            hi = fill_ref[N_EXPERTS + 1 + g]
            tiles = (hi - lo) // MOE_BLOCK

            def fill(t, carry, lo=lo):
                _slot_copy(stage, xs_ref, sem, 1 - par, 0, 0, lo + t * MOE_BLOCK, MOE_BLOCK).start()
                return carry

            def drain(t, carry):
                _slot_copy(stage, xs_ref, sem, 1 - par, 0, 0, 0, MOE_BLOCK).wait()
                return carry

            lax.fori_loop(0, tiles, fill, 0)
            lax.fori_loop(0, tiles, drain, 0)

            @pl.when(lo + tiles * MOE_BLOCK < hi)
            def _():
                _slot_copy(stage, xs_ref, sem, 1 - par, 0, 0, hi - MOE_BLOCK, MOE_BLOCK).start()
                _slot_copy(stage, xs_ref, sem, 1 - par, 0, 0, 0, MOE_BLOCK).wait()


def _dispatch(start, first, fill, u2, route_t, n_slots):
    n, d = u2.shape
    tb = MOE_BLOCK
    return pl.pallas_call(
        _dispatch_kernel,
        out_shape=jax.ShapeDtypeStruct((n_slots, d), F32),
        grid_spec=pltpu.PrefetchScalarGridSpec(
            num_scalar_prefetch=3,
            grid=(n // tb,),
            in_specs=[pl.BlockSpec((tb, d), lambda b, s, f, z: (b, 0)),
                      pl.BlockSpec((SUBLANES, tb), lambda b, s, f, z: (0, b))],
            out_specs=pl.BlockSpec(memory_space=pl.ANY),
            scratch_shapes=[pltpu.VMEM((2, BLOCK_ROWS + tb, d), F32),
                            pltpu.SemaphoreType.DMA((2, N_EXPERTS))]),
        compiler_params=_cparams(("arbitrary",)),
        name="moe_dispatch",
    )(start, first, fill, u2, route_t)


def _expert_kernel(te_ref, tn_ref, tf_ref, x_ref, us_ref, comb_ref, w1_hbm, w2_hbm, o_ref, os_ref,
                   xb, wgr, wur, w2r, sg, su, s2, sem, *, layer, nf, fc):
    j = pl.program_id(0)
    n_sub = tn_ref[j]
    subs = o_ref.shape[0] // MOE_BLOCK
    f_hidden = nf * fc
    expert = layer * N_EXPERTS + te_ref[j]

    def chunk_copies(c, slot):
        col = pl.multiple_of(c * fc, LANES)
        return (pltpu.make_async_copy(w1_hbm.at[expert, :, pl.ds(col, fc)], sg.at[slot], sem.at[0, slot]),
                pltpu.make_async_copy(w1_hbm.at[expert, :, pl.ds(pl.multiple_of(f_hidden + col, LANES), fc)],
                                      su.at[slot], sem.at[1, slot]),
                pltpu.make_async_copy(w2_hbm.at[expert, pl.ds(col, fc), :], s2.at[slot], sem.at[2, slot]))

    @pl.when(j == 0)
    def _():
        os_ref[...] = jnp.zeros_like(os_ref)

    o_ref[...] = jnp.zeros_like(o_ref)

    def cast(i, carry):
        rows = pl.ds(pl.multiple_of(i * MOE_BLOCK, MOE_BLOCK), MOE_BLOCK)
        xb[rows, :] = x_ref[rows, :].astype(BF16)
        return carry

    lax.fori_loop(0, n_sub, cast, 0)

    def swiglu(x, f):
        g = jnp.dot(x, wgr[f], preferred_element_type=F32)
        up = jnp.dot(x, wur[f], preferred_element_type=F32)
        h = (_silu(g) * up).astype(BF16)
        return jnp.dot(h, w2r[f], preferred_element_type=F32)

    def ffn(rows, f):
        o_ref[rows, :] += swiglu(xb[rows, :], f)

    def ffn_blocks(f, lo):
        def sub(i, carry):
            ffn(pl.ds(pl.multiple_of(i * MOE_BLOCK, MOE_BLOCK), MOE_BLOCK), f)
            return carry

        lax.fori_loop(lo, n_sub, sub, 0)

    @pl.when(tf_ref[j] == 1)
    def _():
        comb = comb_ref[...]
        lane = lax.broadcasted_iota(jnp.int32, comb.shape, 1)
        gate = jnp.sum(jnp.where(lane == te_ref[j], comb, 0.0), axis=1, keepdims=True)
        for cp in chunk_copies(0, 0):
            cp.start()

        def stream(c, carry):
            slot = c % 2

            @pl.when(c + 1 < nf)
            def _():
                for cp in chunk_copies(c + 1, 1 - slot):
                    cp.start()

            for cp in chunk_copies(c, slot):
                cp.wait()
            wgr[c] = sg[slot].astype(BF16)
            wur[c] = su[slot].astype(BF16)
            w2r[c] = s2[slot].astype(BF16)

            @pl.when(n_sub == subs)
            def _():
                y = swiglu(jnp.concatenate([xb[...], us_ref[...]], axis=0), c)
                o_ref[...] += y[:o_ref.shape[0]]
                os_ref[...] += gate * y[o_ref.shape[0]:]

            @pl.when((n_sub > 0) & (n_sub < subs))
            def _():
                y = swiglu(jnp.concatenate([xb[:MOE_BLOCK, :], us_ref[...]], axis=0), c)
                o_ref[:MOE_BLOCK, :] += y[:MOE_BLOCK]
                os_ref[...] += gate * y[MOE_BLOCK:]
                ffn_blocks(c, 1)

            @pl.when(n_sub == 0)
            def _():
                os_ref[...] += gate * swiglu(us_ref[...], c)

            return carry

        lax.fori_loop(0, nf, stream, 0)

    @pl.when((tf_ref[j] == 0) & (n_sub == subs))
    def _():
        def chunk(f, carry):
            ffn(slice(None), f)
            return carry

        lax.fori_loop(0, nf, chunk, 0)

    @pl.when((tf_ref[j] == 0) & (n_sub > 0) & (n_sub < subs))
    def _():
        def chunk(f, carry):
            ffn_blocks(f, 0)
            return carry

        lax.fori_loop(0, nf, chunk, 0)


def _experts(tile_e, tile_n, tile_first, xs, u2_rows, comb_rows, w_e1, w_e2, layer, *, fc):
    n_slots, d = xs.shape
    f_hidden = w_e2.shape[1]
    nf = f_hidden // fc
    tg = MOE_TILE
    full = lambda a: pl.BlockSpec(a.shape, lambda j, te, tn, tf: (0,) * a.ndim)
    return pl.pallas_call(
        functools.partial(_expert_kernel, layer=layer, nf=nf, fc=fc),
        out_shape=[jax.ShapeDtypeStruct((n_slots, d), F32), jax.ShapeDtypeStruct(u2_rows.shape, F32)],
        grid_spec=pltpu.PrefetchScalarGridSpec(
            num_scalar_prefetch=3,
            grid=(n_slots // tg,),
            in_specs=[pl.BlockSpec((tg, d), lambda j, te, tn, tf: (j, 0)),
                      full(u2_rows), full(comb_rows),
                      pl.BlockSpec(memory_space=pl.ANY), pl.BlockSpec(memory_space=pl.ANY)],
            out_specs=[pl.BlockSpec((tg, d), lambda j, te, tn, tf: (j, 0)), full(u2_rows)],
            scratch_shapes=[pltpu.VMEM((tg, d), BF16), pltpu.VMEM((nf, d, fc), BF16),
                            pltpu.VMEM((nf, d, fc), BF16), pltpu.VMEM((nf, fc, d), BF16),
                            pltpu.VMEM((2, d, fc), F32), pltpu.VMEM((2, d, fc), F32),
                            pltpu.VMEM((2, fc, d), F32), pltpu.SemaphoreType.DMA((3, 2))]),
        compiler_params=_cparams(("arbitrary",), EXPERT_VMEM_LIMIT),
        name="moe_experts",
    )(tile_e, tile_n, tile_first, xs, u2_rows, comb_rows, w_e1, w_e2)


def _fetch_copy(ys_ref, buf, sem, slot, e, row, start):
    src = ys_ref.at[pl.ds(pl.multiple_of(start, SUBLANES), MOE_CHUNK), :]
    dst = buf.at[slot, e, pl.ds(pl.multiple_of(row, MOE_CHUNK), MOE_CHUNK), :]
    return pltpu.make_async_copy(src, dst, sem.at[slot, e])


def _combine_kernel(start_ref, first_ref, rt_ref, ys_ref, x1_ref, mod_ref, lg_ref, lb_ref, o_ref, buf, yc, sem,
                    *, alpha):
    b = pl.program_id(0)
    nb = pl.num_programs(0)

    def fetch(blk, slot):
        for e in range(N_EXPERTS):
            start = start_ref[blk * N_EXPERTS + e]

            def get(c, carry, e=e, start=start):
                _fetch_copy(ys_ref, buf, sem, slot, e, c * MOE_CHUNK, start + c * MOE_CHUNK).start()
                return carry

            lax.fori_loop(0, _run_chunks(first_ref, blk, e), get, 0)

    @pl.when(b == 0)
    def _():
        fetch(0, 0)
        yc[...] = jnp.zeros_like(yc)

    @pl.when(b + 1 < nb)
    def _():
        fetch(b + 1, (b + 1) % 2)

    cur = b % 2
    tb = x1_ref.shape[0]
    for e in range(N_EXPERTS):
        def landed(c, carry, e=e):
            _fetch_copy(ys_ref, buf, sem, cur, e, 0, 0).wait()
            return carry

        lax.fori_loop(0, _run_chunks(first_ref, b, e), landed, 0)
        first = first_ref[b * (N_EXPERTS + 1) + e]
        groups = (first_ref[b * (N_EXPERTS + 1) + e + 1] - first) // BLOCK_ALIGN

        def pack(g, carry, e=e, first=first):
            src = pl.ds(pl.multiple_of(g * BLOCK_ALIGN, BLOCK_ALIGN), BLOCK_ALIGN)
            dst = pl.ds(pl.multiple_of(first + g * BLOCK_ALIGN, BLOCK_ALIGN), BLOCK_ALIGN)
            yc[dst, :] = buf[cur, e, src, :].astype(BF16)
            return carry

        lax.fori_loop(0, groups, pack, 0)

    slot = lax.broadcasted_iota(jnp.int32, (BLOCK_ROWS, tb), 0).astype(F32)
    gate_t = (jnp.where(slot == rt_ref[0:1, :], rt_ref[2:3, :], 0.0)
              + jnp.where(slot == rt_ref[1:2, :], rt_ref[3:4, :], 0.0)).astype(BF16)
    acc = lax.dot_general(gate_t, yc[...], (((0,), (0,)), ((), ())), preferred_element_type=F32)
    o_ref[...] = _post_norm2(x1_ref[...], acc, mod_ref, lg_ref, lb_ref, False, alpha)


def _combine(start, first, route_t, ys, x1, mod, ln_g, ln_b, *, tiles_per_seq, alpha):
    n, d = x1.shape
    tb = MOE_BLOCK
    full = lambda a: pl.BlockSpec(a.shape, lambda b, s, f: (0,) * a.ndim)
    return pl.pallas_call(
        functools.partial(_combine_kernel, alpha=alpha),
        out_shape=jax.ShapeDtypeStruct((n, d), F32),
        grid_spec=pltpu.PrefetchScalarGridSpec(
            num_scalar_prefetch=2,
            grid=(n // tb,),
            in_specs=[pl.BlockSpec((SUBLANES, tb), lambda b, s, f: (0, b)),
                      pl.BlockSpec(memory_space=pl.ANY),
                      pl.BlockSpec((tb, d), lambda b, s, f: (b, 0)),
                      pl.BlockSpec((1, 6, d), lambda b, s, f: (b // tiles_per_seq, 0, 0)),
                      full(ln_g), full(ln_b)],
            out_specs=pl.BlockSpec((tb, d), lambda b, s, f: (b, 0)),
            scratch_shapes=[pltpu.VMEM((2, N_EXPERTS, tb, d), F32), pltpu.VMEM((BLOCK_ROWS, d), BF16),
                            pltpu.SemaphoreType.DMA((2, N_EXPERTS))]),
        compiler_params=_cparams(("arbitrary",)),
        name="moe_combine",
    )(start, first, route_t, ys, x1, mod, ln_g, ln_b)


def _routing_tables(cum_t, n_tokens, n_slots):
    ne, tb, tg = N_EXPERTS, MOE_BLOCK, MOE_TILE
    nb = n_tokens // tb
    cum = cum_t.reshape(cum_t.shape[0], nb, LANES)[:ne, :, 0].T.astype(jnp.int32)
    total = cum[-1]
    base = jnp.concatenate([jnp.zeros((1, ne), jnp.int32), cum[:-1]], axis=0)
    region = (total + tb + tg - 1) // tg * tg
    end = jnp.cumsum(region)
    off = end - region
    start = jnp.concatenate([(off[None, :] + base).reshape(-1), off + total])
    run = (cum - base + BLOCK_ALIGN - 1) // BLOCK_ALIGN * BLOCK_ALIGN
    first = jnp.concatenate([jnp.zeros((nb, 1), jnp.int32), jnp.cumsum(run, axis=1)], axis=1).reshape(-1)
    fill = jnp.concatenate([off + total + tb, end[-1:], end, jnp.full((1,), n_slots, jnp.int32)]).astype(jnp.int32)
    tile_row = jnp.arange(n_slots // tg, dtype=jnp.int32) * tg
    tile_e = jnp.minimum(jnp.sum(tile_row[:, None] >= end[None, :], axis=1), ne - 1).astype(jnp.int32)
    left = total[tile_e] - (tile_row - off[tile_e])
    tile_n = jnp.clip((left + tb - 1) // tb, 0, tg // tb).astype(jnp.int32)
    tile_first = (tile_row == off[tile_e]).astype(jnp.int32)
    return start, first, fill, tile_e, tile_n, tile_first


def kernel(x_prompt, x_sample, cache_k, cache_v, state_conv, c_prompt, c_sample, w_ada, b_ada, w_in,
           conv_w, g_conv_out, g_att_out, w_out, ln1_g, ln1_b, ln2_g, ln2_b, w_ff1, w_ff2, w_router,
           w_e1, w_e2):
    bsz, s_len, d = x_prompt.shape
    nb_s, t_new, _ = x_sample.shape
    depth = w_in.shape[0]
    dc = conv_w.shape[2]
    da = w_in.shape[2] // 3 - dc
    assert t_new == 1 and da == N_HEADS * HEAD_DIM and da // LANES * LANES == da
    assert s_len % (Q_BLOCK * max(dl for _, dl in DILATED_PATTERNS)) == 0
    assert cache_k.shape[2] == max(w for w, _ in DILATED_PATTERNS)
    alpha = (2 * depth) ** 0.25
    n_p = bsz * s_len
    keep = min(cache_k.shape[2], s_len)
    n_exp, f_exp = w_e2.shape[1], w_e2.shape[2]
    assert n_exp == N_EXPERTS
    n_slots = n_p * TOP_K + n_exp * ((n_p // MOE_BLOCK) * (SUBLANES - 1) + MOE_BLOCK + MOE_TILE - 1)
    n_slots = (n_slots + MOE_TILE - 1) // MOE_TILE * MOE_TILE

    slopes = jnp.exp2(-8.0 * jnp.arange(1, N_HEADS + 1, dtype=F32) / N_HEADS)
    slopes_pairs = jnp.broadcast_to(slopes.reshape(N_HEADS // 2, 2, 1), (N_HEADS // 2, 2, 2 * Q_BLOCK))
    slopes_col = slopes.reshape(N_HEADS, 1, 1)
    w_buf = cache_k.shape[2]
    cache_kt = cache_k.transpose(0, 1, 3, 4, 2).reshape(depth * nb_s, N_HEADS, HEAD_DIM, w_buf)
    cache_vt = cache_v.transpose(0, 1, 3, 4, 2).reshape(depth * nb_s, N_HEADS, HEAD_DIM, w_buf)

    rows_c = (bsz + nb_s + 7) // 8 * 8
    c_all = jnp.zeros((rows_c, d), F32).at[:bsz].set(c_prompt).at[bsz:bsz + nb_s].set(c_sample)
    ada = _adaln_all(c_all, w_ada, b_ada)

    w_e1f = w_e1.reshape((-1,) + w_e1.shape[2:])
    w_e2f = w_e2.reshape((-1,) + w_e2.shape[2:])

    w_in_b = w_in.astype(BF16)
    w_out_b = w_out.astype(BF16)
    xp = x_prompt.reshape(n_p, d)
    xs = x_sample.reshape(nb_s, d)
    tm_p = 512
    assert (s_len - keep) % tm_p == 0
    kt_all = jnp.zeros((depth, bsz, da, keep), F32)
    vt_all = jnp.zeros((depth, bsz, da, keep), F32)
    outs = {k: [] for k in ("cp", "ks", "vs", "cs")}
    row2 = lambda a: a.reshape(1, -1)
    for l in range(depth):
        moe = l % 2 == 1
        li = l // 2
        mod_p = ada[l, :bsz].reshape(bsz, 6, d)
        mod_s = ada[l, bsz:bsz + nb_s].reshape(nb_s, 6, d).transpose(1, 0, 2)
        lnp = (row2(ln1_g[l]), row2(ln1_b[l]))
        ln2 = (row2(ln2_g[l]), row2(ln2_b[l]))
        norm_w = (conv_w[l], row2(g_conv_out[l]), row2(g_att_out[l]))

        bg, z, q, k, v = _in_proj_sample(xs, mod_s, w_in_b, l)
        att = _sample_attention(q, k, v, cache_kt, cache_vt, l, slopes_col)
        st = state_conv[l]
        wr = jnp.zeros((d, LANES), F32).at[:, :n_exp].set(w_router[li]).astype(BF16) if moe else None
        x1_s, u2_s, *comb_s = _out_proj(bg, z, st[:, 1], st[:, 0], att, xs, mod_s, *norm_w, w_out_b, *lnp, wr,
                                        layer=l, per_row=True, tm=nb_s, tiles_per_seq=1, alpha=alpha)
        outs["ks"].append(k.reshape(nb_s, 1, N_HEADS, HEAD_DIM))
        outs["vs"].append(v.reshape(nb_s, 1, N_HEADS, HEAD_DIM))
        outs["cs"].append(jnp.stack([st[:, 1], z], axis=1))

        bg, z, qe, qo, k, v, kt_all, vt_all = _in_proj_prompt(xp, mod_p, w_in_b, kt_all, vt_all, l,
                                                              tm=tm_p, s_len=s_len)
        att = _prompt_attention(qe, qo, k, v, slopes_pairs, bsz, s_len)
        if moe:
            wr_t = jnp.zeros((16, d), F32).at[:n_exp].set(w_router[li].T).astype(BF16)
            x1, u2, route_t, cum_t = _out_proj(
                bg, z, z, z, att, xp, mod_p, *norm_w, w_out_b, *lnp, wr_t,
                layer=l, per_row=False, tm=tm_p, tiles_per_seq=s_len // tm_p, alpha=alpha)
            start, first, fill, tile_e, tile_n, tile_first = _routing_tables(cum_t, n_p, n_slots)
            xsorted = _dispatch(start, first, fill, u2, route_t, n_slots)
            ysorted, mix_s = _experts(tile_e, tile_n, tile_first, xsorted, u2_s, comb_s[0], w_e1f, w_e2f, li,
                                      fc=512)
            xp = _combine(start, first, route_t, ysorted, x1, mod_p, *ln2,
                          tiles_per_seq=s_len // MOE_BLOCK, alpha=alpha)
            xs = _post_norm_rows(x1_s, mix_s, mod_s, *ln2, alpha=alpha)
        else:
            x1, u2 = _out_proj(bg, z, z, z, att, xp, mod_p, *norm_w, w_out_b, *lnp, None,
                               layer=l, per_row=False, tm=tm_p, tiles_per_seq=s_len // tm_p, alpha=alpha)
            xp = _dense_ffn(u2, w_ff1, w_ff2, li, x1, mod_p, *ln2, per_row=False, tm=1024,
                            tiles_per_seq=s_len // 1024, fc=256, alpha=alpha)
            xs = _dense_ffn(u2_s, w_ff1, w_ff2, li, x1_s, mod_s, *ln2, per_row=True, tm=nb_s,
                            tiles_per_seq=1, fc=256, alpha=alpha)
        outs["cp"].append(z.reshape(bsz, s_len, dc)[:, s_len - (CONV_WIDTH - 1):])

    to_rows = lambda t: t.reshape(depth, bsz, N_HEADS, HEAD_DIM, keep).transpose(0, 1, 4, 2, 3)
    return (xp.reshape(bsz, s_len, d), xs.reshape(nb_s, 1, d),
            to_rows(kt_all), to_rows(vt_all), jnp.stack(outs["cp"]),
            jnp.stack(outs["ks"]), jnp.stack(outs["vs"]), jnp.stack(outs["cs"]))
```

```python
import functools

import jax
import jax.numpy as jnp
from jax import lax
from jax.experimental import pallas as pl
from jax.experimental.pallas import tpu as pltpu

F32 = jnp.float32
BF16 = jnp.bfloat16

N_HEADS = 8
HEAD_DIM = 64
CONV_WIDTH = 3
DILATED_PATTERNS = ((128, 1), (512, 4), (2048, 16))
Q_BLOCK = 128
N_EXPERTS = 8
TOP_K = 2
LN_EPS = 1e-5
LOG2E = 1.4426950408889634

LANES = 128
SUBLANES = 8
PROJ_ROWS = 512
FFN_ROWS = 1024
DENSE_CHUNK = 256
EXPERT_CHUNK = 512
SAMPLE_SEQS = 2
SAMPLE_HEAD_GROUP = 4
ATTN_GROUP = 8
MOE_BLOCK = 256
MOE_TILE = 1024
BLOCK_ALIGN = 16
MOE_CHUNK = 32
BLOCK_ROWS = -(-(TOP_K * MOE_BLOCK + N_EXPERTS * (BLOCK_ALIGN - 1)) // LANES) * LANES
VMEM_LIMIT = 56 * 1024 * 1024
EXPERT_VMEM_LIMIT = 61 * 1024 * 1024


def _cparams(sem, vmem=VMEM_LIMIT):
    return pltpu.CompilerParams(dimension_semantics=sem, vmem_limit_bytes=vmem)


def _ln(x):
    mu = jnp.mean(x, axis=-1, keepdims=True)
    xc = x - mu
    var = jnp.mean(xc * xc, axis=-1, keepdims=True)
    return xc * lax.rsqrt(var + LN_EPS)


def _silu(x):
    return x * jax.nn.sigmoid(x)


def _mod_rows(mod_ref, k, per_row):
    return mod_ref[k] if per_row else mod_ref[0, k:k + 1, :]


def _mod_spec(per_row, rows, d, tiles_per_seq):
    if per_row:
        return pl.BlockSpec((6, rows, d), lambda i, *_: (0, 0, 0))
    return pl.BlockSpec((1, 6, d), lambda i, *_: (i // tiles_per_seq, 0, 0))


def _ada_kernel(c_ref, w_ref, b_ref, o_ref):
    c = c_ref[...]
    s = _silu(c).astype(BF16)
    o_ref[0] = jnp.dot(s, w_ref[0].astype(BF16), preferred_element_type=F32) + b_ref[0]


def _adaln_all(c_all, w_ada, b_ada):
    depth, d, e6 = w_ada.shape
    rows = c_all.shape[0]
    tn = e6 // 4
    return pl.pallas_call(
        _ada_kernel,
        out_shape=jax.ShapeDtypeStruct((depth, rows, e6), F32),
        grid=(depth, e6 // tn),
        in_specs=[pl.BlockSpec((rows, d), lambda l, j: (0, 0)),
                  pl.BlockSpec((1, d, tn), lambda l, j: (l, 0, j)),
                  pl.BlockSpec((1, 1, tn), lambda l, j: (l, 0, j))],
        out_specs=pl.BlockSpec((1, rows, tn), lambda l, j: (l, 0, j)),
        compiler_params=_cparams(("arbitrary", "arbitrary")),
        name="adaln",
    )(c_all, w_ada, b_ada.reshape(depth, 1, e6))


def _inproj_kernel(*refs, prompt, dc, tiles_per_seq, first_kept):
    if prompt:
        x_ref, mod_ref, w_ref, _, _, bg_ref, z_ref, qe_ref, qo_ref, k_ref, v_ref, kt_ref, vt_ref = refs
    else:
        x_ref, mod_ref, w_ref, bg_ref, z_ref, q_ref, k_ref, v_ref = refs
    shift = _mod_rows(mod_ref, 0, not prompt)
    scale = _mod_rows(mod_ref, 1, not prompt)
    u = (_ln(x_ref[...]) * (1.0 + scale) + shift).astype(BF16)

    def proj(j):
        return jnp.dot(u, w_ref[:, j * dc:(j + 1) * dc], preferred_element_type=F32)

    bg_ref[...] = proj(0)
    z_ref[...] = proj(1) * proj(2)
    q = proj(3)
    k = proj(4)
    v = proj(5)
    k_ref[...] = k
    v_ref[...] = v
    if not prompt:
        q_ref[...] = q
        return
    q = q * (HEAD_DIM ** -0.5 * LOG2E)
    odd = (lax.broadcasted_iota(jnp.int32, q.shape, 1) // HEAD_DIM) % 2 == 1
    qe_ref[...] = jnp.where(odd, 0.0, q)
    qo_ref[...] = jnp.where(odd, q, 0.0)

    @pl.when(pl.program_id(0) % tiles_per_seq >= first_kept)
    def _():
        kt_ref[...] = k.T
        vt_ref[...] = v.T


def _in_proj_prompt(x, mod, w_in_b, kt_buf, vt_buf, layer, *, tm, s_len):
    n, d = x.shape
    dc = w_in_b.shape[2] // 6
    keep = kt_buf.shape[3]
    tps = s_len // tm
    first_kept = (s_len - keep) // tm
    row = lambda: pl.BlockSpec((tm, dc), lambda i: (i, 0))
    kept = lambda: pl.BlockSpec((None, None, dc, tm),
                                lambda i: (layer, i // tps, 0, jnp.maximum(i % tps - first_kept, 0)))
    rows = jax.ShapeDtypeStruct((n, dc), F32)
    outs = pl.pallas_call(
        functools.partial(_inproj_kernel, prompt=True, dc=dc, tiles_per_seq=tps, first_kept=first_kept),
        out_shape=[rows] * 6 + [jax.ShapeDtypeStruct(kt_buf.shape, F32)] * 2,
        grid=(n // tm,),
        in_specs=[pl.BlockSpec((tm, d), lambda i: (i, 0)),
                  _mod_spec(False, tm, d, tps),
                  pl.BlockSpec((None,) + w_in_b.shape[1:], lambda i: (layer, 0, 0), pipeline_mode=pl.Buffered(1)),
                  pl.BlockSpec(memory_space=pl.ANY), pl.BlockSpec(memory_space=pl.ANY)],
        out_specs=[row() for _ in range(6)] + [kept(), kept()],
        input_output_aliases={3: 6, 4: 7},
        compiler_params=_cparams(("arbitrary",)),
        name="in_proj",
    )(x, mod, w_in_b, kt_buf, vt_buf)
    return outs


def _in_proj_sample(x, mod, w_in_b, layer):
    n, d = x.shape
    dc = w_in_b.shape[2] // 6
    row = lambda: pl.BlockSpec((n, dc), lambda i: (0, 0))
    return pl.pallas_call(
        functools.partial(_inproj_kernel, prompt=False, dc=dc, tiles_per_seq=1, first_kept=0),
        out_shape=[jax.ShapeDtypeStruct((n, dc), F32)] * 5,
        grid=(1,),
        in_specs=[pl.BlockSpec((n, d), lambda i: (0, 0)),
                  _mod_spec(True, n, d, 1),
                  pl.BlockSpec((None,) + w_in_b.shape[1:], lambda i: (layer, 0, 0), pipeline_mode=pl.Buffered(1))],
        out_specs=[row() for _ in range(5)],
        compiler_params=_cparams(("arbitrary",)),
        name="in_proj_rows",
    )(x, mod, w_in_b)


def _attn_kernel(qe_ref, qo_ref, k_ref, v_ref, sl_ref, o_ref, m_scr, l_scr, bias_scr):
    s_len = k_ref.shape[0]
    qb = Q_BLOCK
    half = lax.broadcasted_iota(jnp.int32, (qb, LANES), 1) >= HEAD_DIM
    ri = lax.broadcasted_iota(jnp.int32, (2 * qb, 2 * qb), 0)
    ji = lax.broadcasted_iota(jnp.int32, (2 * qb, 2 * qb), 1)
    step = qb + (ri % qb) - ji
    band = (step >= 0) & (step <= qb)
    slope = jnp.where(ri < qb, sl_ref[0, 0:1, :], sl_ref[0, 1:2, :]) * LOG2E
    for pi, (_, d) in enumerate(DILATED_PATTERNS):
        bias = jnp.where(band, -(slope * (d * step).astype(F32)), -jnp.inf)
        bias_scr[2 * pi] = bias
        bias_scr[2 * pi + 1] = jnp.where(ji < qb, -jnp.inf, bias)
    ones = jnp.ones((2 * qb, LANES), BF16)

    def block(idx, pi, d, first, last):
        r = idx % d
        n = idx // d
        base = n * (qb * d) + r
        prev = jnp.maximum(n - 1, 0) * (qb * d) + r
        rows = pl.ds(base, qb, stride=d)
        prow = pl.ds(prev, qb, stride=d)
        q2 = jnp.concatenate([qe_ref[rows, :], qo_ref[rows, :]], axis=0).astype(BF16)
        k2 = jnp.concatenate([k_ref[prow, :], k_ref[rows, :]], axis=0).astype(BF16)
        v2 = jnp.concatenate([jnp.concatenate([v_ref[prow, :], v_ref[rows, :]], axis=0).astype(BF16), ones],
                             axis=1)
        s = lax.dot_general(q2, k2, (((1,), (1,)), ((), ())), preferred_element_type=F32)
        s = s + bias_scr[2 * pi + jnp.where(n == 0, 1, 0)]
        s0, s1 = s[:, :qb], s[:, qb:]
        mb = jnp.max(jnp.maximum(s0, s1), axis=1, keepdims=True)
        if first:
            m_new = jnp.broadcast_to(mb, (2 * qb, LANES))
        else:
            m_old = jnp.concatenate([m_scr[0, rows, :], m_scr[1, rows, :]], axis=0)
            m_new = jnp.maximum(m_old, mb)
        p = jnp.concatenate([jnp.exp2(s0 - m_new), jnp.exp2(s1 - m_new)], axis=1).astype(BF16)
        pvs = jnp.dot(p, v2, preferred_element_type=F32)
        pv, rs = pvs[:, :LANES], pvs[:, LANES:]
        if first:
            l_new = rs
            acc = jnp.where(half, pv[qb:], pv[:qb])
        else:
            alpha = jnp.exp2(m_old - m_new)
            l_old = jnp.concatenate([l_scr[0, rows, :], l_scr[1, rows, :]], axis=0)
            l_new = alpha * l_old + rs
            acc_old = o_ref[rows, :]
            acc = jnp.where(half, alpha[qb:] * acc_old + pv[qb:], alpha[:qb] * acc_old + pv[:qb])
        if last:
            acc = acc / jnp.where(half, l_new[qb:], l_new[:qb])
        return rows, m_new, l_new, acc

    order = sorted(range(len(DILATED_PATTERNS)), key=lambda pi: -DILATED_PATTERNS[pi][1])
    for pos, pi in enumerate(order):
        d = DILATED_PATTERNS[pi][1]
        first, last = pos == 0, pos == len(order) - 1

        def body(it, carry, pi=pi, d=d, first=first, last=last):
            done = [block(it * ATTN_GROUP + g, pi, d, first, last) for g in range(ATTN_GROUP)]
            for rows, m_new, l_new, acc in done:
                if not last:
                    m_scr[0, rows, :] = m_new[:qb]
                    m_scr[1, rows, :] = m_new[qb:]
                    l_scr[0, rows, :] = l_new[:qb]
                    l_scr[1, rows, :] = l_new[qb:]
                o_ref[rows, :] = acc
            return carry

        lax.fori_loop(0, s_len // (qb * ATTN_GROUP), body, 0)


def _prompt_attention(qe, qo, k, v, slopes_pairs, bsz, s_len):
    da = k.shape[1]
    pairs = da // LANES
    spec = lambda: pl.BlockSpec((None, s_len, LANES), lambda b, h: (b, 0, h))
    rs = lambda a: a.reshape(bsz, s_len, da)
    out = pl.pallas_call(
        _attn_kernel,
        out_shape=jax.ShapeDtypeStruct((bsz, s_len, da), F32),
        grid=(bsz, pairs),
        in_specs=[spec(), spec(), spec(), spec(),
                  pl.BlockSpec((1, 2, 2 * Q_BLOCK), lambda b, h: (h, 0, 0))],
        out_specs=spec(),
        scratch_shapes=[pltpu.VMEM((2, s_len, LANES), F32), pltpu.VMEM((2, s_len, LANES), F32),
                        pltpu.VMEM((2 * len(DILATED_PATTERNS), 2 * Q_BLOCK, 2 * Q_BLOCK), F32)],
        compiler_params=_cparams(("arbitrary", "arbitrary")),
        name="prompt_attention",
    )(rs(qe), rs(qo), rs(k), rs(v), slopes_pairs)
    return out.reshape(bsz * s_len, da)


def _sattn_kernel(q_ref, kn_ref, vn_ref, kt_ref, vt_ref, sl_ref, o_ref):
    bb, nh, hd, w = kt_ref.shape
    da = q_ref.shape[1]
    step = pl.program_id(0)
    dist = w - lax.broadcasted_iota(jnp.int32, (1, w), 1)
    count = jnp.zeros((1, w), F32)
    for window, d in DILATED_PATTERNS:
        count = count + jnp.where((dist % d == 0) & (dist <= window), 1.0, 0.0)
    distf = dist.astype(F32)
    n_pat = float(len(DILATED_PATTERNS))
    scale = HEAD_DIM ** -0.5
    feat = lax.broadcasted_iota(jnp.int32, (hd, da), 0)
    lane = lax.broadcasted_iota(jnp.int32, (hd, da), 1)

    def head(i):
        b = i // nh
        h = i % nh
        row = pl.ds(step * bb + b, 1)
        mine = lane == feat + h * hd

        def column(ref):
            return jnp.sum(jnp.where(mine, ref[row, :], 0.0), axis=1, keepdims=True)

        q = column(q_ref) * scale
        s = jnp.sum(kt_ref[b, h] * q, axis=0, keepdims=True)
        s = jnp.where(count > 0.0, s - sl_ref[h] * distf, -jnp.inf)
        s_self = jnp.sum(q * column(kn_ref), axis=0, keepdims=True)
        m = jnp.maximum(jnp.max(s, axis=1, keepdims=True), s_self)
        p = jnp.exp(s - m) * count
        p_self = n_pat * jnp.exp(s_self - m)
        den = jnp.sum(p, axis=1, keepdims=True) + p_self
        num = jnp.sum(vt_ref[b, h] * p, axis=1, keepdims=True) + p_self * column(vn_ref)
        return jnp.sum(jnp.where(mine, num / den, 0.0), axis=0, keepdims=True)

    def body(it, carry):
        first = it * SAMPLE_HEAD_GROUP
        out = head(first)
        for g in range(1, SAMPLE_HEAD_GROUP):
            out = out + head(first + g)
        row = pl.ds(step * bb + first // nh, 1)

        @pl.when(first % nh == 0)
        def _():
            o_ref[row, :] = out

        @pl.when(first % nh != 0)
        def _():
            o_ref[row, :] += out

        return carry

    lax.fori_loop(0, bb * nh // SAMPLE_HEAD_GROUP, body, 0)


def _sample_attention(q, k, v, cache_kt, cache_vt, layer, slopes_col, *, bb=SAMPLE_SEQS):
    nb, da = q.shape
    _, nh, hd, w_buf = cache_kt.shape
    assert nh % SAMPLE_HEAD_GROUP == 0
    rows = lambda: pl.BlockSpec((nb, da), lambda i: (0, 0))
    steps = nb // bb
    off = layer * steps
    win = lambda: pl.BlockSpec((bb, nh, hd, w_buf), lambda i: (off + i, 0, 0, 0))
    return pl.pallas_call(
        _sattn_kernel,
        out_shape=jax.ShapeDtypeStruct((nb, da), F32),
        grid=(steps,),
        in_specs=[rows(), rows(), rows(), win(), win(), pl.BlockSpec(slopes_col.shape, lambda i: (0, 0, 0))],
        out_specs=rows(),
        compiler_params=_cparams(("arbitrary",)),
        name="sample_attention",
    )(q, k, v, cache_kt, cache_vt, slopes_col)


def _outproj_kernel(*refs, per_row, moe, router_rows, tiles_per_seq, alpha):
    it = iter(refs)
    bg_ref, z_ref, za_ref, zb_ref, att_ref, x_ref, mod_ref = (next(it) for _ in range(7))
    cw_ref, gc_ref, ga_ref, w_ref, lg_ref, lb_ref = (next(it) for _ in range(6))
    wr_ref = next(it) if moe else None
    x1_ref, u2_ref = next(it), next(it)
    if moe and router_rows:
        comb_ref = next(it)
    elif moe:
        route_ref, cum_ref, carry = (next(it) for _ in range(3))

    i = pl.program_id(0)
    z = z_ref[...]
    tm, dc = z.shape
    if per_row:
        z1, z2 = za_ref[...], zb_ref[...]
    else:
        keep = jnp.where(i % tiles_per_seq == 0, 0.0, 1.0)
        h1 = za_ref[7:8, :] * keep
        h2 = za_ref[6:7, :] * keep
        rid = lax.broadcasted_iota(jnp.int32, (tm, dc), 0)
        z1 = jnp.where(rid == 0, h1, pltpu.roll(z, 1, axis=0))
        z2 = jnp.where(rid == 0, h2, jnp.where(rid == 1, h1, pltpu.roll(z, 2, axis=0)))
    conv = cw_ref[0:1, :] * z2 + cw_ref[1:2, :] * z1 + cw_ref[2:3, :] * z
    yc = bg_ref[...] * conv
    yc = yc * lax.rsqrt(jnp.mean(yc * yc, axis=-1, keepdims=True) + LN_EPS) * gc_ref[...]
    ya = att_ref[...]
    ya = ya * lax.rsqrt(jnp.mean(ya * ya, axis=-1, keepdims=True) + LN_EPS) * ga_ref[...]
    h = (jnp.dot(yc.astype(BF16), w_ref[:dc, :], preferred_element_type=F32)
         + jnp.dot(ya.astype(BF16), w_ref[dc:, :], preferred_element_type=F32))
    gate1 = _mod_rows(mod_ref, 2, per_row)
    x1 = _ln(alpha * x_ref[...] + (1.0 + gate1) * h) * lg_ref[...] + lb_ref[...]
    x1_ref[...] = x1
    u2 = (_ln(x1) * (1.0 + _mod_rows(mod_ref, 4, per_row)) + _mod_rows(mod_ref, 3, per_row)).astype(BF16)
    u2_ref[...] = u2
    if not moe:
        return

    if router_rows:
        lg = jnp.dot(u2, wr_ref[...], preferred_element_type=F32)
        lane = lax.broadcasted_iota(jnp.int32, lg.shape, 1)
        lg = jnp.where(lane < N_EXPERTS, lg, -jnp.inf)
        m1 = jnp.max(lg, axis=1, keepdims=True)
        i1 = jnp.min(jnp.where(lg == m1, lane, LANES), axis=1, keepdims=True)
        lg2 = jnp.where(lane == i1, -jnp.inf, lg)
        m2 = jnp.max(lg2, axis=1, keepdims=True)
        i2 = jnp.min(jnp.where(lg2 == m2, lane, LANES), axis=1, keepdims=True)
        ex = jnp.exp(m2 - m1)
        g1 = 1.0 / (1.0 + ex)
        comb_ref[...] = jnp.where(lane == i1, g1, 0.0) + jnp.where(lane == i2, ex * g1, 0.0)
        return

    lg = lax.dot_general(wr_ref[...], u2, (((1,), (1,)), ((), ())), preferred_element_type=F32)
    row = lax.broadcasted_iota(jnp.int32, lg.shape, 0)
    lg = jnp.where(row < N_EXPERTS, lg, -jnp.inf)
    m1 = jnp.max(lg, axis=0, keepdims=True)
    i1 = jnp.min(jnp.where(lg == m1, row, lg.shape[0]), axis=0, keepdims=True)
    lg2 = jnp.where(row == i1, -jnp.inf, lg)
    m2 = jnp.max(lg2, axis=0, keepdims=True)
    i2 = jnp.min(jnp.where(lg2 == m2, row, lg.shape[0]), axis=0, keepdims=True)
    ex = jnp.exp(m2 - m1)
    g1 = 1.0 / (1.0 + ex)
    sel1, sel2 = row == i1, row == i2
    onehot = jnp.where(sel1 | sel2, 1.0, 0.0)
    ta = lax.broadcasted_iota(jnp.int32, (tm, tm), 0)
    tb = lax.broadcasted_iota(jnp.int32, (tm, tm), 1)
    upper = jnp.where((ta < tb) & (ta // MOE_BLOCK == tb // MOE_BLOCK), 1.0, 0.0).astype(BF16)
    rank = jnp.dot(onehot.astype(BF16), upper, preferred_element_type=F32)
    ea = lax.broadcasted_iota(jnp.int32, (lg.shape[0], lg.shape[0]), 0)
    eb = lax.broadcasted_iota(jnp.int32, (lg.shape[0], lg.shape[0]), 1)
    before = jnp.where(eb < ea, 1.0, 0.0).astype(BF16)

    @pl.when(i == 0)
    def _():
        carry[...] = jnp.zeros_like(carry)

    firsts = []
    for blk in range(tm // MOE_BLOCK):
        count = jnp.sum(onehot[:, blk * MOE_BLOCK:(blk + 1) * MOE_BLOCK], axis=1, keepdims=True)
        run = jnp.floor((count + (BLOCK_ALIGN - 1.0)) * (1.0 / BLOCK_ALIGN)) * BLOCK_ALIGN
        first = jnp.dot(before, jnp.broadcast_to(run, (lg.shape[0], LANES)).astype(BF16),
                        preferred_element_type=F32)[:, 0:1]
        firsts.append(jnp.broadcast_to(first, (lg.shape[0], MOE_BLOCK)))
        carry[...] = carry[...] + jnp.floor((count + (SUBLANES - 1.0)) * (1.0 / SUBLANES)) * SUBLANES
        cum_ref[:, blk * LANES:(blk + 1) * LANES] = carry[...]
    pos = jnp.concatenate(firsts, axis=1) + rank
    pos1 = jnp.sum(jnp.where(sel1, pos, 0.0), axis=0, keepdims=True)
    pos2 = jnp.sum(jnp.where(sel2, pos, 0.0), axis=0, keepdims=True)
    r8 = lax.broadcasted_iota(jnp.int32, route_ref.shape, 0)
    route_ref[...] = jnp.where(r8 == 0, pos1, jnp.where(r8 == 1, pos2, jnp.where(
        r8 == 2, g1, jnp.where(r8 == 3, ex * g1, 0.0))))


def _out_proj(bg, z, za, zb, att, x, mod, conv_w, g_conv, g_att, w_out_b, ln_g, ln_b, w_router_b,
              *, layer, per_row, tm, tiles_per_seq, alpha):
    n, d = x.shape
    dc = z.shape[1]
    moe = w_router_b is not None
    router_rows = moe and per_row
    full = lambda a: pl.BlockSpec(a.shape, lambda i: (0,) * a.ndim)
    rowc = lambda: pl.BlockSpec((tm, dc), lambda i: (i, 0))
    rowd = lambda: pl.BlockSpec((tm, d), lambda i: (i, 0))
    if per_row:
        za_spec, zb_spec = rowc(), rowc()
    else:
        za_spec = pl.BlockSpec((8, dc), lambda i: (jnp.maximum(i * (tm // 8) - 1, 0), 0))
        zb_spec = pl.BlockSpec((8, dc), lambda i: (0, 0))
    args = [bg, z, za, zb, att, x, mod, conv_w, g_conv, g_att, w_out_b, ln_g, ln_b]
    in_specs = [rowc(), rowc(), za_spec, zb_spec, rowc(), rowd(), _mod_spec(per_row, tm, d, tiles_per_seq),
                full(conv_w), full(g_conv), full(g_att),
                pl.BlockSpec((None,) + w_out_b.shape[1:], lambda i: (layer, 0, 0), pipeline_mode=pl.Buffered(1)),
                full(ln_g), full(ln_b)]
    out_shape = [jax.ShapeDtypeStruct((n, d), F32), jax.ShapeDtypeStruct((n, d), BF16)]
    out_specs = [rowd(), rowd()]
    scratch = []
    if moe:
        args.append(w_router_b)
        in_specs.append(full(w_router_b))
        if router_rows:
            out_shape.append(jax.ShapeDtypeStruct((n, LANES), F32))
            out_specs.append(pl.BlockSpec((tm, LANES), lambda i: (i, 0)))
        else:
            er = w_router_b.shape[0]
            out_shape += [jax.ShapeDtypeStruct((SUBLANES, n), F32),
                          jax.ShapeDtypeStruct((er, (n // MOE_BLOCK) * LANES), F32)]
            out_specs += [pl.BlockSpec((SUBLANES, tm), lambda i: (0, i)),
                          pl.BlockSpec((er, (tm // MOE_BLOCK) * LANES), lambda i: (0, i))]
            scratch.append(pltpu.VMEM((er, LANES), F32))
    return pl.pallas_call(
        functools.partial(_outproj_kernel, per_row=per_row, moe=moe, router_rows=router_rows,
                          tiles_per_seq=tiles_per_seq, alpha=alpha),
        out_shape=out_shape,
        grid=(n // tm,),
        in_specs=in_specs,
        out_specs=out_specs,
        scratch_shapes=scratch,
        compiler_params=_cparams(("arbitrary",)),
        name="out_proj",
    )(*args)


def _post_norm2(x1, h, mod_ref, lg_ref, lb_ref, per_row, alpha):
    gate2 = _mod_rows(mod_ref, 5, per_row)
    return _ln(alpha * x1 + (1.0 + gate2) * h) * lg_ref[...] + lb_ref[...]


def _ffn_kernel(u_ref, w1_hbm, w2_hbm, x1_ref, mod_ref, lg_ref, lb_ref, o_ref, acc, wgr, wur, w2r, sg, su, s2,
                sem, *, layer, nf, fc, per_row, alpha):
    i = pl.program_id(0)
    f_hidden = nf * fc

    def chunk_copies(c, slot):
        return (pltpu.make_async_copy(w1_hbm.at[layer, :, pl.ds(c * fc, fc)], sg.at[slot], sem.at[0, slot]),
                pltpu.make_async_copy(w1_hbm.at[layer, :, pl.ds(f_hidden + c * fc, fc)], su.at[slot],
                                      sem.at[1, slot]),
                pltpu.make_async_copy(w2_hbm.at[layer, pl.ds(c * fc, fc), :], s2.at[slot], sem.at[2, slot]))

    def ffn(f, first=False):
        u = u_ref[...]
        g = jnp.dot(u, wgr[f], preferred_element_type=F32)
        up = jnp.dot(u, wur[f], preferred_element_type=F32)
        h = (_silu(g) * up).astype(BF16)
        y = jnp.dot(h, w2r[f], preferred_element_type=F32)
        if first:
            acc[...] = y
        else:
            acc[...] += y

    @pl.when(i == 0)
    def _():
        for cp in chunk_copies(0, 0):
            cp.start()
        for c in range(nf):
            if c + 1 < nf:
                for cp in chunk_copies(c + 1, (c + 1) % 2):
                    cp.start()
            for cp in chunk_copies(c, c % 2):
                cp.wait()
            wgr[c] = sg[c % 2].astype(BF16)
            wur[c] = su[c % 2].astype(BF16)
            w2r[c] = s2[c % 2].astype(BF16)
            ffn(c, first=c == 0)

    @pl.when(i > 0)
    def _():
        ffn(0, first=True)

        def chunk(f, carry):
            ffn(f)
            return carry

        lax.fori_loop(1, nf, chunk, 0)

    o_ref[...] = _post_norm2(x1_ref[...], acc[...], mod_ref, lg_ref, lb_ref, per_row, alpha)


def _dense_ffn(u2, w_ff1, w_ff2, layer, x1, mod, ln_g, ln_b, *, per_row, tm, tiles_per_seq, fc, alpha):
    n, d = x1.shape
    f_hidden = w_ff2.shape[1]
    nf = f_hidden // fc
    full = lambda a: pl.BlockSpec(a.shape, lambda i: (0,) * a.ndim)
    return pl.pallas_call(
        functools.partial(_ffn_kernel, layer=layer, nf=nf, fc=fc, per_row=per_row, alpha=alpha),
        out_shape=jax.ShapeDtypeStruct((n, d), F32),
        grid=(n // tm,),
        in_specs=[pl.BlockSpec((tm, d), lambda i: (i, 0)),
                  pl.BlockSpec(memory_space=pl.ANY), pl.BlockSpec(memory_space=pl.ANY),
                  pl.BlockSpec((tm, d), lambda i: (i, 0)),
                  _mod_spec(per_row, tm, d, tiles_per_seq), full(ln_g), full(ln_b)],
        out_specs=pl.BlockSpec((tm, d), lambda i: (i, 0)),
        scratch_shapes=[pltpu.VMEM((tm, d), F32), pltpu.VMEM((nf, d, fc), BF16),
                        pltpu.VMEM((nf, d, fc), BF16), pltpu.VMEM((nf, fc, d), BF16),
                        pltpu.VMEM((2, d, fc), F32), pltpu.VMEM((2, d, fc), F32),
                        pltpu.VMEM((2, fc, d), F32), pltpu.SemaphoreType.DMA((3, 2))],
        compiler_params=_cparams(("arbitrary",)),
        name="dense_ffn",
    )(u2, w_ff1, w_ff2, x1, mod, ln_g, ln_b)


def _post_norm_rows_kernel(x1_ref, h_ref, mod_ref, lg_ref, lb_ref, o_ref, *, alpha):
    o_ref[...] = _post_norm2(x1_ref[...], h_ref[...], mod_ref, lg_ref, lb_ref, True, alpha)


def _post_norm_rows(x1, h, mod, ln_g, ln_b, *, alpha):
    full = lambda a: pl.BlockSpec(a.shape, lambda i: (0,) * a.ndim)
    return pl.pallas_call(
        functools.partial(_post_norm_rows_kernel, alpha=alpha),
        out_shape=jax.ShapeDtypeStruct(x1.shape, F32),
        grid=(1,),
        in_specs=[full(x1), full(h), full(mod), full(ln_g), full(ln_b)],
        out_specs=full(x1),
        compiler_params=_cparams(("arbitrary",)),
        name="post_norm_rows",
    )(x1, h, mod, ln_g, ln_b)


def _slot_copy(stage, xs_ref, sem, par, e, first, start, rows):
    src = stage.at[par, pl.ds(pl.multiple_of(first, SUBLANES), rows), :]
    dst = xs_ref.at[pl.ds(pl.multiple_of(start, SUBLANES), rows), :]
    return pltpu.make_async_copy(src, dst, sem.at[par, e])


def _run_chunks(first_ref, b, e):
    base = b * (N_EXPERTS + 1) + e
    return (first_ref[base + 1] - first_ref[base] + MOE_CHUNK - 1) // MOE_CHUNK


def _dispatch_kernel(start_ref, first_ref, fill_ref, u_ref, rt_ref, xs_ref, stage, sem):
    b = pl.program_id(0)
    nb = pl.num_programs(0)
    par = b % 2
    tb = u_ref.shape[0]

    @pl.when(b < 2)
    def _():
        stage[par, BLOCK_ROWS:, :] = jnp.zeros((MOE_BLOCK, stage.shape[2]), F32)

    slot = lax.broadcasted_iota(jnp.int32, (BLOCK_ROWS, tb), 0).astype(F32)
    sel = jnp.where((slot == rt_ref[0:1, :]) | (slot == rt_ref[1:2, :]), 1.0, 0.0).astype(BF16)
    stage[par, :BLOCK_ROWS, :] = jnp.dot(sel, u_ref[...], preferred_element_type=F32)
    def wait_chunks(blk, p, e):
        def one(c, carry):
            _slot_copy(stage, xs_ref, sem, p, e, 0, 0, MOE_CHUNK).wait()
            return carry

        lax.fori_loop(0, _run_chunks(first_ref, blk, e), one, 0)

    for e in range(N_EXPERTS):
        @pl.when(b > 0)
        def _():
            wait_chunks(b - 1, 1 - par, e)

        first = first_ref[b * (N_EXPERTS + 1) + e]
        start = start_ref[b * N_EXPERTS + e]

        def send(c, carry, e=e, first=first, start=start):
            _slot_copy(stage, xs_ref, sem, par, e, first + c * MOE_CHUNK, start + c * MOE_CHUNK,
                       MOE_CHUNK).start()
            return carry

        lax.fori_loop(0, _run_chunks(first_ref, b, e), send, 0)

    @pl.when(b == nb - 1)
    def _():
        for e in range(N_EXPERTS):
            wait_chunks(b, par, e)
        stage[1 - par, :MOE_BLOCK, :] = jnp.zeros((MOE_BLOCK, stage.shape[2]), F32)
        for e in range(N_EXPERTS):
            _slot_copy(stage, xs_ref, sem, 1 - par, e, 0, start_ref[nb * N_EXPERTS + e], MOE_BLOCK).start()
        for e in range(N_EXPERTS):
            _slot_copy(stage, xs_ref, sem, 1 - par, e, 0, 0, MOE_BLOCK).wait()
        for g in range(N_EXPERTS + 1):
            lo = fill_ref[g]
            hi = fill_ref[N_EXPERTS + 1 + g]
            tiles = (hi - lo) // MOE_BLOCK

            def fill(t, carry, lo=lo):
                _slot_copy(stage, xs_ref, sem, 1 - par, 0, 0, lo + t * MOE_BLOCK, MOE_BLOCK).start()
                return carry

            def drain(t, carry):
                _slot_copy(stage, xs_ref, sem, 1 - par, 0, 0, 0, MOE_BLOCK).wait()
                return carry

            lax.fori_loop(0, tiles, fill, 0)
            lax.fori_loop(0, tiles, drain, 0)

            @pl.when(lo + tiles * MOE_BLOCK < hi)
            def _():
                _slot_copy(stage, xs_ref, sem, 1 - par, 0, 0, hi - MOE_BLOCK, MOE_BLOCK).start()
                _slot_copy(stage, xs_ref, sem, 1 - par, 0, 0, 0, MOE_BLOCK).wait()


def _dispatch(start, first, fill, u2, route_t, n_slots):
    n, d = u2.shape
    tb = MOE_BLOCK
    return pl.pallas_call(
        _dispatch_kernel,
        out_shape=jax.ShapeDtypeStruct((n_slots, d), F32),
        grid_spec=pltpu.PrefetchScalarGridSpec(
            num_scalar_prefetch=3,
            grid=(n // tb,),
            in_specs=[pl.BlockSpec((tb, d), lambda b, s, f, z: (b, 0)),
                      pl.BlockSpec((SUBLANES, tb), lambda b, s, f, z: (0, b))],
            out_specs=pl.BlockSpec(memory_space=pl.ANY),
            scratch_shapes=[pltpu.VMEM((2, BLOCK_ROWS + tb, d), F32),
                            pltpu.SemaphoreType.DMA((2, N_EXPERTS))]),
        compiler_params=_cparams(("arbitrary",)),
        name="moe_dispatch",
    )(start, first, fill, u2, route_t)


def _expert_kernel(te_ref, tn_ref, tf_ref, x_ref, us_ref, comb_ref, w1_hbm, w2_hbm, o_ref, os_ref,
                   xb, wgr, wur, w2r, sg, su, s2, sem, *, layer, nf, fc):
    j = pl.program_id(0)
    n_sub = tn_ref[j]
    subs = o_ref.shape[0] // MOE_BLOCK
    f_hidden = nf * fc
    expert = layer * N_EXPERTS + te_ref[j]

    def chunk_copies(c, slot):
        col = pl.multiple_of(c * fc, LANES)
        return (pltpu.make_async_copy(w1_hbm.at[expert, :, pl.ds(col, fc)], sg.at[slot], sem.at[0, slot]),
                pltpu.make_async_copy(w1_hbm.at[expert, :, pl.ds(pl.multiple_of(f_hidden + col, LANES), fc)],
                                      su.at[slot], sem.at[1, slot]),
                pltpu.make_async_copy(w2_hbm.at[expert, pl.ds(col, fc), :], s2.at[slot], sem.at[2, slot]))

    @pl.when(j == 0)
    def _():
        os_ref[...] = jnp.zeros_like(os_ref)

    full_resident = (tf_ref[j] == 0) & (n_sub == subs)

    @pl.when(jnp.logical_not(full_resident))
    def _():
        o_ref[...] = jnp.zeros_like(o_ref)

    def cast(i, carry):
        rows = pl.ds(pl.multiple_of(i * MOE_BLOCK, MOE_BLOCK), MOE_BLOCK)
        xb[rows, :] = x_ref[rows, :].astype(BF16)
        return carry

    lax.fori_loop(0, n_sub, cast, 0)

    def swiglu(x, f):
        g = jnp.dot(x, wgr[f], preferred_element_type=F32)
        up = jnp.dot(x, wur[f], preferred_element_type=F32)
        h = (_silu(g) * up).astype(BF16)
        return jnp.dot(h, w2r[f], preferred_element_type=F32)

    def ffn(rows, f):
        o_ref[rows, :] += swiglu(xb[rows, :], f)

    def ffn_blocks(f, lo):
        def sub(i, carry):
            ffn(pl.ds(pl.multiple_of(i * MOE_BLOCK, MOE_BLOCK), MOE_BLOCK), f)
            return carry

        lax.fori_loop(lo, n_sub, sub, 0)

    @pl.when(tf_ref[j] == 1)
    def _():
        comb = comb_ref[...]
        lane = lax.broadcasted_iota(jnp.int32, comb.shape, 1)
        gate = jnp.sum(jnp.where(lane == te_ref[j], comb, 0.0), axis=1, keepdims=True)
        for cp in chunk_copies(0, 0):
            cp.start()

        def stream(c, carry):
            slot = c % 2

            @pl.when(c + 1 < nf)
            def _():
                for cp in chunk_copies(c + 1, 1 - slot):
                    cp.start()

            for cp in chunk_copies(c, slot):
                cp.wait()
            wgr[c] = sg[slot].astype(BF16)
            wur[c] = su[slot].astype(BF16)
            w2r[c] = s2[slot].astype(BF16)

            @pl.when(n_sub == subs)
            def _():
                y = swiglu(jnp.concatenate([xb[...], us_ref[...]], axis=0), c)
                o_ref[...] += y[:o_ref.shape[0]]
                os_ref[...] += gate * y[o_ref.shape[0]:]

            @pl.when((n_sub > 0) & (n_sub < subs))
            def _():
                y = swiglu(jnp.concatenate([xb[:MOE_BLOCK, :], us_ref[...]], axis=0), c)
                o_ref[:MOE_BLOCK, :] += y[:MOE_BLOCK]
                os_ref[...] += gate * y[MOE_BLOCK:]
                ffn_blocks(c, 1)

            @pl.when(n_sub == 0)
            def _():
                os_ref[...] += gate * swiglu(us_ref[...], c)

            return carry

        lax.fori_loop(0, nf, stream, 0)

    @pl.when(full_resident)
    def _():
        o_ref[...] = swiglu(xb[...], 0)

        def chunk(f, carry):
            ffn(slice(None), f)
            return carry

        lax.fori_loop(1, nf, chunk, 0)

    @pl.when((tf_ref[j] == 0) & (n_sub > 0) & (n_sub < subs))
    def _():
        def chunk(f, carry):
            ffn_blocks(f, 0)
            return carry

        lax.fori_loop(0, nf, chunk, 0)


def _experts(tile_e, tile_n, tile_first, xs, u2_rows, comb_rows, w_e1, w_e2, layer, *, fc):
    n_slots, d = xs.shape
    f_hidden = w_e2.shape[1]
    nf = f_hidden // fc
    tg = MOE_TILE
    full = lambda a: pl.BlockSpec(a.shape, lambda j, te, tn, tf: (0,) * a.ndim)
    return pl.pallas_call(
        functools.partial(_expert_kernel, layer=layer, nf=nf, fc=fc),
        out_shape=[jax.ShapeDtypeStruct((n_slots, d), F32), jax.ShapeDtypeStruct(u2_rows.shape, F32)],
        grid_spec=pltpu.PrefetchScalarGridSpec(
            num_scalar_prefetch=3,
            grid=(n_slots // tg,),
            in_specs=[pl.BlockSpec((tg, d), lambda j, te, tn, tf: (j, 0)),
                      full(u2_rows), full(comb_rows),
                      pl.BlockSpec(memory_space=pl.ANY), pl.BlockSpec(memory_space=pl.ANY)],
            out_specs=[pl.BlockSpec((tg, d), lambda j, te, tn, tf: (j, 0)), full(u2_rows)],
            scratch_shapes=[pltpu.VMEM((tg, d), BF16), pltpu.VMEM((nf, d, fc), BF16),
                            pltpu.VMEM((nf, d, fc), BF16), pltpu.VMEM((nf, fc, d), BF16),
                            pltpu.VMEM((2, d, fc), F32), pltpu.VMEM((2, d, fc), F32),
                            pltpu.VMEM((2, fc, d), F32), pltpu.SemaphoreType.DMA((3, 2))]),
        compiler_params=_cparams(("arbitrary",), EXPERT_VMEM_LIMIT),
        name="moe_experts",
    )(tile_e, tile_n, tile_first, xs, u2_rows, comb_rows, w_e1, w_e2)


def _fetch_copy(ys_ref, buf, sem, slot, e, row, start):
    src = ys_ref.at[pl.ds(pl.multiple_of(start, SUBLANES), MOE_CHUNK), :]
    dst = buf.at[slot, e, pl.ds(pl.multiple_of(row, MOE_CHUNK), MOE_CHUNK), :]
    return pltpu.make_async_copy(src, dst, sem.at[slot, e])


def _combine_kernel(start_ref, first_ref, rt_ref, ys_ref, x1_ref, mod_ref, lg_ref, lb_ref, o_ref, buf, yc, sem,
                    *, alpha):
    b = pl.program_id(0)
    nb = pl.num_programs(0)

    def fetch(blk, slot):
        for e in range(N_EXPERTS):
            start = start_ref[blk * N_EXPERTS + e]

            def get(c, carry, e=e, start=start):
                _fetch_copy(ys_ref, buf, sem, slot, e, c * MOE_CHUNK, start + c * MOE_CHUNK).start()
                return carry

            lax.fori_loop(0, _run_chunks(first_ref, blk, e), get, 0)

    @pl.when(b == 0)
    def _():
        fetch(0, 0)
        yc[...] = jnp.zeros_like(yc)

    @pl.when(b + 1 < nb)
    def _():
        fetch(b + 1, (b + 1) % 2)

    cur = b % 2
    tb = x1_ref.shape[0]
    for e in range(N_EXPERTS):
        def landed(c, carry, e=e):
            _fetch_copy(ys_ref, buf, sem, cur, e, 0, 0).wait()
            return carry

        lax.fori_loop(0, _run_chunks(first_ref, b, e), landed, 0)
        first = first_ref[b * (N_EXPERTS + 1) + e]
        groups = (first_ref[b * (N_EXPERTS + 1) + e + 1] - first) // BLOCK_ALIGN

        def pack(g, carry, e=e, first=first):
            src = pl.ds(pl.multiple_of(g * BLOCK_ALIGN, BLOCK_ALIGN), BLOCK_ALIGN)
            dst = pl.ds(pl.multiple_of(first + g * BLOCK_ALIGN, BLOCK_ALIGN), BLOCK_ALIGN)
            yc[dst, :] = buf[cur, e, src, :].astype(BF16)
            return carry

        lax.fori_loop(0, groups, pack, 0)

    slot = lax.broadcasted_iota(jnp.int32, (BLOCK_ROWS, tb), 0).astype(F32)
    gate_t = (jnp.where(slot == rt_ref[0:1, :], rt_ref[2:3, :], 0.0)
              + jnp.where(slot == rt_ref[1:2, :], rt_ref[3:4, :], 0.0)).astype(BF16)
    acc = lax.dot_general(gate_t, yc[...], (((0,), (0,)), ((), ())), preferred_element_type=F32)
    o_ref[...] = _post_norm2(x1_ref[...], acc, mod_ref, lg_ref, lb_ref, False, alpha)


def _combine(start, first, route_t, ys, x1, mod, ln_g, ln_b, *, tiles_per_seq, alpha):
    n, d = x1.shape
    tb = MOE_BLOCK
    full = lambda a: pl.BlockSpec(a.shape, lambda b, s, f: (0,) * a.ndim)
    return pl.pallas_call(
        functools.partial(_combine_kernel, alpha=alpha),
        out_shape=jax.ShapeDtypeStruct((n, d), F32),
        grid_spec=pltpu.PrefetchScalarGridSpec(
            num_scalar_prefetch=2,
            grid=(n // tb,),
            in_specs=[pl.BlockSpec((SUBLANES, tb), lambda b, s, f: (0, b)),
                      pl.BlockSpec(memory_space=pl.ANY),
                      pl.BlockSpec((tb, d), lambda b, s, f: (b, 0)),
                      pl.BlockSpec((1, 6, d), lambda b, s, f: (b // tiles_per_seq, 0, 0)),
                      full(ln_g), full(ln_b)],
            out_specs=pl.BlockSpec((tb, d), lambda b, s, f: (b, 0)),
            scratch_shapes=[pltpu.VMEM((2, N_EXPERTS, tb, d), F32), pltpu.VMEM((BLOCK_ROWS, d), BF16),
                            pltpu.SemaphoreType.DMA((2, N_EXPERTS))]),
        compiler_params=_cparams(("arbitrary",)),
        name="moe_combine",
    )(start, first, route_t, ys, x1, mod, ln_g, ln_b)


def _routing_tables(cum_t, n_tokens, n_slots):
    ne, tb, tg = N_EXPERTS, MOE_BLOCK, MOE_TILE
    nb = n_tokens // tb
    cum = cum_t.reshape(cum_t.shape[0], nb, LANES)[:ne, :, 0].T.astype(jnp.int32)
    total = cum[-1]
    base = jnp.concatenate([jnp.zeros((1, ne), jnp.int32), cum[:-1]], axis=0)
    region = (total + tb + tg - 1) // tg * tg
    end = jnp.cumsum(region)
    off = end - region
    start = jnp.concatenate([(off[None, :] + base).reshape(-1), off + total])
    run = (cum - base + BLOCK_ALIGN - 1) // BLOCK_ALIGN * BLOCK_ALIGN
    first = jnp.concatenate([jnp.zeros((nb, 1), jnp.int32), jnp.cumsum(run, axis=1)], axis=1).reshape(-1)
    fill = jnp.concatenate([off + total + tb, end[-1:], end, jnp.full((1,), n_slots, jnp.int32)]).astype(jnp.int32)
    tile_row = jnp.arange(n_slots // tg, dtype=jnp.int32) * tg
    tile_e = jnp.minimum(jnp.sum(tile_row[:, None] >= end[None, :], axis=1), ne - 1).astype(jnp.int32)
    left = total[tile_e] - (tile_row - off[tile_e])
    tile_n = jnp.clip((left + tb - 1) // tb, 0, tg // tb).astype(jnp.int32)
    tile_first = (tile_row == off[tile_e]).astype(jnp.int32)
    return start, first, fill, tile_e, tile_n, tile_first


def kernel(x_prompt, x_sample, cache_k, cache_v, state_conv, c_prompt, c_sample, w_ada, b_ada, w_in,
           conv_w, g_conv_out, g_att_out, w_out, ln1_g, ln1_b, ln2_g, ln2_b, w_ff1, w_ff2, w_router,
           w_e1, w_e2):
    bsz, s_len, d = x_prompt.shape
    nb_s, t_new, _ = x_sample.shape
    depth = w_in.shape[0]
    dc = conv_w.shape[2]
    da = w_in.shape[2] // 3 - dc
    assert t_new == 1 and da == N_HEADS * HEAD_DIM and da // LANES * LANES == da
    assert s_len % (Q_BLOCK * max(dl for _, dl in DILATED_PATTERNS)) == 0
    assert cache_k.shape[2] == max(w for w, _ in DILATED_PATTERNS)
    alpha = (2 * depth) ** 0.25
    n_p = bsz * s_len
    keep = min(cache_k.shape[2], s_len)
    n_exp, f_exp = w_e2.shape[1], w_e2.shape[2]
    assert n_exp == N_EXPERTS
    n_slots = n_p * TOP_K + n_exp * ((n_p // MOE_BLOCK) * (SUBLANES - 1) + MOE_BLOCK + MOE_TILE - 1)
    n_slots = (n_slots + MOE_TILE - 1) // MOE_TILE * MOE_TILE

    slopes = jnp.exp2(-8.0 * jnp.arange(1, N_HEADS + 1, dtype=F32) / N_HEADS)
    slopes_pairs = jnp.broadcast_to(slopes.reshape(N_HEADS // 2, 2, 1), (N_HEADS // 2, 2, 2 * Q_BLOCK))
    slopes_col = slopes.reshape(N_HEADS, 1, 1)
    w_buf = cache_k.shape[2]
    cache_kt = cache_k.transpose(0, 1, 3, 4, 2).reshape(depth * nb_s, N_HEADS, HEAD_DIM, w_buf)
    cache_vt = cache_v.transpose(0, 1, 3, 4, 2).reshape(depth * nb_s, N_HEADS, HEAD_DIM, w_buf)

    rows_c = (bsz + nb_s + 7) // 8 * 8
    c_all = jnp.zeros((rows_c, d), F32).at[:bsz].set(c_prompt).at[bsz:bsz + nb_s].set(c_sample)
    ada = _adaln_all(c_all, w_ada, b_ada)

    w_e1f = w_e1.reshape((-1,) + w_e1.shape[2:])
    w_e2f = w_e2.reshape((-1,) + w_e2.shape[2:])

    w_in_b = w_in.astype(BF16)
    w_out_b = w_out.astype(BF16)
    xp = x_prompt.reshape(n_p, d)
    xs = x_sample.reshape(nb_s, d)
    tm_p = PROJ_ROWS
    assert (s_len - keep) % tm_p == 0 and tm_p % MOE_BLOCK == 0 and s_len % FFN_ROWS == 0
    kt_all = jnp.zeros((depth, bsz, da, keep), F32)
    vt_all = jnp.zeros((depth, bsz, da, keep), F32)
    outs = {k: [] for k in ("cp", "ks", "vs", "cs")}
    row2 = lambda a: a.reshape(1, -1)
    for l in range(depth):
        moe = l % 2 == 1
        li = l // 2
        mod_p = ada[l, :bsz].reshape(bsz, 6, d)
        mod_s = ada[l, bsz:bsz + nb_s].reshape(nb_s, 6, d).transpose(1, 0, 2)
        lnp = (row2(ln1_g[l]), row2(ln1_b[l]))
        ln2 = (row2(ln2_g[l]), row2(ln2_b[l]))
        norm_w = (conv_w[l], row2(g_conv_out[l]), row2(g_att_out[l]))

        bg, z, q, k, v = _in_proj_sample(xs, mod_s, w_in_b, l)
        att = _sample_attention(q, k, v, cache_kt, cache_vt, l, slopes_col)
        st = state_conv[l]
        wr = jnp.zeros((d, LANES), F32).at[:, :n_exp].set(w_router[li]).astype(BF16) if moe else None
        x1_s, u2_s, *comb_s = _out_proj(bg, z, st[:, 1], st[:, 0], att, xs, mod_s, *norm_w, w_out_b, *lnp, wr,
                                        layer=l, per_row=True, tm=nb_s, tiles_per_seq=1, alpha=alpha)
        outs["ks"].append(k.reshape(nb_s, 1, N_HEADS, HEAD_DIM))
        outs["vs"].append(v.reshape(nb_s, 1, N_HEADS, HEAD_DIM))
        outs["cs"].append(jnp.stack([st[:, 1], z], axis=1))

        bg, z, qe, qo, k, v, kt_all, vt_all = _in_proj_prompt(xp, mod_p, w_in_b, kt_all, vt_all, l,
                                                              tm=tm_p, s_len=s_len)
        att = _prompt_attention(qe, qo, k, v, slopes_pairs, bsz, s_len)
        if moe:
            wr_t = jnp.zeros((16, d), F32).at[:n_exp].set(w_router[li].T).astype(BF16)
            x1, u2, route_t, cum_t = _out_proj(
                bg, z, z, z, att, xp, mod_p, *norm_w, w_out_b, *lnp, wr_t,
                layer=l, per_row=False, tm=tm_p, tiles_per_seq=s_len // tm_p, alpha=alpha)
            start, first, fill, tile_e, tile_n, tile_first = _routing_tables(cum_t, n_p, n_slots)
            xsorted = _dispatch(start, first, fill, u2, route_t, n_slots)
            ysorted, mix_s = _experts(tile_e, tile_n, tile_first, xsorted, u2_s, comb_s[0], w_e1f, w_e2f, li,
                                      fc=EXPERT_CHUNK)
            xp = _combine(start, first, route_t, ysorted, x1, mod_p, *ln2,
                          tiles_per_seq=s_len // MOE_BLOCK, alpha=alpha)
            xs = _post_norm_rows(x1_s, mix_s, mod_s, *ln2, alpha=alpha)
        else:
            x1, u2 = _out_proj(bg, z, z, z, att, xp, mod_p, *norm_w, w_out_b, *lnp, None,
                               layer=l, per_row=False, tm=tm_p, tiles_per_seq=s_len // tm_p, alpha=alpha)
            xp = _dense_ffn(u2, w_ff1, w_ff2, li, x1, mod_p, *ln2, per_row=False, tm=FFN_ROWS,
                            tiles_per_seq=s_len // FFN_ROWS, fc=DENSE_CHUNK, alpha=alpha)
            xs = _dense_ffn(u2_s, w_ff1, w_ff2, li, x1_s, mod_s, *ln2, per_row=True, tm=nb_s,
                            tiles_per_seq=1, fc=DENSE_CHUNK, alpha=alpha)
        outs["cp"].append(z.reshape(bsz, s_len, dc)[:, s_len - (CONV_WIDTH - 1):])

    to_rows = lambda t: t.reshape(depth, bsz, N_HEADS, HEAD_DIM, keep).transpose(0, 1, 4, 2, 3)
    return (xp.reshape(bsz, s_len, d), xs.reshape(nb_s, 1, d),
            to_rows(kt_all), to_rows(vt_all), jnp.stack(outs["cp"]),
            jnp.stack(outs["ks"]), jnp.stack(outs["vs"]), jnp.stack(outs["cs"]))
```

```python
import functools

import jax
import jax.numpy as jnp
from jax import lax
from jax.experimental import pallas as pl
from jax.experimental.pallas import tpu as pltpu

F32 = jnp.float32
BF16 = jnp.bfloat16

N_HEADS = 8
HEAD_DIM = 64
CONV_WIDTH = 3
DILATED_PATTERNS = ((128, 1), (512, 4), (2048, 16))
Q_BLOCK = 128
N_EXPERTS = 8
TOP_K = 2
LN_EPS = 1e-5
LOG2E = 1.4426950408889634

LANES = 128
SUBLANES = 8
PROJ_ROWS = 512
FFN_ROWS = 1024
DENSE_CHUNK = 256
EXPERT_CHUNK = 512
SAMPLE_SEQS = 2
SAMPLE_HEAD_GROUP = 4
ATTN_GROUP = 8
MOE_BLOCK = 256
MOE_TILE = 1024
BLOCK_ALIGN = 16
MOE_CHUNK = 32
BLOCK_ROWS = -(-(TOP_K * MOE_BLOCK + N_EXPERTS * (BLOCK_ALIGN - 1)) // LANES) * LANES
VMEM_LIMIT = 56 * 1024 * 1024
EXPERT_VMEM_LIMIT = 61 * 1024 * 1024


def _cparams(sem, vmem=VMEM_LIMIT):
    return pltpu.CompilerParams(dimension_semantics=sem, vmem_limit_bytes=vmem)


def _ln(x):
    mu = jnp.mean(x, axis=-1, keepdims=True)
    xc = x - mu
    var = jnp.mean(xc * xc, axis=-1, keepdims=True)
    return xc * lax.rsqrt(var + LN_EPS)


def _silu(x):
    return x * jax.nn.sigmoid(x)


def _mod_rows(mod_ref, k, per_row):
    return mod_ref[k] if per_row else mod_ref[0, k:k + 1, :]


def _mod_spec(per_row, rows, d, tiles_per_seq):
    if per_row:
        return pl.BlockSpec((6, rows, d), lambda i, *_: (0, 0, 0))
    return pl.BlockSpec((1, 6, d), lambda i, *_: (i // tiles_per_seq, 0, 0))


def _ada_kernel(c_ref, w_ref, b_ref, o_ref):
    c = c_ref[...]
    s = _silu(c).astype(BF16)
    o_ref[0] = jnp.dot(s, w_ref[0].astype(BF16), preferred_element_type=F32) + b_ref[0]


def _adaln_all(c_all, w_ada, b_ada):
    depth, d, e6 = w_ada.shape
    rows = c_all.shape[0]
    tn = e6 // 4
    return pl.pallas_call(
        _ada_kernel,
        out_shape=jax.ShapeDtypeStruct((depth, rows, e6), F32),
        grid=(depth, e6 // tn),
        in_specs=[pl.BlockSpec((rows, d), lambda l, j: (0, 0)),
                  pl.BlockSpec((1, d, tn), lambda l, j: (l, 0, j)),
                  pl.BlockSpec((1, 1, tn), lambda l, j: (l, 0, j))],
        out_specs=pl.BlockSpec((1, rows, tn), lambda l, j: (l, 0, j)),
        compiler_params=_cparams(("arbitrary", "arbitrary")),
        name="adaln",
    )(c_all, w_ada, b_ada.reshape(depth, 1, e6))


def _inproj_kernel(*refs, prompt, dc, tiles_per_seq, first_kept):
    if prompt:
        x_ref, mod_ref, w_ref, _, _, bg_ref, z_ref, qe_ref, qo_ref, k_ref, v_ref, kt_ref, vt_ref = refs
    else:
        x_ref, mod_ref, w_ref, bg_ref, z_ref, q_ref, k_ref, v_ref = refs
    shift = _mod_rows(mod_ref, 0, not prompt)
    scale = _mod_rows(mod_ref, 1, not prompt)
    u = (_ln(x_ref[...]) * (1.0 + scale) + shift).astype(BF16)

    def proj(j):
        return jnp.dot(u, w_ref[:, j * dc:(j + 1) * dc], preferred_element_type=F32)

    bg_ref[...] = proj(0)
    z_ref[...] = proj(1) * proj(2)
    q = proj(3)
    k = proj(4)
    v = proj(5)
    k_ref[...] = k
    v_ref[...] = v
    if not prompt:
        q_ref[...] = q
        return
    q = q * (HEAD_DIM ** -0.5 * LOG2E)
    odd = (lax.broadcasted_iota(jnp.int32, q.shape, 1) // HEAD_DIM) % 2 == 1
    qe_ref[...] = jnp.where(odd, 0.0, q)
    qo_ref[...] = jnp.where(odd, q, 0.0)

    @pl.when(pl.program_id(0) % tiles_per_seq >= first_kept)
    def _():
        kt_ref[...] = k.T
        vt_ref[...] = v.T


def _in_proj_prompt(x, mod, w_in_b, kt_buf, vt_buf, layer, *, tm, s_len):
    n, d = x.shape
    dc = w_in_b.shape[2] // 6
    keep = kt_buf.shape[3]
    tps = s_len // tm
    first_kept = (s_len - keep) // tm
    row = lambda: pl.BlockSpec((tm, dc), lambda i: (i, 0))
    kept = lambda: pl.BlockSpec((None, None, dc, tm),
                                lambda i: (layer, i // tps, 0, jnp.maximum(i % tps - first_kept, 0)))
    rows = jax.ShapeDtypeStruct((n, dc), F32)
    outs = pl.pallas_call(
        functools.partial(_inproj_kernel, prompt=True, dc=dc, tiles_per_seq=tps, first_kept=first_kept),
        out_shape=[rows] * 6 + [jax.ShapeDtypeStruct(kt_buf.shape, F32)] * 2,
        grid=(n // tm,),
        in_specs=[pl.BlockSpec((tm, d), lambda i: (i, 0)),
                  _mod_spec(False, tm, d, tps),
                  pl.BlockSpec((None,) + w_in_b.shape[1:], lambda i: (layer, 0, 0), pipeline_mode=pl.Buffered(1)),
                  pl.BlockSpec(memory_space=pl.ANY), pl.BlockSpec(memory_space=pl.ANY)],
        out_specs=[row() for _ in range(6)] + [kept(), kept()],
        input_output_aliases={3: 6, 4: 7},
        compiler_params=_cparams(("arbitrary",)),
        name="in_proj",
    )(x, mod, w_in_b, kt_buf, vt_buf)
    return outs


def _in_proj_sample(x, mod, w_in_b, layer):
    n, d = x.shape
    dc = w_in_b.shape[2] // 6
    row = lambda: pl.BlockSpec((n, dc), lambda i: (0, 0))
    return pl.pallas_call(
        functools.partial(_inproj_kernel, prompt=False, dc=dc, tiles_per_seq=1, first_kept=0),
        out_shape=[jax.ShapeDtypeStruct((n, dc), F32)] * 5,
        grid=(1,),
        in_specs=[pl.BlockSpec((n, d), lambda i: (0, 0)),
                  _mod_spec(True, n, d, 1),
                  pl.BlockSpec((None,) + w_in_b.shape[1:], lambda i: (layer, 0, 0), pipeline_mode=pl.Buffered(1))],
        out_specs=[row() for _ in range(5)],
        compiler_params=_cparams(("arbitrary",)),
        name="in_proj_rows",
    )(x, mod, w_in_b)


def _attn_kernel(qe_ref, qo_ref, k_ref, v_ref, sl_ref, o_ref, m_scr, l_scr, bias_scr):
    s_len = k_ref.shape[0]
    qb = Q_BLOCK
    half = lax.broadcasted_iota(jnp.int32, (qb, LANES), 1) >= HEAD_DIM
    ri = lax.broadcasted_iota(jnp.int32, (2 * qb, 2 * qb), 0)
    ji = lax.broadcasted_iota(jnp.int32, (2 * qb, 2 * qb), 1)
    step = qb + (ri % qb) - ji
    band = (step >= 0) & (step <= qb)
    slope = jnp.where(ri < qb, sl_ref[0, 0:1, :], sl_ref[0, 1:2, :]) * LOG2E
    for pi, (_, d) in enumerate(DILATED_PATTERNS):
        bias = jnp.where(band, -(slope * (d * step).astype(F32)), -jnp.inf)
        bias_scr[2 * pi] = bias
        bias_scr[2 * pi + 1] = jnp.where(ji < qb, -jnp.inf, bias)
    ones = jnp.ones((2 * qb, LANES), BF16)

    def block(idx, pi, d, first, last):
        r = idx % d
        n = idx // d
        base = n * (qb * d) + r
        prev = jnp.maximum(n - 1, 0) * (qb * d) + r
        rows = pl.ds(base, qb, stride=d)
        prow = pl.ds(prev, qb, stride=d)
        q2 = jnp.concatenate([qe_ref[rows, :], qo_ref[rows, :]], axis=0).astype(BF16)
        k2 = jnp.concatenate([k_ref[prow, :], k_ref[rows, :]], axis=0).astype(BF16)
        v2 = jnp.concatenate([jnp.concatenate([v_ref[prow, :], v_ref[rows, :]], axis=0).astype(BF16), ones],
                             axis=1)
        s = lax.dot_general(q2, k2, (((1,), (1,)), ((), ())), preferred_element_type=F32)
        s = s + bias_scr[2 * pi + jnp.where(n == 0, 1, 0)]
        s0, s1 = s[:, :qb], s[:, qb:]
        mb = jnp.max(jnp.maximum(s0, s1), axis=1, keepdims=True)
        if first:
            m_new = jnp.broadcast_to(mb, (2 * qb, LANES))
        else:
            m_old = jnp.concatenate([m_scr[0, rows, :], m_scr[1, rows, :]], axis=0)
            m_new = jnp.maximum(m_old, mb)
        p = jnp.concatenate([jnp.exp2(s0 - m_new), jnp.exp2(s1 - m_new)], axis=1).astype(BF16)
        pvs = jnp.dot(p, v2, preferred_element_type=F32)
        pv, rs = pvs[:, :LANES], pvs[:, LANES:]
        if first:
            l_new = rs
            acc = jnp.where(half, pv[qb:], pv[:qb])
        else:
            alpha = jnp.exp2(m_old - m_new)
            l_old = jnp.concatenate([l_scr[0, rows, :], l_scr[1, rows, :]], axis=0)
            l_new = alpha * l_old + rs
            acc_old = o_ref[rows, :]
            acc = jnp.where(half, alpha[qb:] * acc_old + pv[qb:], alpha[:qb] * acc_old + pv[:qb])
        if last:
            acc = acc / jnp.where(half, l_new[qb:], l_new[:qb])
        return rows, m_new, l_new, acc

    order = sorted(range(len(DILATED_PATTERNS)), key=lambda pi: -DILATED_PATTERNS[pi][1])
    for pos, pi in enumerate(order):
        d = DILATED_PATTERNS[pi][1]
        first, last = pos == 0, pos == len(order) - 1

        def body(it, carry, pi=pi, d=d, first=first, last=last):
            done = [block(it * ATTN_GROUP + g, pi, d, first, last) for g in range(ATTN_GROUP)]
            for rows, m_new, l_new, acc in done:
                if not last:
                    m_scr[0, rows, :] = m_new[:qb]
                    m_scr[1, rows, :] = m_new[qb:]
                    l_scr[0, rows, :] = l_new[:qb]
                    l_scr[1, rows, :] = l_new[qb:]
                o_ref[rows, :] = acc
            return carry

        lax.fori_loop(0, s_len // (qb * ATTN_GROUP), body, 0)


def _prompt_attention(qe, qo, k, v, slopes_pairs, bsz, s_len):
    da = k.shape[1]
    pairs = da // LANES
    spec = lambda: pl.BlockSpec((None, s_len, LANES), lambda b, h: (b, 0, h))
    rs = lambda a: a.reshape(bsz, s_len, da)
    out = pl.pallas_call(
        _attn_kernel,
        out_shape=jax.ShapeDtypeStruct((bsz, s_len, da), F32),
        grid=(bsz, pairs),
        in_specs=[spec(), spec(), spec(), spec(),
                  pl.BlockSpec((1, 2, 2 * Q_BLOCK), lambda b, h: (h, 0, 0))],
        out_specs=spec(),
        scratch_shapes=[pltpu.VMEM((2, s_len, LANES), F32), pltpu.VMEM((2, s_len, LANES), F32),
                        pltpu.VMEM((2 * len(DILATED_PATTERNS), 2 * Q_BLOCK, 2 * Q_BLOCK), F32)],
        compiler_params=_cparams(("arbitrary", "arbitrary")),
        name="prompt_attention",
    )(rs(qe), rs(qo), rs(k), rs(v), slopes_pairs)
    return out.reshape(bsz * s_len, da)


def _sattn_kernel(q_ref, kn_ref, vn_ref, kt_ref, vt_ref, sl_ref, o_ref):
    bb, nh, hd, w = kt_ref.shape
    da = q_ref.shape[1]
    step = pl.program_id(0)
    dist = w - lax.broadcasted_iota(jnp.int32, (1, w), 1)
    count = jnp.zeros((1, w), F32)
    for window, d in DILATED_PATTERNS:
        count = count + jnp.where((dist % d == 0) & (dist <= window), 1.0, 0.0)
    distf = dist.astype(F32)
    n_pat = float(len(DILATED_PATTERNS))
    scale = HEAD_DIM ** -0.5
    feat = lax.broadcasted_iota(jnp.int32, (hd, da), 0)
    lane = lax.broadcasted_iota(jnp.int32, (hd, da), 1)

    def head(i):
        b = i // nh
        h = i % nh
        row = pl.ds(step * bb + b, 1)
        mine = lane == feat + h * hd

        def column(ref):
            return jnp.sum(jnp.where(mine, ref[row, :], 0.0), axis=1, keepdims=True)

        q = column(q_ref) * scale
        s = jnp.sum(kt_ref[b, h] * q, axis=0, keepdims=True)
        s = jnp.where(count > 0.0, s - sl_ref[h] * distf, -jnp.inf)
        s_self = jnp.sum(q * column(kn_ref), axis=0, keepdims=True)
        m = jnp.maximum(jnp.max(s, axis=1, keepdims=True), s_self)
        p = jnp.exp(s - m) * count
        p_self = n_pat * jnp.exp(s_self - m)
        den = jnp.sum(p, axis=1, keepdims=True) + p_self
        num = jnp.sum(vt_ref[b, h] * p, axis=1, keepdims=True) + p_self * column(vn_ref)
        return jnp.sum(jnp.where(mine, num / den, 0.0), axis=0, keepdims=True)

    def body(it, carry):
        first = it * SAMPLE_HEAD_GROUP
        out = head(first)
        for g in range(1, SAMPLE_HEAD_GROUP):
            out = out + head(first + g)
        row = pl.ds(step * bb + first // nh, 1)

        @pl.when(first % nh == 0)
        def _():
            o_ref[row, :] = out

        @pl.when(first % nh != 0)
        def _():
            o_ref[row, :] += out

        return carry

    lax.fori_loop(0, bb * nh // SAMPLE_HEAD_GROUP, body, 0)


def _sample_attention(q, k, v, cache_kt, cache_vt, layer, slopes_col, *, bb=SAMPLE_SEQS):
    nb, da = q.shape
    _, nh, hd, w_buf = cache_kt.shape
    assert nh % SAMPLE_HEAD_GROUP == 0
    rows = lambda: pl.BlockSpec((nb, da), lambda i: (0, 0))
    steps = nb // bb
    off = layer * steps
    win = lambda: pl.BlockSpec((bb, nh, hd, w_buf), lambda i: (off + i, 0, 0, 0))
    return pl.pallas_call(
        _sattn_kernel,
        out_shape=jax.ShapeDtypeStruct((nb, da), F32),
        grid=(steps,),
        in_specs=[rows(), rows(), rows(), win(), win(), pl.BlockSpec(slopes_col.shape, lambda i: (0, 0, 0))],
        out_specs=rows(),
        compiler_params=_cparams(("arbitrary",)),
        name="sample_attention",
    )(q, k, v, cache_kt, cache_vt, slopes_col)


def _outproj_kernel(*refs, per_row, moe, router_rows, tiles_per_seq, alpha):
    it = iter(refs)
    bg_ref, z_ref, za_ref, zb_ref, att_ref, x_ref, mod_ref = (next(it) for _ in range(7))
    cw_ref, gc_ref, ga_ref, w_ref, lg_ref, lb_ref = (next(it) for _ in range(6))
    wr_ref = next(it) if moe else None
    x1_ref, u2_ref = next(it), next(it)
    if moe and router_rows:
        comb_ref = next(it)
    elif moe:
        route_ref, cum_ref, carry = (next(it) for _ in range(3))

    i = pl.program_id(0)
    z = z_ref[...]
    tm, dc = z.shape
    if per_row:
        z1, z2 = za_ref[...], zb_ref[...]
    else:
        keep = jnp.where(i % tiles_per_seq == 0, 0.0, 1.0)
        h1 = za_ref[7:8, :] * keep
        h2 = za_ref[6:7, :] * keep
        rid = lax.broadcasted_iota(jnp.int32, (tm, dc), 0)
        z1 = jnp.where(rid == 0, h1, pltpu.roll(z, 1, axis=0))
        z2 = jnp.where(rid == 0, h2, jnp.where(rid == 1, h1, pltpu.roll(z, 2, axis=0)))
    conv = cw_ref[0:1, :] * z2 + cw_ref[1:2, :] * z1 + cw_ref[2:3, :] * z
    yc = bg_ref[...] * conv
    yc = yc * lax.rsqrt(jnp.mean(yc * yc, axis=-1, keepdims=True) + LN_EPS) * gc_ref[...]
    ya = att_ref[...]
    ya = ya * lax.rsqrt(jnp.mean(ya * ya, axis=-1, keepdims=True) + LN_EPS) * ga_ref[...]
    h = (jnp.dot(yc.astype(BF16), w_ref[:dc, :], preferred_element_type=F32)
         + jnp.dot(ya.astype(BF16), w_ref[dc:, :], preferred_element_type=F32))
    gate1 = _mod_rows(mod_ref, 2, per_row)
    x1 = _ln(alpha * x_ref[...] + (1.0 + gate1) * h) * lg_ref[...] + lb_ref[...]
    x1_ref[...] = x1
    u2 = (_ln(x1) * (1.0 + _mod_rows(mod_ref, 4, per_row)) + _mod_rows(mod_ref, 3, per_row)).astype(BF16)
    u2_ref[...] = u2
    if not moe:
        return

    if router_rows:
        lg = jnp.dot(u2, wr_ref[...], preferred_element_type=F32)
        lane = lax.broadcasted_iota(jnp.int32, lg.shape, 1)
        lg = jnp.where(lane < N_EXPERTS, lg, -jnp.inf)
        m1 = jnp.max(lg, axis=1, keepdims=True)
        i1 = jnp.min(jnp.where(lg == m1, lane, LANES), axis=1, keepdims=True)
        lg2 = jnp.where(lane == i1, -jnp.inf, lg)
        m2 = jnp.max(lg2, axis=1, keepdims=True)
        i2 = jnp.min(jnp.where(lg2 == m2, lane, LANES), axis=1, keepdims=True)
        ex = jnp.exp(m2 - m1)
        g1 = 1.0 / (1.0 + ex)
        comb_ref[...] = jnp.where(lane == i1, g1, 0.0) + jnp.where(lane == i2, ex * g1, 0.0)
        return

    lg = lax.dot_general(wr_ref[...], u2, (((1,), (1,)), ((), ())), preferred_element_type=F32)
    row = lax.broadcasted_iota(jnp.int32, lg.shape, 0)
    lg = jnp.where(row < N_EXPERTS, lg, -jnp.inf)
    m1 = jnp.max(lg, axis=0, keepdims=True)
    i1 = jnp.min(jnp.where(lg == m1, row, lg.shape[0]), axis=0, keepdims=True)
    lg2 = jnp.where(row == i1, -jnp.inf, lg)
    m2 = jnp.max(lg2, axis=0, keepdims=True)
    i2 = jnp.min(jnp.where(lg2 == m2, row, lg.shape[0]), axis=0, keepdims=True)
    ex = jnp.exp(m2 - m1)
    g1 = 1.0 / (1.0 + ex)
    sel1, sel2 = row == i1, row == i2
    onehot = jnp.where(sel1 | sel2, 1.0, 0.0)
    ta = lax.broadcasted_iota(jnp.int32, (tm, tm), 0)
    tb = lax.broadcasted_iota(jnp.int32, (tm, tm), 1)
    upper = jnp.where((ta < tb) & (ta // MOE_BLOCK == tb // MOE_BLOCK), 1.0, 0.0).astype(BF16)
    rank = jnp.dot(onehot.astype(BF16), upper, preferred_element_type=F32)
    ea = lax.broadcasted_iota(jnp.int32, (lg.shape[0], lg.shape[0]), 0)
    eb = lax.broadcasted_iota(jnp.int32, (lg.shape[0], lg.shape[0]), 1)
    before = jnp.where(eb < ea, 1.0, 0.0).astype(BF16)

    @pl.when(i == 0)
    def _():
        carry[...] = jnp.zeros_like(carry)

    firsts = []
    for blk in range(tm // MOE_BLOCK):
        count = jnp.sum(onehot[:, blk * MOE_BLOCK:(blk + 1) * MOE_BLOCK], axis=1, keepdims=True)
        run = jnp.floor((count + (BLOCK_ALIGN - 1.0)) * (1.0 / BLOCK_ALIGN)) * BLOCK_ALIGN
        first = jnp.dot(before, jnp.broadcast_to(run, (lg.shape[0], LANES)).astype(BF16),
                        preferred_element_type=F32)[:, 0:1]
        firsts.append(jnp.broadcast_to(first, (lg.shape[0], MOE_BLOCK)))
        carry[...] = carry[...] + jnp.floor((count + (SUBLANES - 1.0)) * (1.0 / SUBLANES)) * SUBLANES
        cum_ref[:, blk * LANES:(blk + 1) * LANES] = carry[...]
    pos = jnp.concatenate(firsts, axis=1) + rank
    pos1 = jnp.sum(jnp.where(sel1, pos, 0.0), axis=0, keepdims=True)
    pos2 = jnp.sum(jnp.where(sel2, pos, 0.0), axis=0, keepdims=True)
    r8 = lax.broadcasted_iota(jnp.int32, route_ref.shape, 0)
    route_ref[...] = jnp.where(r8 == 0, pos1, jnp.where(r8 == 1, pos2, jnp.where(
        r8 == 2, g1, jnp.where(r8 == 3, ex * g1, 0.0))))


def _out_proj(bg, z, za, zb, att, x, mod, conv_w, g_conv, g_att, w_out_b, ln_g, ln_b, w_router_b,
              *, layer, per_row, tm, tiles_per_seq, alpha):
    n, d = x.shape
    dc = z.shape[1]
    moe = w_router_b is not None
    router_rows = moe and per_row
    full = lambda a: pl.BlockSpec(a.shape, lambda i: (0,) * a.ndim)
    rowc = lambda: pl.BlockSpec((tm, dc), lambda i: (i, 0))
    rowd = lambda: pl.BlockSpec((tm, d), lambda i: (i, 0))
    if per_row:
        za_spec, zb_spec = rowc(), rowc()
    else:
        za_spec = pl.BlockSpec((8, dc), lambda i: (jnp.maximum(i * (tm // 8) - 1, 0), 0))
        zb_spec = pl.BlockSpec((8, dc), lambda i: (0, 0))
    args = [bg, z, za, zb, att, x, mod, conv_w, g_conv, g_att, w_out_b, ln_g, ln_b]
    in_specs = [rowc(), rowc(), za_spec, zb_spec, rowc(), rowd(), _mod_spec(per_row, tm, d, tiles_per_seq),
                full(conv_w), full(g_conv), full(g_att),
                pl.BlockSpec((None,) + w_out_b.shape[1:], lambda i: (layer, 0, 0), pipeline_mode=pl.Buffered(1)),
                full(ln_g), full(ln_b)]
    out_shape = [jax.ShapeDtypeStruct((n, d), F32), jax.ShapeDtypeStruct((n, d), BF16)]
    out_specs = [rowd(), rowd()]
    scratch = []
    if moe:
        args.append(w_router_b)
        in_specs.append(full(w_router_b))
        if router_rows:
            out_shape.append(jax.ShapeDtypeStruct((n, LANES), F32))
            out_specs.append(pl.BlockSpec((tm, LANES), lambda i: (i, 0)))
        else:
            er = w_router_b.shape[0]
            out_shape += [jax.ShapeDtypeStruct((SUBLANES, n), F32),
                          jax.ShapeDtypeStruct((er, (n // MOE_BLOCK) * LANES), F32)]
            out_specs += [pl.BlockSpec((SUBLANES, tm), lambda i: (0, i)),
                          pl.BlockSpec((er, (tm // MOE_BLOCK) * LANES), lambda i: (0, i))]
            scratch.append(pltpu.VMEM((er, LANES), F32))
    return pl.pallas_call(
        functools.partial(_outproj_kernel, per_row=per_row, moe=moe, router_rows=router_rows,
                          tiles_per_seq=tiles_per_seq, alpha=alpha),
        out_shape=out_shape,
        grid=(n // tm,),
        in_specs=in_specs,
        out_specs=out_specs,
        scratch_shapes=scratch,
        compiler_params=_cparams(("arbitrary",)),
        name="out_proj",
    )(*args)


def _post_norm2(x1, h, mod_ref, lg_ref, lb_ref, per_row, alpha):
    gate2 = _mod_rows(mod_ref, 5, per_row)
    return _ln(alpha * x1 + (1.0 + gate2) * h) * lg_ref[...] + lb_ref[...]


def _ffn_kernel(u_ref, w1_hbm, w2_hbm, x1_ref, mod_ref, lg_ref, lb_ref, o_ref, acc, wgr, wur, w2r, sg, su, s2,
                sem, *, layer, nf, fc, per_row, alpha):
    i = pl.program_id(0)
    f_hidden = nf * fc

    def chunk_copies(c, slot):
        return (pltpu.make_async_copy(w1_hbm.at[layer, :, pl.ds(c * fc, fc)], sg.at[slot], sem.at[0, slot]),
                pltpu.make_async_copy(w1_hbm.at[layer, :, pl.ds(f_hidden + c * fc, fc)], su.at[slot],
                                      sem.at[1, slot]),
                pltpu.make_async_copy(w2_hbm.at[layer, pl.ds(c * fc, fc), :], s2.at[slot], sem.at[2, slot]))

    def ffn(f, first=False):
        u = u_ref[...]
        g = jnp.dot(u, wgr[f], preferred_element_type=F32)
        up = jnp.dot(u, wur[f], preferred_element_type=F32)
        h = (_silu(g) * up).astype(BF16)
        y = jnp.dot(h, w2r[f], preferred_element_type=F32)
        if first:
            acc[...] = y
        else:
            acc[...] += y

    @pl.when(i == 0)
    def _():
        for cp in chunk_copies(0, 0):
            cp.start()
        for c in range(nf):
            if c + 1 < nf:
                for cp in chunk_copies(c + 1, (c + 1) % 2):
                    cp.start()
            for cp in chunk_copies(c, c % 2):
                cp.wait()
            wgr[c] = sg[c % 2].astype(BF16)
            wur[c] = su[c % 2].astype(BF16)
            w2r[c] = s2[c % 2].astype(BF16)
            ffn(c, first=c == 0)

    @pl.when(i > 0)
    def _():
        ffn(0, first=True)

        def chunk(f, carry):
            ffn(f)
            return carry

        lax.fori_loop(1, nf, chunk, 0)

    o_ref[...] = _post_norm2(x1_ref[...], acc[...], mod_ref, lg_ref, lb_ref, per_row, alpha)


def _dense_ffn(u2, w_ff1, w_ff2, layer, x1, mod, ln_g, ln_b, *, per_row, tm, tiles_per_seq, fc, alpha):
    n, d = x1.shape
    f_hidden = w_ff2.shape[1]
    nf = f_hidden // fc
    full = lambda a: pl.BlockSpec(a.shape, lambda i: (0,) * a.ndim)
    return pl.pallas_call(
        functools.partial(_ffn_kernel, layer=layer, nf=nf, fc=fc, per_row=per_row, alpha=alpha),
        out_shape=jax.ShapeDtypeStruct((n, d), F32),
        grid=(n // tm,),
        in_specs=[pl.BlockSpec((tm, d), lambda i: (i, 0)),
                  pl.BlockSpec(memory_space=pl.ANY), pl.BlockSpec(memory_space=pl.ANY),
                  pl.BlockSpec((tm, d), lambda i: (i, 0)),
                  _mod_spec(per_row, tm, d, tiles_per_seq), full(ln_g), full(ln_b)],
        out_specs=pl.BlockSpec((tm, d), lambda i: (i, 0)),
        scratch_shapes=[pltpu.VMEM((tm, d), F32), pltpu.VMEM((nf, d, fc), BF16),
                        pltpu.VMEM((nf, d, fc), BF16), pltpu.VMEM((nf, fc, d), BF16),
                        pltpu.VMEM((2, d, fc), F32), pltpu.VMEM((2, d, fc), F32),
                        pltpu.VMEM((2, fc, d), F32), pltpu.SemaphoreType.DMA((3, 2))],
        compiler_params=_cparams(("arbitrary",)),
        name="dense_ffn",
    )(u2, w_ff1, w_ff2, x1, mod, ln_g, ln_b)


def _post_norm_rows_kernel(x1_ref, h_ref, mod_ref, lg_ref, lb_ref, o_ref, *, alpha):
    o_ref[...] = _post_norm2(x1_ref[...], h_ref[...], mod_ref, lg_ref, lb_ref, True, alpha)


def _post_norm_rows(x1, h, mod, ln_g, ln_b, *, alpha):
    full = lambda a: pl.BlockSpec(a.shape, lambda i: (0,) * a.ndim)
    return pl.pallas_call(
        functools.partial(_post_norm_rows_kernel, alpha=alpha),
        out_shape=jax.ShapeDtypeStruct(x1.shape, F32),
        grid=(1,),
        in_specs=[full(x1), full(h), full(mod), full(ln_g), full(ln_b)],
        out_specs=full(x1),
        compiler_params=_cparams(("arbitrary",)),
        name="post_norm_rows",
    )(x1, h, mod, ln_g, ln_b)


def _slot_copy(stage, xs_ref, sem, par, e, first, start, rows):
    src = stage.at[par, pl.ds(pl.multiple_of(first, SUBLANES), rows), :]
    dst = xs_ref.at[pl.ds(pl.multiple_of(start, SUBLANES), rows), :]
    return pltpu.make_async_copy(src, dst, sem.at[par, e])


def _run_chunks(first_ref, b, e):
    base = b * (N_EXPERTS + 1) + e
    return (first_ref[base + 1] - first_ref[base] + MOE_CHUNK - 1) // MOE_CHUNK


def _dispatch_kernel(start_ref, first_ref, fill_ref, u_ref, rt_ref, xs_ref, stage, sem):
    b = pl.program_id(0)
    nb = pl.num_programs(0)
    par = b % 2
    tb = u_ref.shape[0]

    @pl.when(b < 2)
    def _():
        stage[par, BLOCK_ROWS:, :] = jnp.zeros((MOE_BLOCK, stage.shape[2]), F32)

    slot = lax.broadcasted_iota(jnp.int32, (BLOCK_ROWS, tb), 0).astype(F32)
    sel = jnp.where((slot == rt_ref[0:1, :]) | (slot == rt_ref[1:2, :]), 1.0, 0.0).astype(BF16)
    stage[par, :BLOCK_ROWS, :] = jnp.dot(sel, u_ref[...], preferred_element_type=F32)
    def wait_chunks(blk, p, e):
        def one(c, carry):
            _slot_copy(stage, xs_ref, sem, p, e, 0, 0, MOE_CHUNK).wait()
            return carry

        lax.fori_loop(0, _run_chunks(first_ref, blk, e), one, 0)

    for e in range(N_EXPERTS):
        @pl.when(b > 0)
        def _():
            wait_chunks(b - 1, 1 - par, e)

        first = first_ref[b * (N_EXPERTS + 1) + e]
        start = start_ref[b * N_EXPERTS + e]

        def send(c, carry, e=e, first=first, start=start):
            _slot_copy(stage, xs_ref, sem, par, e, first + c * MOE_CHUNK, start + c * MOE_CHUNK,
                       MOE_CHUNK).start(priority=e % 2)
            return carry

        lax.fori_loop(0, _run_chunks(first_ref, b, e), send, 0)

    @pl.when(b == nb - 1)
    def _():
        for e in range(N_EXPERTS):
            wait_chunks(b, par, e)
        stage[1 - par, :MOE_BLOCK, :] = jnp.zeros((MOE_BLOCK, stage.shape[2]), F32)
        for e in range(N_EXPERTS):
            _slot_copy(stage, xs_ref, sem, 1 - par, e, 0, start_ref[nb * N_EXPERTS + e], MOE_BLOCK).start()
        for e in range(N_EXPERTS):
            _slot_copy(stage, xs_ref, sem, 1 - par, e, 0, 0, MOE_BLOCK).wait()
        for g in range(N_EXPERTS + 1):
            lo = fill_ref[g]
            hi = fill_ref[N_EXPERTS + 1 + g]
            tiles = (hi - lo) // MOE_BLOCK

            def fill(t, carry, lo=lo):
                _slot_copy(stage, xs_ref, sem, 1 - par, 0, 0, lo + t * MOE_BLOCK, MOE_BLOCK).start()
                return carry

            def drain(t, carry):
                _slot_copy(stage, xs_ref, sem, 1 - par, 0, 0, 0, MOE_BLOCK).wait()
                return carry

            lax.fori_loop(0, tiles, fill, 0)
            lax.fori_loop(0, tiles, drain, 0)

            @pl.when(lo + tiles * MOE_BLOCK < hi)
            def _():
                _slot_copy(stage, xs_ref, sem, 1 - par, 0, 0, hi - MOE_BLOCK, MOE_BLOCK).start()
                _slot_copy(stage, xs_ref, sem, 1 - par, 0, 0, 0, MOE_BLOCK).wait()


def _dispatch(start, first, fill, u2, route_t, n_slots):
    n, d = u2.shape
    tb = MOE_BLOCK
    return pl.pallas_call(
        _dispatch_kernel,
        out_shape=jax.ShapeDtypeStruct((n_slots, d), F32),
        grid_spec=pltpu.PrefetchScalarGridSpec(
            num_scalar_prefetch=3,
            grid=(n // tb,),
            in_specs=[pl.BlockSpec((tb, d), lambda b, s, f, z: (b, 0)),
                      pl.BlockSpec((SUBLANES, tb), lambda b, s, f, z: (0, b))],
            out_specs=pl.BlockSpec(memory_space=pl.ANY),
            scratch_shapes=[pltpu.VMEM((2, BLOCK_ROWS + tb, d), F32),
                            pltpu.SemaphoreType.DMA((2, N_EXPERTS))]),
        compiler_params=_cparams(("arbitrary",)),
        name="moe_dispatch",
    )(start, first, fill, u2, route_t)


def _expert_kernel(te_ref, tn_ref, tf_ref, x_ref, us_ref, comb_ref, w1_hbm, w2_hbm, o_ref, os_ref,
                   xb, wgr, wur, w2r, sg, su, s2, sem, *, layer, nf, fc):
    j = pl.program_id(0)
    n_sub = tn_ref[j]
    subs = o_ref.shape[0] // MOE_BLOCK
    f_hidden = nf * fc
    expert = layer * N_EXPERTS + te_ref[j]

    def chunk_copies(c, slot):
        col = pl.multiple_of(c * fc, LANES)
        return (pltpu.make_async_copy(w1_hbm.at[expert, :, pl.ds(col, fc)], sg.at[slot], sem.at[0, slot]),
                pltpu.make_async_copy(w1_hbm.at[expert, :, pl.ds(pl.multiple_of(f_hidden + col, LANES), fc)],
                                      su.at[slot], sem.at[1, slot]),
                pltpu.make_async_copy(w2_hbm.at[expert, pl.ds(col, fc), :], s2.at[slot], sem.at[2, slot]))

    @pl.when(j == 0)
    def _():
        os_ref[...] = jnp.zeros_like(os_ref)

    full_resident = (tf_ref[j] == 0) & (n_sub == subs)

    @pl.when(jnp.logical_not(full_resident))
    def _():
        o_ref[...] = jnp.zeros_like(o_ref)

    def cast(i, carry):
        rows = pl.ds(pl.multiple_of(i * MOE_BLOCK, MOE_BLOCK), MOE_BLOCK)
        xb[rows, :] = x_ref[rows, :].astype(BF16)
        return carry

    lax.fori_loop(0, n_sub, cast, 0)

    def swiglu(x, f):
        g = jnp.dot(x, wgr[f], preferred_element_type=F32)
        up = jnp.dot(x, wur[f], preferred_element_type=F32)
        h = (_silu(g) * up).astype(BF16)
        return jnp.dot(h, w2r[f], preferred_element_type=F32)

    def ffn(rows, f):
        o_ref[rows, :] += swiglu(xb[rows, :], f)

    def ffn_blocks(f, lo):
        def sub(i, carry):
            ffn(pl.ds(pl.multiple_of(i * MOE_BLOCK, MOE_BLOCK), MOE_BLOCK), f)
            return carry

        lax.fori_loop(lo, n_sub, sub, 0)

    @pl.when(tf_ref[j] == 1)
    def _():
        comb = comb_ref[...]
        lane = lax.broadcasted_iota(jnp.int32, comb.shape, 1)
        gate = jnp.sum(jnp.where(lane == te_ref[j], comb, 0.0), axis=1, keepdims=True)
        for cp in chunk_copies(0, 0):
            cp.start()

        def stream(c, carry):
            slot = c % 2

            @pl.when(c + 1 < nf)
            def _():
                for cp in chunk_copies(c + 1, 1 - slot):
                    cp.start()

            for cp in chunk_copies(c, slot):
                cp.wait()
            wgr[c] = sg[slot].astype(BF16)
            wur[c] = su[slot].astype(BF16)
            w2r[c] = s2[slot].astype(BF16)

            @pl.when(n_sub == subs)
            def _():
                y = swiglu(jnp.concatenate([xb[...], us_ref[...]], axis=0), c)
                o_ref[...] += y[:o_ref.shape[0]]
                os_ref[...] += gate * y[o_ref.shape[0]:]

            @pl.when((n_sub > 0) & (n_sub < subs))
            def _():
                y = swiglu(jnp.concatenate([xb[:MOE_BLOCK, :], us_ref[...]], axis=0), c)
                o_ref[:MOE_BLOCK, :] += y[:MOE_BLOCK]
                os_ref[...] += gate * y[MOE_BLOCK:]
                ffn_blocks(c, 1)

            @pl.when(n_sub == 0)
            def _():
                os_ref[...] += gate * swiglu(us_ref[...], c)

            return carry

        lax.fori_loop(0, nf, stream, 0)

    @pl.when(full_resident)
    def _():
        o_ref[...] = swiglu(xb[...], 0)

        def chunk(f, carry):
            ffn(slice(None), f)
            return carry

        lax.fori_loop(1, nf, chunk, 0)

    @pl.when((tf_ref[j] == 0) & (n_sub > 0) & (n_sub < subs))
    def _():
        def chunk(f, carry):
            ffn_blocks(f, 0)
            return carry

        lax.fori_loop(0, nf, chunk, 0)


def _experts(tile_e, tile_n, tile_first, xs, u2_rows, comb_rows, w_e1, w_e2, layer, *, fc):
    n_slots, d = xs.shape
    f_hidden = w_e2.shape[1]
    nf = f_hidden // fc
    tg = MOE_TILE
    full = lambda a: pl.BlockSpec(a.shape, lambda j, te, tn, tf: (0,) * a.ndim)
    return pl.pallas_call(
        functools.partial(_expert_kernel, layer=layer, nf=nf, fc=fc),
        out_shape=[jax.ShapeDtypeStruct((n_slots, d), F32), jax.ShapeDtypeStruct(u2_rows.shape, F32)],
        grid_spec=pltpu.PrefetchScalarGridSpec(
            num_scalar_prefetch=3,
            grid=(n_slots // tg,),
            in_specs=[pl.BlockSpec((tg, d), lambda j, te, tn, tf: (j, 0)),
                      full(u2_rows), full(comb_rows),
                      pl.BlockSpec(memory_space=pl.ANY), pl.BlockSpec(memory_space=pl.ANY)],
            out_specs=[pl.BlockSpec((tg, d), lambda j, te, tn, tf: (j, 0)), full(u2_rows)],
            scratch_shapes=[pltpu.VMEM((tg, d), BF16), pltpu.VMEM((nf, d, fc), BF16),
                            pltpu.VMEM((nf, d, fc), BF16), pltpu.VMEM((nf, fc, d), BF16),
                            pltpu.VMEM((2, d, fc), F32), pltpu.VMEM((2, d, fc), F32),
                            pltpu.VMEM((2, fc, d), F32), pltpu.SemaphoreType.DMA((3, 2))]),
        compiler_params=_cparams(("arbitrary",), EXPERT_VMEM_LIMIT),
        name="moe_experts",
    )(tile_e, tile_n, tile_first, xs, u2_rows, comb_rows, w_e1, w_e2)


def _fetch_copy(ys_ref, buf, sem, slot, e, row, start):
    src = ys_ref.at[pl.ds(pl.multiple_of(start, SUBLANES), MOE_CHUNK), :]
    dst = buf.at[slot, e, pl.ds(pl.multiple_of(row, MOE_CHUNK), MOE_CHUNK), :]
    return pltpu.make_async_copy(src, dst, sem.at[slot, e])


def _combine_kernel(start_ref, first_ref, rt_ref, ys_ref, x1_ref, mod_ref, lg_ref, lb_ref, o_ref, buf, yc, sem,
                    *, alpha):
    b = pl.program_id(0)
    nb = pl.num_programs(0)

    def fetch(blk, slot):
        for e in range(N_EXPERTS):
            start = start_ref[blk * N_EXPERTS + e]

            def get(c, carry, e=e, start=start):
                _fetch_copy(ys_ref, buf, sem, slot, e, c * MOE_CHUNK, start + c * MOE_CHUNK).start(priority=e % 2)
                return carry

            lax.fori_loop(0, _run_chunks(first_ref, blk, e), get, 0)

    @pl.when(b == 0)
    def _():
        fetch(0, 0)
        yc[...] = jnp.zeros_like(yc)

    @pl.when(b + 1 < nb)
    def _():
        fetch(b + 1, (b + 1) % 2)

    cur = b % 2
    tb = x1_ref.shape[0]
    for e in range(N_EXPERTS):
        def landed(c, carry, e=e):
            _fetch_copy(ys_ref, buf, sem, cur, e, 0, 0).wait()
            return carry

        lax.fori_loop(0, _run_chunks(first_ref, b, e), landed, 0)
        first = first_ref[b * (N_EXPERTS + 1) + e]
        groups = (first_ref[b * (N_EXPERTS + 1) + e + 1] - first) // BLOCK_ALIGN

        def pack(g, carry, e=e, first=first):
            src = pl.ds(pl.multiple_of(g * BLOCK_ALIGN, BLOCK_ALIGN), BLOCK_ALIGN)
            dst = pl.ds(pl.multiple_of(first + g * BLOCK_ALIGN, BLOCK_ALIGN), BLOCK_ALIGN)
            yc[dst, :] = buf[cur, e, src, :].astype(BF16)
            return carry

        lax.fori_loop(0, groups, pack, 0)

    slot = lax.broadcasted_iota(jnp.int32, (BLOCK_ROWS, tb), 0).astype(F32)
    gate_t = (jnp.where(slot == rt_ref[0:1, :], rt_ref[2:3, :], 0.0)
              + jnp.where(slot == rt_ref[1:2, :], rt_ref[3:4, :], 0.0)).astype(BF16)
    acc = lax.dot_general(gate_t, yc[...], (((0,), (0,)), ((), ())), preferred_element_type=F32)
    o_ref[...] = _post_norm2(x1_ref[...], acc, mod_ref, lg_ref, lb_ref, False, alpha)


def _combine(start, first, route_t, ys, x1, mod, ln_g, ln_b, *, tiles_per_seq, alpha):
    n, d = x1.shape
    tb = MOE_BLOCK
    full = lambda a: pl.BlockSpec(a.shape, lambda b, s, f: (0,) * a.ndim)
    return pl.pallas_call(
        functools.partial(_combine_kernel, alpha=alpha),
        out_shape=jax.ShapeDtypeStruct((n, d), F32),
        grid_spec=pltpu.PrefetchScalarGridSpec(
            num_scalar_prefetch=2,
            grid=(n // tb,),
            in_specs=[pl.BlockSpec((SUBLANES, tb), lambda b, s, f: (0, b)),
                      pl.BlockSpec(memory_space=pl.ANY),
                      pl.BlockSpec((tb, d), lambda b, s, f: (b, 0)),
                      pl.BlockSpec((1, 6, d), lambda b, s, f: (b // tiles_per_seq, 0, 0)),
                      full(ln_g), full(ln_b)],
            out_specs=pl.BlockSpec((tb, d), lambda b, s, f: (b, 0)),
            scratch_shapes=[pltpu.VMEM((2, N_EXPERTS, tb, d), F32), pltpu.VMEM((BLOCK_ROWS, d), BF16),
                            pltpu.SemaphoreType.DMA((2, N_EXPERTS))]),
        compiler_params=_cparams(("arbitrary",)),
        name="moe_combine",
    )(start, first, route_t, ys, x1, mod, ln_g, ln_b)


def _routing_tables(cum_t, n_tokens, n_slots):
    ne, tb, tg = N_EXPERTS, MOE_BLOCK, MOE_TILE
    nb = n_tokens // tb
    cum = cum_t.reshape(cum_t.shape[0], nb, LANES)[:ne, :, 0].T.astype(jnp.int32)
    total = cum[-1]
    base = jnp.concatenate([jnp.zeros((1, ne), jnp.int32), cum[:-1]], axis=0)
    region = (total + tb + tg - 1) // tg * tg
    end = jnp.cumsum(region)
    off = end - region
    start = jnp.concatenate([(off[None, :] + base).reshape(-1), off + total])
    run = (cum - base + BLOCK_ALIGN - 1) // BLOCK_ALIGN * BLOCK_ALIGN
    first = jnp.concatenate([jnp.zeros((nb, 1), jnp.int32), jnp.cumsum(run, axis=1)], axis=1).reshape(-1)
    fill = jnp.concatenate([off + total + tb, end[-1:], end, jnp.full((1,), n_slots, jnp.int32)]).astype(jnp.int32)
    tile_row = jnp.arange(n_slots // tg, dtype=jnp.int32) * tg
    tile_e = jnp.minimum(jnp.sum(tile_row[:, None] >= end[None, :], axis=1), ne - 1).astype(jnp.int32)
    left = total[tile_e] - (tile_row - off[tile_e])
    tile_n = jnp.clip((left + tb - 1) // tb, 0, tg // tb).astype(jnp.int32)
    tile_first = (tile_row == off[tile_e]).astype(jnp.int32)
    return start, first, fill, tile_e, tile_n, tile_first


def kernel(x_prompt, x_sample, cache_k, cache_v, state_conv, c_prompt, c_sample, w_ada, b_ada, w_in,
           conv_w, g_conv_out, g_att_out, w_out, ln1_g, ln1_b, ln2_g, ln2_b, w_ff1, w_ff2, w_router,
           w_e1, w_e2):
    bsz, s_len, d = x_prompt.shape
    nb_s, t_new, _ = x_sample.shape
    depth = w_in.shape[0]
    dc = conv_w.shape[2]
    da = w_in.shape[2] // 3 - dc
    assert t_new == 1 and da == N_HEADS * HEAD_DIM and da // LANES * LANES == da
    assert s_len % (Q_BLOCK * max(dl for _, dl in DILATED_PATTERNS)) == 0
    assert cache_k.shape[2] == max(w for w, _ in DILATED_PATTERNS)
    alpha = (2 * depth) ** 0.25
    n_p = bsz * s_len
    keep = min(cache_k.shape[2], s_len)
    n_exp, f_exp = w_e2.shape[1], w_e2.shape[2]
    assert n_exp == N_EXPERTS
    n_slots = n_p * TOP_K + n_exp * ((n_p // MOE_BLOCK) * (SUBLANES - 1) + MOE_BLOCK + MOE_TILE - 1)
    n_slots = (n_slots + MOE_TILE - 1) // MOE_TILE * MOE_TILE

    slopes = jnp.exp2(-8.0 * jnp.arange(1, N_HEADS + 1, dtype=F32) / N_HEADS)
    slopes_pairs = jnp.broadcast_to(slopes.reshape(N_HEADS // 2, 2, 1), (N_HEADS // 2, 2, 2 * Q_BLOCK))
    slopes_col = slopes.reshape(N_HEADS, 1, 1)
    w_buf = cache_k.shape[2]
    cache_kt = cache_k.transpose(0, 1, 3, 4, 2).reshape(depth * nb_s, N_HEADS, HEAD_DIM, w_buf)
    cache_vt = cache_v.transpose(0, 1, 3, 4, 2).reshape(depth * nb_s, N_HEADS, HEAD_DIM, w_buf)

    rows_c = (bsz + nb_s + 7) // 8 * 8
    c_all = jnp.zeros((rows_c, d), F32).at[:bsz].set(c_prompt).at[bsz:bsz + nb_s].set(c_sample)
    ada = _adaln_all(c_all, w_ada, b_ada)

    w_e1f = w_e1.reshape((-1,) + w_e1.shape[2:])
    w_e2f = w_e2.reshape((-1,) + w_e2.shape[2:])

    w_in_b = w_in.astype(BF16)
    w_out_b = w_out.astype(BF16)
    xp = x_prompt.reshape(n_p, d)
    xs = x_sample.reshape(nb_s, d)
    tm_p = PROJ_ROWS
    assert (s_len - keep) % tm_p == 0 and tm_p % MOE_BLOCK == 0 and s_len % FFN_ROWS == 0
    kt_all = jnp.zeros((depth, bsz, da, keep), F32)
    vt_all = jnp.zeros((depth, bsz, da, keep), F32)
    outs = {k: [] for k in ("cp", "ks", "vs", "cs")}
    row2 = lambda a: a.reshape(1, -1)
    for l in range(depth):
        moe = l % 2 == 1
        li = l // 2
        mod_p = ada[l, :bsz].reshape(bsz, 6, d)
        mod_s = ada[l, bsz:bsz + nb_s].reshape(nb_s, 6, d).transpose(1, 0, 2)
        lnp = (row2(ln1_g[l]), row2(ln1_b[l]))
        ln2 = (row2(ln2_g[l]), row2(ln2_b[l]))
        norm_w = (conv_w[l], row2(g_conv_out[l]), row2(g_att_out[l]))

        bg, z, q, k, v = _in_proj_sample(xs, mod_s, w_in_b, l)
        att = _sample_attention(q, k, v, cache_kt, cache_vt, l, slopes_col)
        st = state_conv[l]
        wr = jnp.zeros((d, LANES), F32).at[:, :n_exp].set(w_router[li]).astype(BF16) if moe else None
        x1_s, u2_s, *comb_s = _out_proj(bg, z, st[:, 1], st[:, 0], att, xs, mod_s, *norm_w, w_out_b, *lnp, wr,
                                        layer=l, per_row=True, tm=nb_s, tiles_per_seq=1, alpha=alpha)
        outs["ks"].append(k.reshape(nb_s, 1, N_HEADS, HEAD_DIM))
        outs["vs"].append(v.reshape(nb_s, 1, N_HEADS, HEAD_DIM))
        outs["cs"].append(jnp.stack([st[:, 1], z], axis=1))

        bg, z, qe, qo, k, v, kt_all, vt_all = _in_proj_prompt(xp, mod_p, w_in_b, kt_all, vt_all, l,
                                                              tm=tm_p, s_len=s_len)
        att = _prompt_attention(qe, qo, k, v, slopes_pairs, bsz, s_len)
        if moe:
            wr_t = jnp.zeros((16, d), F32).at[:n_exp].set(w_router[li].T).astype(BF16)
            x1, u2, route_t, cum_t = _out_proj(
                bg, z, z, z, att, xp, mod_p, *norm_w, w_out_b, *lnp, wr_t,
                layer=l, per_row=False, tm=tm_p, tiles_per_seq=s_len // tm_p, alpha=alpha)
            start, first, fill, tile_e, tile_n, tile_first = _routing_tables(cum_t, n_p, n_slots)
            xsorted = _dispatch(start, first, fill, u2, route_t, n_slots)
            ysorted, mix_s = _experts(tile_e, tile_n, tile_first, xsorted, u2_s, comb_s[0], w_e1f, w_e2f, li,
                                      fc=EXPERT_CHUNK)
            xp = _combine(start, first, route_t, ysorted, x1, mod_p, *ln2,
                          tiles_per_seq=s_len // MOE_BLOCK, alpha=alpha)
            xs = _post_norm_rows(x1_s, mix_s, mod_s, *ln2, alpha=alpha)
        else:
            x1, u2 = _out_proj(bg, z, z, z, att, xp, mod_p, *norm_w, w_out_b, *lnp, None,
                               layer=l, per_row=False, tm=tm_p, tiles_per_seq=s_len // tm_p, alpha=alpha)
            xp = _dense_ffn(u2, w_ff1, w_ff2, li, x1, mod_p, *ln2, per_row=False, tm=FFN_ROWS,
                            tiles_per_seq=s_len // FFN_ROWS, fc=DENSE_CHUNK, alpha=alpha)
            xs = _dense_ffn(u2_s, w_ff1, w_ff2, li, x1_s, mod_s, *ln2, per_row=True, tm=nb_s,
                            tiles_per_seq=1, fc=DENSE_CHUNK, alpha=alpha)
        outs["cp"].append(z.reshape(bsz, s_len, dc)[:, s_len - (CONV_WIDTH - 1):])

    to_rows = lambda t: t.reshape(depth, bsz, N_HEADS, HEAD_DIM, keep).transpose(0, 1, 4, 2, 3)
    return (xp.reshape(bsz, s_len, d), xs.reshape(nb_s, 1, d),
            to_rows(kt_all), to_rows(vt_all), jnp.stack(outs["cp"]),
            jnp.stack(outs["ks"]), jnp.stack(outs["vs"]), jnp.stack(outs["cs"]))
```
